```python
import jax, jax.numpy as jnp
from jax import lax
import numpy as np

D_MODEL = 1024
BATCH = 2
SEQ = 8192
DEPTH = 2
DEC_BATCH = 32
DEC_SEQ = 8
PAST_LEN = 8192
PAGE_SIZE = 128

N_A_LAYERS = DEPTH // 2
N_B_LAYERS = DEPTH - N_A_LAYERS
D_RNN = D_MODEL
N_LRU_BLOCKS = 8
LRU_BLOCK = D_RNN // N_LRU_BLOCKS
CONV_WIDTH = 4
LRU_C = 8.0
N_HEADS = 8
HEAD_DIM = D_MODEL // N_HEADS
HD = N_HEADS * HEAD_DIM
Q_BLOCK = 128
N_GROUPS = 4
EXPERTS_PER_GROUP = 4
N_EXPERTS = N_GROUPS * EXPERTS_PER_GROUP
TOP_K_IN_GROUP = 2
D_EXPERT = D_MODEL // 2
EPS = 1e-6

kernel_name = "yoco_hawk_fox_hmoe_step"

F32 = jnp.float32


def rmsnorm(x, g):
    x32 = x.astype(F32)
    y = x32 * lax.rsqrt(jnp.mean(x32 * x32, axis=-1, keepdims=True) + EPS)
    return (y * g.astype(F32)).astype(x.dtype)


def modulate(h, shift, scale):
    return h * (1 + scale[:, None, :]) + shift[:, None, :]


def rglru_block(h, conv0, h0, w_in, conv_w, conv_b, w_a, b_a, w_x, b_x, lam, w_out):
    B, T, _ = h.shape
    xb, gb = jnp.split(h @ w_in, 2, axis=-1)
    xpad = jnp.concatenate([conv0.astype(xb.dtype), xb], axis=1)
    xc = conv_b + sum(conv_w[j] * xpad[:, j:j + T] for j in range(CONV_WIDTH))
    new_conv = xpad[:, -(CONV_WIDTH - 1):]
    xblk = xc.reshape(B, T, N_LRU_BLOCKS, LRU_BLOCK)
    r = jax.nn.sigmoid(jnp.einsum('btnd,nde->btne', xblk, w_a).reshape(B, T, D_RNN) + b_a)
    i = jax.nn.sigmoid(jnp.einsum('btnd,nde->btne', xblk, w_x).reshape(B, T, D_RNN) + b_x)
    log_a = (LRU_C * r.astype(F32)) * jax.nn.log_sigmoid(lam.astype(F32))
    a = jnp.exp(log_a)
    bterm = jnp.sqrt(-jnp.expm1(2.0 * log_a)) * (i * xc).astype(F32)
    bterm = bterm.at[:, 0].add(a[:, 0] * h0.astype(F32))

    def combine(left, right):
        a1, b1 = left
        a2, b2 = right
        return a1 * a2, a2 * b1 + b2

    _, hs = lax.associative_scan(combine, (a, bterm), axis=1)
    y = hs.astype(gb.dtype) * jax.nn.gelu(gb)
    return y @ w_out, hs[:, -1].astype(h.dtype), new_conv


def hier_moe(h, w_grp, b_grp, w_exp, b_exp, w_up, w_down):
    B, T, _ = h.shape
    glog = (h @ w_grp + b_grp).astype(F32)
    gprob = jax.nn.softmax(glog, axis=-1)
    g_sel = jnp.argmax(glog, axis=-1)
    p_grp = jnp.max(gprob, axis=-1, keepdims=True)
    elog = (h @ w_exp + b_exp).astype(F32).reshape(B, T, N_GROUPS, EXPERTS_PER_GROUP)
    elog_g = jnp.einsum('btge,btg->bte', elog, jax.nn.one_hot(g_sel, N_GROUPS, dtype=F32))
    top_v, top_i = lax.top_k(elog_g, TOP_K_IN_GROUP)
    w_top = jax.nn.softmax(top_v, axis=-1) * p_grp
    e_idx = g_sel[..., None] * EXPERTS_PER_GROUP + top_i
    gates = jnp.sum(jax.nn.one_hot(e_idx, N_EXPERTS, dtype=F32) * w_top[..., None], axis=-2)
    gates = gates.astype(h.dtype)
    y = jnp.zeros_like(h)
    for e in range(N_EXPERTS):
        g_, u_ = jnp.split(h @ w_up[e], 2, axis=-1)
        y = y + gates[..., e:e + 1] * ((jax.nn.silu(g_) * u_) @ w_down[e])
    return y


def shared_kv(h, c, kv_g, kv_w_mod, kv_b_mod, kv_w, kv_b_f):
    B, T, _ = h.shape
    shift, scale = jnp.split(c @ kv_w_mod + kv_b_mod, 2, axis=-1)
    hn = modulate(rmsnorm(h, kv_g), shift, scale)
    kvf = hn @ kv_w
    k = kvf[..., :HD].reshape(B, T, N_HEADS, HEAD_DIM)
    v = kvf[..., HD:2 * HD].reshape(B, T, N_HEADS, HEAD_DIM)
    logf = jax.nn.log_sigmoid((kvf[..., 2 * HD:] + kv_b_f).astype(F32))
    return k, v, logf


def fox_attend(q, k, v, fq, fk, qpos, kpos):
    s = jnp.einsum('bqhd,bkhd->bhqk', q, k).astype(F32) * (HEAD_DIM ** -0.5)
    s = s + jnp.transpose(fq, (0, 2, 1))[..., :, None] - jnp.transpose(fk, (0, 2, 1))[..., None, :]
    s = jnp.where(kpos[None, :] <= qpos[:, None], s, -jnp.inf)
    p = jax.nn.softmax(s, axis=-1)
    return jnp.einsum('bhqk,bkhd->bqhd', p.astype(v.dtype), v)


def fox_prompt(q, k, v, fcum):
    B, T, H, Dh = q.shape
    nb = T // Q_BLOCK
    qb = jnp.transpose(q.reshape(B, nb, Q_BLOCK, H, Dh), (1, 0, 2, 3, 4))
    fb = jnp.transpose(fcum.reshape(B, nb, Q_BLOCK, H), (1, 0, 2, 3))
    pos = jnp.arange(T, dtype=jnp.int32)
    pb = pos.reshape(nb, Q_BLOCK)
    out = lax.map(lambda a: fox_attend(a[0], k, v, a[1], fcum, a[2], pos), (qb, fb, pb))
    return jnp.transpose(out, (1, 0, 2, 3, 4)).reshape(B, T, H, Dh)


def run_trunk(x, c, conv0, h0, past, p):
    B, T, _ = x.shape
    new_h, new_conv = [], []
    kv = None
    for layer in range(DEPTH):
        sh1, sc1, g1, sh2, sc2, g2 = jnp.split(c @ p['w_mod'][layer] + p['b_mod'][layer], 6, axis=-1)
        hn = modulate(rmsnorm(x, p['g_mix'][layer]), sh1, sc1)
        if layer < N_A_LAYERS:
            a = layer
            y, hT, cT = rglru_block(hn, conv0[a], h0[a], p['lru_w_in'][a], p['lru_conv_w'][a],
                                    p['lru_conv_b'][a], p['lru_w_a'][a], p['lru_b_a'][a],
                                    p['lru_w_x'][a], p['lru_b_x'][a], p['lru_lambda'][a],
                                    p['lru_w_out'][a])
            new_h.append(hT)
            new_conv.append(cT)
        else:
            b = layer - N_A_LAYERS
            k, v, lf = kv
            q = (hn @ p['attn_w_q'][b]).reshape(B, T, N_HEADS, HEAD_DIM)
            if past is None:
                o = fox_prompt(q, k, v, jnp.cumsum(lf, axis=1))
            else:
                kp, vp, lfp = past
                P = kp.shape[1]
                k_all = jnp.concatenate([kp.astype(k.dtype), k], axis=1)
                v_all = jnp.concatenate([vp.astype(v.dtype), v], axis=1)
                fcum = jnp.cumsum(jnp.concatenate([lfp.astype(F32), lf], axis=1), axis=1)
                qpos = P + jnp.arange(T, dtype=jnp.int32)
                kpos = jnp.arange(P + T, dtype=jnp.int32)
                o = fox_attend(q, k_all, v_all, fcum[:, P:], fcum, qpos, kpos)
            y = o.reshape(B, T, HD) @ p['attn_w_o'][b]
        x = x + g1[:, None, :] * y
        hn = modulate(rmsnorm(x, p['g_moe'][layer]), sh2, sc2)
        x = x + g2[:, None, :] * hier_moe(hn, p['moe_w_grp'][layer], p['moe_b_grp'][layer],
                                          p['moe_w_exp'][layer], p['moe_b_exp'][layer],
                                          p['moe_w_up'][layer], p['moe_w_down'][layer])
        if layer == N_A_LAYERS - 1:
            kv = shared_kv(x, c, p['kv_g'], p['kv_w_mod'], p['kv_b_mod'], p['kv_w'], p['kv_b_f'])
    y_out = rmsnorm(x, p['g_final'])
    return y_out, jnp.stack(new_h, axis=0), jnp.stack(new_conv, axis=0), kv


def setup_inputs(seed: int = 0) -> dict:
    key = jax.random.key(seed)
    ks = iter(jax.random.split(key, 64))
    nrm = lambda shape, s: jax.random.normal(next(ks), shape, F32) * s
    n_pages = PAST_LEN // PAGE_SIZE
    n_used = DEC_BATCH * n_pages
    n_pool = n_used + max(1, n_used // 4)
    d = D_MODEL
    out = {}
    out['x_prompt'] = nrm((BATCH, SEQ, d), 1.0)
    out['x_sample'] = nrm((DEC_BATCH, DEC_SEQ, d), 1.0)
    out['state_h'] = nrm((N_A_LAYERS, DEC_BATCH, D_RNN), 0.5)
    out['state_conv'] = nrm((N_A_LAYERS, DEC_BATCH, CONV_WIDTH - 1, D_RNN), 1.0)
    out['cache_k'] = nrm((n_pool, PAGE_SIZE, N_HEADS, HEAD_DIM), 1.0)
    out['cache_v'] = nrm((n_pool, PAGE_SIZE, N_HEADS, HEAD_DIM), 1.0)
    out['cache_logf'] = jax.nn.log_sigmoid(nrm((n_pool, PAGE_SIZE, N_HEADS), 1.0) + 2.5)
    perm = jax.random.permutation(next(ks), n_pool)
    out['page_table'] = perm[:n_used].reshape(DEC_BATCH, n_pages).astype(jnp.int32)
    out['c_prompt'] = nrm((BATCH, d), 1.0)
    out['c_sample'] = nrm((DEC_BATCH, d), 1.0)
    out['w_mod'] = nrm((DEPTH, d, 6 * d), 0.3 * d ** -0.5)
    out['b_mod'] = nrm((DEPTH, 6 * d), 0.1)
    out['g_mix'] = 1.0 + nrm((DEPTH, d), 0.05)
    out['g_moe'] = 1.0 + nrm((DEPTH, d), 0.05)
    out['lru_w_in'] = nrm((N_A_LAYERS, d, 2 * D_RNN), d ** -0.5)
    out['lru_conv_w'] = nrm((N_A_LAYERS, CONV_WIDTH, D_RNN), CONV_WIDTH ** -0.5)
    out['lru_conv_b'] = nrm((N_A_LAYERS, D_RNN), 0.02)
    out['lru_w_a'] = nrm((N_A_LAYERS, N_LRU_BLOCKS, LRU_BLOCK, LRU_BLOCK), LRU_BLOCK ** -0.5)
    out['lru_b_a'] = nrm((N_A_LAYERS, D_RNN), 0.02)
    out['lru_w_x'] = nrm((N_A_LAYERS, N_LRU_BLOCKS, LRU_BLOCK, LRU_BLOCK), LRU_BLOCK ** -0.5)
    out['lru_b_x'] = nrm((N_A_LAYERS, D_RNN), 0.02)
    a0 = jax.random.uniform(next(ks), (N_A_LAYERS, D_RNN), F32, 0.9, 0.999)
    pa = a0 ** (1.0 / LRU_C)
    out['lru_lambda'] = jnp.log(pa) - jnp.log1p(-pa)
    out['lru_w_out'] = nrm((N_A_LAYERS, D_RNN, d), D_RNN ** -0.5)
    out['kv_g'] = 1.0 + nrm((d,), 0.05)
    out['kv_w_mod'] = nrm((d, 2 * d), 0.3 * d ** -0.5)
    out['kv_b_mod'] = nrm((2 * d,), 0.1)
    out['kv_w'] = nrm((d, 2 * HD + N_HEADS), d ** -0.5)
    out['kv_b_f'] = jax.random.uniform(next(ks), (N_HEADS,), F32, 1.0, 4.0)
    out['attn_w_q'] = nrm((N_B_LAYERS, d, HD), d ** -0.5)
    out['attn_w_o'] = nrm((N_B_LAYERS, HD, d), HD ** -0.5)
    out['moe_w_grp'] = nrm((DEPTH, d, N_GROUPS), d ** -0.5)
    out['moe_b_grp'] = nrm((DEPTH, N_GROUPS), 0.01)
    out['moe_w_exp'] = nrm((DEPTH, d, N_EXPERTS), d ** -0.5)
    out['moe_b_exp'] = nrm((DEPTH, N_EXPERTS), 0.01)
    out['moe_w_up'] = nrm((DEPTH, N_EXPERTS, d, 2 * D_EXPERT), d ** -0.5)
    out['moe_w_down'] = nrm((DEPTH, N_EXPERTS, D_EXPERT, d), D_EXPERT ** -0.5)
    out['g_final'] = 1.0 + nrm((d,), 0.05)
    return out


def reference(x_prompt, x_sample, state_h, state_conv, cache_k, cache_v, cache_logf, page_table,
              c_prompt, c_sample, w_mod, b_mod, g_mix, g_moe, lru_w_in, lru_conv_w, lru_conv_b,
              lru_w_a, lru_b_a, lru_w_x, lru_b_x, lru_lambda, lru_w_out, kv_g, kv_w_mod, kv_b_mod,
              kv_w, kv_b_f, attn_w_q, attn_w_o, moe_w_grp, moe_b_grp, moe_w_exp, moe_b_exp,
              moe_w_up, moe_w_down, g_final):
    p = dict(w_mod=w_mod, b_mod=b_mod, g_mix=g_mix, g_moe=g_moe, lru_w_in=lru_w_in,
             lru_conv_w=lru_conv_w, lru_conv_b=lru_conv_b, lru_w_a=lru_w_a, lru_b_a=lru_b_a,
             lru_w_x=lru_w_x, lru_b_x=lru_b_x, lru_lambda=lru_lambda, lru_w_out=lru_w_out,
             kv_g=kv_g, kv_w_mod=kv_w_mod, kv_b_mod=kv_b_mod, kv_w=kv_w, kv_b_f=kv_b_f,
             attn_w_q=attn_w_q, attn_w_o=attn_w_o, moe_w_grp=moe_w_grp, moe_b_grp=moe_b_grp,
             moe_w_exp=moe_w_exp, moe_b_exp=moe_b_exp, moe_w_up=moe_w_up,
             moe_w_down=moe_w_down, g_final=g_final)
    Bp = x_prompt.shape[0]
    conv0 = jnp.zeros((N_A_LAYERS, Bp, CONV_WIDTH - 1, D_RNN), x_prompt.dtype)
    h0 = jnp.zeros((N_A_LAYERS, Bp, D_RNN), x_prompt.dtype)
    y_prompt, h_prompt, conv_prompt, kv_p = run_trunk(x_prompt, c_prompt, conv0, h0, None, p)
    k_prompt, v_prompt, logf_prompt = kv_p
    Bs, n_pages = page_table.shape
    past_len = n_pages * cache_k.shape[1]
    k_past = cache_k[page_table].reshape(Bs, past_len, N_HEADS, HEAD_DIM)
    v_past = cache_v[page_table].reshape(Bs, past_len, N_HEADS, HEAD_DIM)
    lf_past = cache_logf[page_table].reshape(Bs, past_len, N_HEADS)
    y_sample, h_sample, conv_sample, kv_s = run_trunk(x_sample, c_sample, state_conv, state_h,
                                                      (k_past, v_past, lf_past), p)
    k_sample, v_sample, logf_sample = kv_s
    return (y_prompt, y_sample, h_prompt, conv_prompt, k_prompt, v_prompt, logf_prompt,
            h_sample, conv_sample, k_sample, v_sample, logf_sample)
```

```python
import functools
import math

import jax
import jax.numpy as jnp
from jax import lax
from jax.experimental import pallas as pl
from jax.experimental.pallas import tpu as pltpu

F32 = jnp.float32
BF16 = jnp.bfloat16
I32 = jnp.int32

D_MODEL = 1024
D_RNN = D_MODEL
N_LRU_BLOCKS = 8
LRU_BLOCK = D_RNN // N_LRU_BLOCKS
CONV_WIDTH = 4
LRU_C = 8.0
N_HEADS = 8
HEAD_DIM = D_MODEL // N_HEADS
N_GROUPS = 4
EXPERTS_PER_GROUP = 4
N_EXPERTS = N_GROUPS * EXPERTS_PER_GROUP
D_EXPERT = D_MODEL // 2
EPS = 1e-6
N_PAIRS = 6
N_CLASSES = N_GROUPS * N_PAIRS
PAIR_LO = (0, 0, 0, 1, 1, 2)
PAIR_HI = (1, 2, 3, 2, 3, 3)

SUBLANES = 8
LANES = 128
VMEM_LIMIT = 48 * 1024 * 1024

TOKEN_TILE = 256
EXPERT_TILE = 256
ROUTER_ROWS = 32
ATTN_TILE = 512
PAGES_PER_STEP = 4
MOD_TILE = 512

HIGHEST = lax.Precision.HIGHEST
NT_DIMS = (((1,), (1,)), ((), ()))


def _params(*semantics):
    return pltpu.CompilerParams(dimension_semantics=semantics, vmem_limit_bytes=VMEM_LIMIT)


def _norm_mod(x, g, shift, scale):
    y = x * lax.rsqrt(jnp.mean(x * x, axis=-1, keepdims=True) + EPS)
    return (y * g) * (1.0 + scale) + shift


def _log_sigmoid(x):
    return jnp.minimum(x, 0.0) - jnp.log1p(jnp.exp(-jnp.abs(x)))


def _sigmoid(x):
    return 1.0 / (1.0 + jnp.exp(-x))


def _gelu_tanh(x):
    c = math.sqrt(2.0 / math.pi)
    return x * (0.5 * (1.0 + jnp.tanh(c * (x + 0.044715 * (x * x * x)))))


def _mod_spec(arr, tm):
    if arr.shape[1] == 1:
        return pl.BlockSpec((1, 1, arr.shape[2]), lambda b, t: (b, 0, 0))
    return pl.BlockSpec((1, tm, arr.shape[2]), lambda b, t: (b, t, 0))


def _full_spec(arr):
    zeros = (0,) * arr.ndim
    return pl.BlockSpec(arr.shape, lambda *_: zeros)


def _mm_bias_kernel(x_ref, w_ref, b_ref, o_ref):
    o_ref[...] = jnp.dot(x_ref[...].astype(BF16), w_ref[...].astype(BF16),
                         preferred_element_type=F32) + b_ref[...]


def _mm_bias(x, w, b):
    m, d = x.shape
    n = w.shape[1]
    return pl.pallas_call(
        _mm_bias_kernel,
        grid=(n // MOD_TILE,),
        in_specs=[pl.BlockSpec((m, d), lambda j: (0, 0)),
                  pl.BlockSpec((d, MOD_TILE), lambda j: (0, j)),
                  pl.BlockSpec((1, MOD_TILE), lambda j: (0, j))],
        out_specs=pl.BlockSpec((m, MOD_TILE), lambda j: (0, j)),
        out_shape=jax.ShapeDtypeStruct((m, n), F32),
        compiler_params=_params("arbitrary"),
        name="mod_vectors",
    )(x, w, b.reshape(1, n))


def _lru_kernel(x_ref, sh_ref, sc_ref, gt_ref, gmix_ref, win_ref, cw_ref, cb_ref, wa_ref, ba_ref,
                wx_ref, bx_ref, lam_ref, wout_ref, prev_ref, h0_ref,
                xo_ref, hs_ref, xb_ref, *scratch, tm, seg, carry):
    x = x_ref[0]
    hn = _norm_mod(x, gmix_ref[...], sh_ref[0], sc_ref[0])
    xg = jnp.dot(hn.astype(BF16), win_ref[...], preferred_element_type=F32)
    xb = xg[:, :D_RNN]
    gb = xg[:, D_RNN:]

    if carry:
        prev_scr, h_scr = scratch

        @pl.when(pl.program_id(1) == 0)
        def _():
            prev_scr[...] = prev_ref[0]
            h_scr[...] = h0_ref[0]

        prev = prev_scr[...]
        hprev = h_scr[...]
    else:
        prev = prev_ref[0]
        hprev = h0_ref[0]

    row = lax.broadcasted_iota(I32, (tm, 1), 0)
    rseg = row & (seg - 1)
    nprev = prev.shape[0]

    cw = cw_ref[...]
    xc = cb_ref[...] + cw[CONV_WIDTH - 1:CONV_WIDTH] * xb
    for d in range(1, CONV_WIDTH):
        rolled = pltpu.roll(xb, d, axis=0)
        rp = pltpu.roll(prev, nprev - SUBLANES + d, axis=0)
        if carry:
            head = jnp.where(row[:SUBLANES] < d, rp, rolled[:SUBLANES])
            shifted = head if tm == SUBLANES else jnp.concatenate([head, rolled[SUBLANES:]], axis=0)
        else:
            shifted = jnp.where(rseg < d, rp, rolled)
        xc = xc + cw[CONV_WIDTH - 1 - d:CONV_WIDTH - d] * shifted

    xcb = xc.astype(BF16)
    ra = jnp.concatenate(
        [jnp.dot(xcb[:, n * LRU_BLOCK:(n + 1) * LRU_BLOCK], wa_ref[n], preferred_element_type=F32)
         for n in range(N_LRU_BLOCKS)], axis=1)
    rx = jnp.concatenate(
        [jnp.dot(xcb[:, n * LRU_BLOCK:(n + 1) * LRU_BLOCK], wx_ref[n], preferred_element_type=F32)
         for n in range(N_LRU_BLOCKS)], axis=1)
    r = _sigmoid(ra + ba_ref[...])
    ig = _sigmoid(rx + bx_ref[...])
    log_a = (LRU_C * r) * _log_sigmoid(lam_ref[...])
    a = jnp.exp(log_a)
    bt = jnp.sqrt(-jnp.tanh(log_a) * (a * a + 1.0)) * (ig * xc)

    s = 1
    while s < seg:
        inside = rseg >= s
        a_sh = jnp.where(inside, pltpu.roll(a, s, axis=0), 1.0)
        b_sh = jnp.where(inside, pltpu.roll(bt, s, axis=0), 0.0)
        bt = a * b_sh + bt
        a = a * a_sh
        s *= 2
    hs = bt + a * hprev

    if carry:
        h_scr[...] = hs[tm - 1:tm]
        prev_scr[...] = xb[tm - SUBLANES:]
        hs_ref[0] = hs[tm - 1:tm]
        xb_ref[0] = xb[tm - SUBLANES:]
    else:
        hs_ref[0] = hs
        xb_ref[0] = xb

    y = (hs * _gelu_tanh(gb)).astype(BF16)
    out = jnp.dot(y, wout_ref[...], preferred_element_type=F32)
    xo_ref[0] = x + gt_ref[0] * out


def _lru_layer(x, sh, sc, gt, p, prev, h0, *, tm, seg, carry):
    bsz, t_len, d = x.shape
    r = D_RNN
    nt = t_len // tm
    tok = pl.BlockSpec((1, tm, d), lambda b, t: (b, t, 0))
    if carry:
        state_specs = [pl.BlockSpec((1, SUBLANES, r), lambda b, t: (b, 0, 0)),
                       pl.BlockSpec((1, 1, r), lambda b, t: (b, 0, 0))]
        out_specs = [tok,
                     pl.BlockSpec((1, 1, r), lambda b, t: (b, 0, 0)),
                     pl.BlockSpec((1, SUBLANES, r), lambda b, t: (b, 0, 0))]
        out_shape = [jax.ShapeDtypeStruct((bsz, t_len, d), F32),
                     jax.ShapeDtypeStruct((bsz, 1, r), F32),
                     jax.ShapeDtypeStruct((bsz, SUBLANES, r), F32)]
        scratch = [pltpu.VMEM((SUBLANES, r), F32), pltpu.VMEM((1, r), F32)]
    else:
        rtok = pl.BlockSpec((1, tm, r), lambda b, t: (b, t, 0))
        state_specs = [rtok, rtok]
        out_specs = [tok, rtok, rtok]
        out_shape = [jax.ShapeDtypeStruct((bsz, t_len, d), F32),
                     jax.ShapeDtypeStruct((bsz, t_len, r), F32),
                     jax.ShapeDtypeStruct((bsz, t_len, r), F32)]
        scratch = []
    weights = [p["g_mix"], p["w_in"], p["conv_w"], p["conv_b"], p["w_a"], p["b_a"], p["w_x"],
               p["b_x"], p["lam"], p["w_out"]]
    return pl.pallas_call(
        functools.partial(_lru_kernel, tm=tm, seg=seg, carry=carry),
        grid=(bsz, nt),
        in_specs=[tok, _mod_spec(sh, tm), _mod_spec(sc, tm), _mod_spec(gt, tm)]
                 + [_full_spec(w) for w in weights] + state_specs,
        out_specs=out_specs,
        out_shape=out_shape,
        scratch_shapes=scratch,
        compiler_params=_params("arbitrary", "arbitrary"),
        name="rglru_layer",
    )(x, sh, sc, gt, *weights, prev, h0)


def _router_kernel(x_ref, sh_ref, sc_ref, g_ref, wr_ref, br_ref,
                   hn_ref, cls_ref, rank_ref, glo_ref, ghi_ref, cnt_ref, carry_scr, *, tm):
    @pl.when((pl.program_id(0) == 0) & (pl.program_id(1) == 0))
    def _():
        carry_scr[...] = jnp.zeros_like(carry_scr)

    hn = _norm_mod(x_ref[0], g_ref[...], sh_ref[0], sc_ref[0])
    hn_ref[0] = hn.astype(BF16)
    lt = lax.dot_general(wr_ref[...], hn, NT_DIMS, precision=HIGHEST,
                         preferred_element_type=F32) + br_ref[...]

    gl = [lt[k:k + 1] for k in range(N_GROUPS)]
    gmax = jnp.maximum(jnp.maximum(gl[0], gl[1]), jnp.maximum(gl[2], gl[3]))
    g_sel = jnp.where(gl[0] >= gmax, 0, jnp.where(gl[1] >= gmax, 1, jnp.where(gl[2] >= gmax, 2, 3)))
    p_grp = 1.0 / (jnp.exp(gl[0] - gmax) + jnp.exp(gl[1] - gmax)
                   + jnp.exp(gl[2] - gmax) + jnp.exp(gl[3] - gmax))

    def expert_logit(e):
        rows = [lt[N_GROUPS + g * EXPERTS_PER_GROUP + e:N_GROUPS + g * EXPERTS_PER_GROUP + e + 1]
                for g in range(N_GROUPS)]
        return jnp.where(g_sel == 0, rows[0],
                         jnp.where(g_sel == 1, rows[1], jnp.where(g_sel == 2, rows[2], rows[3])))

    es = [expert_logit(e) for e in range(EXPERTS_PER_GROUP)]

    def first_argmax(vals):
        vmax = jnp.maximum(jnp.maximum(vals[0], vals[1]), jnp.maximum(vals[2], vals[3]))
        idx = jnp.where(vals[0] >= vmax, 0,
                        jnp.where(vals[1] >= vmax, 1, jnp.where(vals[2] >= vmax, 2, 3)))
        return vmax, idx

    v1, i1 = first_argmax(es)
    rest = [jnp.where(i1 == e, -jnp.inf, es[e]) for e in range(EXPERTS_PER_GROUP)]
    v2, i2 = first_argmax(rest)
    e21 = jnp.exp(v2 - v1)
    w1 = (1.0 / (1.0 + e21)) * p_grp
    w2 = (e21 / (1.0 + e21)) * p_grp
    first_lower = i1 < i2
    lo = jnp.where(first_lower, i1, i2)
    hi = jnp.where(first_lower, i2, i1)
    glo_ref[0] = jnp.where(first_lower, w1, w2)
    ghi_ref[0] = jnp.where(first_lower, w2, w1)
    pair = jnp.where(lo == 0, hi - 1, jnp.where(lo == 1, hi + 1, 5))
    cls = g_sel * N_PAIRS + pair
    cls_ref[0] = cls

    crow = lax.broadcasted_iota(I32, (ROUTER_ROWS, tm), 0)
    onehot = jnp.where(crow == cls, 1.0, 0.0)
    ki = lax.broadcasted_iota(I32, (tm, tm), 0)
    kj = lax.broadcasted_iota(I32, (tm, tm), 1)
    upper = jnp.where(ki <= kj, 1.0, 0.0).astype(BF16)
    cum = jnp.dot(onehot.astype(BF16), upper, preferred_element_type=F32)
    carry = carry_scr[...]
    rank = jnp.sum(onehot * (cum - 1.0 + carry), axis=0, keepdims=True)
    rank_ref[0] = rank.astype(I32)
    carry = carry + cum[:, tm - 1:tm]
    carry_scr[...] = carry
    cnt_ref[...] = carry


def _router(x, sh, sc, g, wr_t, br, *, tm):
    bsz, t_len, d = x.shape
    nt = t_len // tm
    n_tiles = bsz * nt
    tok = pl.BlockSpec((1, tm, d), lambda b, t: (b, t, 0))
    lane_row = pl.BlockSpec((1, 1, tm), lambda b, t: (b * nt + t, 0, 0))
    row_shape = jax.ShapeDtypeStruct((n_tiles, 1, tm), F32)
    row_shape_i = jax.ShapeDtypeStruct((n_tiles, 1, tm), I32)
    return pl.pallas_call(
        functools.partial(_router_kernel, tm=tm),
        grid=(bsz, nt),
        in_specs=[tok, _mod_spec(sh, tm), _mod_spec(sc, tm), _full_spec(g), _full_spec(wr_t),
                  _full_spec(br)],
        out_specs=[tok, lane_row, lane_row, lane_row, lane_row,
                   pl.BlockSpec((ROUTER_ROWS, 1), lambda b, t: (0, 0))],
        out_shape=[jax.ShapeDtypeStruct((bsz, t_len, d), BF16), row_shape_i, row_shape_i,
                   row_shape, row_shape, jax.ShapeDtypeStruct((ROUTER_ROWS, 1), F32)],
        scratch_shapes=[pltpu.VMEM((ROUTER_ROWS, 1), F32)],
        compiler_params=_params("arbitrary", "arbitrary"),
        name="moe_router",
    )(x, sh, sc, g, wr_t, br)


def _expert_kernel(e1_ref, e2_ref, valid_ref, xblk_ref, oblk_ref,
                   xs_ref, glo_ref, ghi_ref, wu1_ref, wd1_ref, wu2_ref, wd2_ref, o_ref):
    del e1_ref, e2_ref, xblk_ref, oblk_ref
    j = pl.program_id(0)

    @pl.when(valid_ref[j] == 1)
    def _():
        x = xs_ref[...]

        def ffn(wu_ref, wd_ref, gate):
            gu = jnp.dot(x, wu_ref[0], preferred_element_type=F32)
            g_ = gu[:, :D_EXPERT]
            u_ = gu[:, D_EXPERT:]
            act = ((g_ * _sigmoid(g_)) * u_).astype(BF16)
            return gate * jnp.dot(act, wd_ref[0], preferred_element_type=F32)

        o_ref[...] = ffn(wu1_ref, wd1_ref, glo_ref[...]) + ffn(wu2_ref, wd2_ref, ghi_ref[...])

    @pl.when(valid_ref[j] == 0)
    def _():
        o_ref[...] = jnp.zeros_like(o_ref)


def _experts(xs, glo_s, ghi_s, w_up, w_down, e1, e2, valid, xblk, oblk, n_tiles):
    d = xs.shape[1]
    tmx = EXPERT_TILE
    grid_spec = pltpu.PrefetchScalarGridSpec(
        num_scalar_prefetch=5,
        grid=(n_tiles,),
        in_specs=[
            pl.BlockSpec((tmx, d), lambda j, e1, e2, v, xb, ob: (xb[j], 0)),
            pl.BlockSpec((tmx, 1), lambda j, e1, e2, v, xb, ob: (xb[j], 0)),
            pl.BlockSpec((tmx, 1), lambda j, e1, e2, v, xb, ob: (xb[j], 0)),
            pl.BlockSpec((1, d, 2 * D_EXPERT), lambda j, e1, e2, v, xb, ob: (e1[j], 0, 0)),
            pl.BlockSpec((1, D_EXPERT, d), lambda j, e1, e2, v, xb, ob: (e1[j], 0, 0)),
            pl.BlockSpec((1, d, 2 * D_EXPERT), lambda j, e1, e2, v, xb, ob: (e2[j], 0, 0)),
            pl.BlockSpec((1, D_EXPERT, d), lambda j, e1, e2, v, xb, ob: (e2[j], 0, 0)),
        ],
        out_specs=pl.BlockSpec((tmx, d), lambda j, e1, e2, v, xb, ob: (ob[j], 0)),
    )
    return pl.pallas_call(
        _expert_kernel,
        grid_spec=grid_spec,
        out_shape=jax.ShapeDtypeStruct(((n_tiles + 1) * tmx, d), F32),
        compiler_params=_params("arbitrary"),
        name="moe_experts",
    )(e1, e2, valid, xblk, oblk, xs, glo_s, ghi_s, w_up, w_down, w_up, w_down)


def _moe(x, sh, sc, p, *, tm):
    bsz, t_len, d = x.shape
    n = bsz * t_len
    tmx = EXPERT_TILE
    hn, cls, rank, glo, ghi, cnt = _router(x, sh, sc, p["g_moe"], p["wr_t"], p["br"], tm=tm)
    cls = cls.reshape(n)
    rank = rank.reshape(n)
    counts = cnt[:N_CLASSES, 0].astype(I32)
    padded = ((counts + tmx - 1) // tmx) * tmx
    ends = jnp.cumsum(padded)
    offs = ends - padded
    pos = offs[cls] + rank
    n_tiles = n // tmx + N_CLASSES
    n_rows = n_tiles * tmx
    total = ends[N_CLASSES - 1]
    tile_start = jnp.arange(n_tiles, dtype=I32) * tmx
    valid = (tile_start < total).astype(I32)
    last_tile = jnp.maximum(total // tmx - 1, 0)
    xblk = jnp.minimum(jnp.arange(n_tiles, dtype=I32), last_tile)
    oblk = jnp.where(valid == 1, jnp.arange(n_tiles, dtype=I32), n_tiles)
    tile_cls = jnp.minimum(jnp.searchsorted(ends, xblk * tmx, side="right"), N_CLASSES - 1).astype(I32)
    grp = tile_cls // N_PAIRS
    pair = tile_cls % N_PAIRS
    e1 = grp * EXPERTS_PER_GROUP + jnp.asarray(PAIR_LO, I32)[pair]
    e2 = grp * EXPERTS_PER_GROUP + jnp.asarray(PAIR_HI, I32)[pair]
    src = jnp.zeros((n_rows,), I32).at[pos].set(jnp.arange(n, dtype=I32))
    xs = jnp.take(hn.reshape(n, d), src, axis=0)
    glo_s = jnp.zeros((n_rows, 1), F32).at[pos, 0].set(glo.reshape(n))
    ghi_s = jnp.zeros((n_rows, 1), F32).at[pos, 0].set(ghi.reshape(n))
    ys = _experts(xs, glo_s, ghi_s, p["w_up"], p["w_down"], e1, e2, valid, xblk, oblk, n_tiles)
    return jnp.take(ys, pos, axis=0).reshape(bsz, t_len, d)


def _post0_kernel(x_ref, y_ref, g2_ref, shk_ref, sck_ref, kvg_ref, wk_ref, wv_ref, wft_ref, bf_ref,
                  shq_ref, scq_ref, gq_ref, wq_ref,
                  x1_ref, k_ref, v_ref, kb_ref, vb_ref, q_ref, lft_ref, fct_ref, *scratch,
                  tm, seg, carry):
    x1 = x_ref[0] + g2_ref[0] * y_ref[0]
    x1_ref[0] = x1
    hk = _norm_mod(x1, kvg_ref[...], shk_ref[0], sck_ref[0])
    hkb = hk.astype(BF16)
    k = jnp.dot(hkb, wk_ref[...], preferred_element_type=F32)
    v = jnp.dot(hkb, wv_ref[...], preferred_element_type=F32)
    k_ref[0] = k
    v_ref[0] = v
    kb_ref[0] = k.astype(BF16)
    vb_ref[0] = v.astype(BF16)
    f = lax.dot_general(wft_ref[...], hk, NT_DIMS, precision=HIGHEST,
                        preferred_element_type=F32) + bf_ref[...]
    lf = _log_sigmoid(f)
    lft_ref[0] = lf
    col = lax.broadcasted_iota(I32, (1, tm), 1)
    cseg = col & (seg - 1)
    c = lf
    s = 1
    while s < seg:
        c = c + jnp.where(cseg >= s, pltpu.roll(c, s, axis=1), 0.0)
        s *= 2
    if carry:
        (f_scr,) = scratch

        @pl.when(pl.program_id(1) == 0)
        def _():
            f_scr[...] = jnp.zeros_like(f_scr)

        c = c + f_scr[...]
        f_scr[...] = c[:, tm - 1:tm]
    fct_ref[0] = c
    hq = _norm_mod(x1, gq_ref[...], shq_ref[0], scq_ref[0])
    q_ref[0] = jnp.dot(hq.astype(BF16), wq_ref[...], preferred_element_type=F32).astype(BF16)


def _post0(x, y, g2, shk, sck, shq, scq, p, *, tm, seg, carry):
    bsz, t_len, d = x.shape
    tok = pl.BlockSpec((1, tm, d), lambda b, t: (b, t, 0))
    headrow = pl.BlockSpec((1, N_HEADS, tm), lambda b, t: (b, 0, t))
    weights_kv = [p["kv_g"], p["w_k"], p["w_v"], p["w_f_t"], p["b_f"]]
    weights_q = [p["g_mix1"], p["w_q"]]
    return pl.pallas_call(
        functools.partial(_post0_kernel, tm=tm, seg=seg, carry=carry),
        grid=(bsz, t_len // tm),
        in_specs=[tok, tok, _mod_spec(g2, tm), _mod_spec(shk, tm), _mod_spec(sck, tm)]
                 + [_full_spec(w) for w in weights_kv]
                 + [_mod_spec(shq, tm), _mod_spec(scq, tm)] + [_full_spec(w) for w in weights_q],
        out_specs=[tok, tok, tok, tok, tok, tok, headrow, headrow],
        out_shape=[jax.ShapeDtypeStruct((bsz, t_len, d), F32)] * 3
                  + [jax.ShapeDtypeStruct((bsz, t_len, d), BF16)] * 3
                  + [jax.ShapeDtypeStruct((bsz, N_HEADS, t_len), F32)] * 2,
        scratch_shapes=[pltpu.VMEM((N_HEADS, 1), F32)] if carry else [],
        compiler_params=_params("arbitrary", "arbitrary"),
        name="shared_kv_and_q",
    )(x, y, g2, shk, sck, *weights_kv, shq, scq, *weights_q)


def _fox_prompt_kernel(qi_ref, kj_ref, q_ref, k_ref, v_ref, fq_ref, fk_ref, o_ref,
                       m_scr, l_scr, fq_scr, acc_scr, *, tq):
    p = pl.program_id(1)
    i = qi_ref[p]
    j = kj_ref[p]
    scale = HEAD_DIM ** -0.5
    ri = lax.broadcasted_iota(I32, (tq, tq), 0)
    ci = lax.broadcasted_iota(I32, (tq, tq), 1)

    @pl.when(j == 0)
    def _():
        m_scr[...] = jnp.full_like(m_scr, -jnp.inf)
        l_scr[...] = jnp.zeros_like(l_scr)
        acc_scr[...] = jnp.zeros_like(acc_scr)
        fq = fq_ref[0]
        for h in range(N_HEADS):
            fq_scr[h] = jnp.sum(jnp.where(ri == ci, fq[h:h + 1], 0.0), axis=1, keepdims=True)

    def step(masked):
        fk = fk_ref[0]
        for h in range(N_HEADS):
            hs = slice(h * HEAD_DIM, (h + 1) * HEAD_DIM)
            s = lax.dot_general(q_ref[0, :, hs], k_ref[0, :, hs], NT_DIMS,
                                preferred_element_type=F32) * scale
            s = s + fq_scr[h] - fk[h:h + 1]
            if masked:
                s = jnp.where(ci <= ri, s, -jnp.inf)
            m_old = m_scr[h]
            m_new = jnp.maximum(m_old, jnp.max(s, axis=1, keepdims=True))
            alpha = jnp.exp(m_old - m_new)
            pr = jnp.exp(s - m_new)
            l_scr[h] = alpha * l_scr[h] + jnp.sum(pr, axis=1, keepdims=True)
            acc_scr[:, hs] = alpha * acc_scr[:, hs] + jnp.dot(
                pr.astype(BF16), v_ref[0, :, hs], preferred_element_type=F32)
            m_scr[h] = m_new

    @pl.when(j < i)
    def _():
        step(False)

    @pl.when(j == i)
    def _():
        step(True)
        for h in range(N_HEADS):
            hs = slice(h * HEAD_DIM, (h + 1) * HEAD_DIM)
            o_ref[0, :, hs] = (acc_scr[:, hs] / l_scr[h]).astype(BF16)


def _fox_prompt(q, k, v, fct):
    bsz, t_len, d = q.shape
    tq = ATTN_TILE
    nb = t_len // tq
    pairs = [(i, j) for i in range(nb) for j in range(i + 1)]
    qi = jnp.asarray([a for a, _ in pairs], I32)
    kj = jnp.asarray([b for _, b in pairs], I32)
    grid_spec = pltpu.PrefetchScalarGridSpec(
        num_scalar_prefetch=2,
        grid=(bsz, len(pairs)),
        in_specs=[
            pl.BlockSpec((1, tq, d), lambda b, p, qi, kj: (b, qi[p], 0)),
            pl.BlockSpec((1, tq, d), lambda b, p, qi, kj: (b, kj[p], 0)),
            pl.BlockSpec((1, tq, d), lambda b, p, qi, kj: (b, kj[p], 0)),
            pl.BlockSpec((1, N_HEADS, tq), lambda b, p, qi, kj: (b, 0, qi[p])),
            pl.BlockSpec((1, N_HEADS, tq), lambda b, p, qi, kj: (b, 0, kj[p])),
        ],
        out_specs=pl.BlockSpec((1, tq, d), lambda b, p, qi, kj: (b, qi[p], 0)),
        scratch_shapes=[pltpu.VMEM((N_HEADS, tq, 1), F32), pltpu.VMEM((N_HEADS, tq, 1), F32),
                        pltpu.VMEM((N_HEADS, tq, 1), F32), pltpu.VMEM((tq, d), F32)],
    )
    return pl.pallas_call(
        functools.partial(_fox_prompt_kernel, tq=tq),
        grid_spec=grid_spec,
        out_shape=jax.ShapeDtypeStruct((bsz, t_len, d), BF16),
        compiler_params=_params("arbitrary", "arbitrary"),
        name="fox_prompt_attention",
    )(qi, kj, q, k, v, fct, fct)


def _fox_decode_kernel(pt_ref, q_ref, kn_ref, vn_ref, cq_ref, ck_ref, *refs, pages, page_size):
    del pt_ref
    k_refs = refs[:pages]
    v_refs = refs[pages:2 * pages]
    lf_refs = refs[2 * pages:3 * pages]
    o_ref = refs[3 * pages]
    qbd_scr, m_scr, l_scr, acc_scr, r_scr = refs[3 * pages + 1:]
    step = pl.program_id(1)
    n_tok = q_ref.shape[1]
    rows = n_tok * N_HEADS
    d = q_ref.shape[2]
    scale = HEAD_DIM ** -0.5
    lane_head = lax.broadcasted_iota(I32, (rows, d), 1) // HEAD_DIM
    row_head = lax.broadcasted_iota(I32, (rows, d), 0) & (N_HEADS - 1)
    own_head = lane_head == row_head

    def attend(s, v_bf):
        m_old = m_scr[...]
        m_new = jnp.maximum(m_old, jnp.max(s, axis=1, keepdims=True))
        alpha = jnp.exp(m_old - m_new)
        pr = jnp.exp(s - m_new)
        l_scr[...] = alpha * l_scr[...] + jnp.sum(pr, axis=1, keepdims=True)
        acc_scr[...] = alpha * acc_scr[...] + jnp.dot(pr.astype(BF16), v_bf,
                                                      preferred_element_type=F32)
        m_scr[...] = m_new

    @pl.when(step == 0)
    def _():
        q = q_ref[0]
        q_rows = jnp.concatenate(
            [jnp.broadcast_to(q[t:t + 1], (N_HEADS, d)) for t in range(n_tok)], axis=0)
        qbd_scr[...] = jnp.where(own_head, q_rows, jnp.zeros_like(q_rows))
        m_scr[...] = jnp.full_like(m_scr, -jnp.inf)
        l_scr[...] = jnp.zeros_like(l_scr)
        acc_scr[...] = jnp.zeros_like(acc_scr)
        r_scr[...] = jnp.zeros_like(r_scr)
        pad = jnp.zeros((page_size - n_tok, d), BF16)
        kn = jnp.concatenate([kn_ref[0], pad], axis=0)
        vn = jnp.concatenate([vn_ref[0], pad], axis=0)
        s = lax.dot_general(qbd_scr[...], kn, NT_DIMS, preferred_element_type=F32) * scale
        s = s + cq_ref[0] - ck_ref[0]
        key = lax.broadcasted_iota(I32, (rows, page_size), 1)
        qtok = lax.broadcasted_iota(I32, (rows, page_size), 0) // N_HEADS
        s = jnp.where(key <= qtok, s, -jnp.inf)
        attend(s, vn)

    lane = lax.broadcasted_iota(I32, (N_HEADS, page_size), 1)
    for u in range(pages):
        lf = lf_refs[u][0]
        suf = lf
        sh = 1
        while sh < page_size:
            suf = suf + jnp.where(lane + sh < page_size,
                                  pltpu.roll(suf, page_size - sh, axis=1), 0.0)
            sh *= 2
        after = r_scr[...] + (suf - lf)
        r_scr[...] = r_scr[...] + suf[:, 0:1]
        bias = jnp.concatenate([after] * n_tok, axis=0)
        kp = k_refs[u][0].astype(BF16)
        vp = v_refs[u][0].astype(BF16)
        s = lax.dot_general(qbd_scr[...], kp, NT_DIMS, preferred_element_type=F32) * scale
        s = s + (cq_ref[0] + bias)
        attend(s, vp)

    @pl.when(step == pl.num_programs(1) - 1)
    def _():
        o = jnp.where(own_head, acc_scr[...] / l_scr[...], 0.0)
        o_ref[0] = jnp.concatenate(
            [jnp.sum(o[t * N_HEADS:(t + 1) * N_HEADS], axis=0, keepdims=True) for t in range(n_tok)],
            axis=0).astype(BF16)


def _fox_decode(q, k_new, v_new, cq, ck, cache_k, cache_v, cache_lft, page_table):
    bsz, n_tok, d = q.shape
    n_pages = page_table.shape[1]
    page_size = cache_k.shape[1]
    pages = PAGES_PER_STEP
    rows = n_tok * N_HEADS

    def page_map(u):
        def index_map(b, s, pt):
            return (pt[b, n_pages - 1 - (s * pages + u)], 0, 0)
        return index_map

    seq = lambda b, s, pt: (b, 0, 0)
    in_specs = [pl.BlockSpec((1, n_tok, d), seq), pl.BlockSpec((1, n_tok, d), seq),
                pl.BlockSpec((1, n_tok, d), seq), pl.BlockSpec((1, rows, 1), seq),
                pl.BlockSpec((1, rows, page_size), seq)]
    in_specs += [pl.BlockSpec((1, page_size, d), page_map(u)) for u in range(pages)]
    in_specs += [pl.BlockSpec((1, page_size, d), page_map(u)) for u in range(pages)]
    in_specs += [pl.BlockSpec((1, N_HEADS, page_size), page_map(u)) for u in range(pages)]
    grid_spec = pltpu.PrefetchScalarGridSpec(
        num_scalar_prefetch=1,
        grid=(bsz, n_pages // pages),
        in_specs=in_specs,
        out_specs=pl.BlockSpec((1, n_tok, d), seq),
        scratch_shapes=[pltpu.VMEM((rows, d), BF16), pltpu.VMEM((rows, 1), F32),
                        pltpu.VMEM((rows, 1), F32), pltpu.VMEM((rows, d), F32),
                        pltpu.VMEM((N_HEADS, 1), F32)],
    )
    return pl.pallas_call(
        functools.partial(_fox_decode_kernel, pages=pages, page_size=page_size),
        grid_spec=grid_spec,
        out_shape=jax.ShapeDtypeStruct((bsz, n_tok, d), BF16),
        compiler_params=_params("arbitrary", "arbitrary"),
        name="fox_decode_attention",
    )(page_table, q, k_new, v_new, cq, ck, *([cache_k] * pages), *([cache_v] * pages),
      *([cache_lft] * pages))


def _oproj_kernel(x_ref, o_ref, g1_ref, wo_ref, xo_ref):
    xo_ref[0] = x_ref[0] + g1_ref[0] * jnp.dot(o_ref[0], wo_ref[...], preferred_element_type=F32)


def _oproj(x, o, g1, w_o, *, tm):
    bsz, t_len, d = x.shape
    tok = pl.BlockSpec((1, tm, d), lambda b, t: (b, t, 0))
    return pl.pallas_call(
        _oproj_kernel,
        grid=(bsz, t_len // tm),
        in_specs=[tok, tok, _mod_spec(g1, tm), _full_spec(w_o)],
        out_specs=tok,
        out_shape=jax.ShapeDtypeStruct((bsz, t_len, d), F32),
        compiler_params=_params("arbitrary", "arbitrary"),
        name="attn_out_proj",
    )(x, o, g1, w_o)


def _final_kernel(x_ref, y_ref, g2_ref, g_ref, o_ref):
    x = x_ref[0] + g2_ref[0] * y_ref[0]
    o_ref[0] = (x * lax.rsqrt(jnp.mean(x * x, axis=-1, keepdims=True) + EPS)) * g_ref[...]


def _final(x, y, g2, g_final, *, tm):
    bsz, t_len, d = x.shape
    tok = pl.BlockSpec((1, tm, d), lambda b, t: (b, t, 0))
    return pl.pallas_call(
        _final_kernel,
        grid=(bsz, t_len // tm),
        in_specs=[tok, tok, _mod_spec(g2, tm), _full_spec(g_final)],
        out_specs=tok,
        out_shape=jax.ShapeDtypeStruct((bsz, t_len, d), F32),
        compiler_params=_params("arbitrary", "arbitrary"),
        name="final_norm",
    )(x, y, g2, g_final)


def _trunk(x, mods0, mods1, kvmods, prev, h0, w, attend, *, tm, seg, carry):
    sh1, sc1, g1, sh2, sc2, g2 = mods0
    x, h_out, conv_out = _lru_layer(x, sh1, sc1, g1, w["lru"], prev, h0, tm=tm, seg=seg, carry=carry)
    y = _moe(x, sh2, sc2, w["moe0"], tm=tm)
    sh1b, sc1b, g1b, sh2b, sc2b, g2b = mods1
    shk, sck = kvmods
    x, k, v, kb, vb, q, lft, fct = _post0(x, y, g2, shk, sck, sh1b, sc1b, w["post0"],
                                          tm=tm, seg=seg, carry=carry)
    o = attend(q, kb, vb, lft, fct)
    x = _oproj(x, o, g1b, w["w_o"], tm=tm)
    y = _moe(x, sh2b, sc2b, w["moe1"], tm=tm)
    y_out = _final(x, y, g2b, w["g_final"], tm=tm)
    return y_out, h_out, conv_out, k, v, lft


def kernel(x_prompt, x_sample, state_h, state_conv, cache_k, cache_v, cache_logf, page_table,
           c_prompt, c_sample, w_mod, b_mod, g_mix, g_moe, lru_w_in, lru_conv_w, lru_conv_b,
           lru_w_a, lru_b_a, lru_w_x, lru_b_x, lru_lambda, lru_w_out, kv_g, kv_w_mod, kv_b_mod,
           kv_w, kv_b_f, attn_w_q, attn_w_o, moe_w_grp, moe_b_grp, moe_w_exp, moe_b_exp,
           moe_w_up, moe_w_down, g_final):
    d = D_MODEL
    bp, tp, _ = x_prompt.shape
    bs, ts, _ = x_sample.shape
    hd = N_HEADS * HEAD_DIM
    row = lambda a: a.reshape(1, -1)

    def moe_weights(layer):
        wr_t = jnp.concatenate(
            [moe_w_grp[layer].T, moe_w_exp[layer].T,
             jnp.zeros((ROUTER_ROWS - N_GROUPS - N_EXPERTS, d), F32)], axis=0)
        br = jnp.concatenate(
            [moe_b_grp[layer], moe_b_exp[layer],
             jnp.zeros((ROUTER_ROWS - N_GROUPS - N_EXPERTS,), F32)]).reshape(ROUTER_ROWS, 1)
        return dict(g_moe=row(g_moe[layer]), wr_t=wr_t, br=br,
                    w_up=moe_w_up[layer].astype(BF16), w_down=moe_w_down[layer].astype(BF16))

    w = dict(
        lru=dict(g_mix=row(g_mix[0]), w_in=lru_w_in[0].astype(BF16), conv_w=lru_conv_w[0],
                 conv_b=row(lru_conv_b[0]), w_a=lru_w_a[0].astype(BF16), b_a=row(lru_b_a[0]),
                 w_x=lru_w_x[0].astype(BF16), b_x=row(lru_b_x[0]), lam=row(lru_lambda[0]),
                 w_out=lru_w_out[0].astype(BF16)),
        moe0=moe_weights(0),
        moe1=moe_weights(1),
        post0=dict(kv_g=row(kv_g), w_k=kv_w[:, :hd].astype(BF16),
                   w_v=kv_w[:, hd:2 * hd].astype(BF16), w_f_t=kv_w[:, 2 * hd:].T,
                   b_f=kv_b_f.reshape(N_HEADS, 1), g_mix1=row(g_mix[1]),
                   w_q=attn_w_q[0].astype(BF16)),
        w_o=attn_w_o[0].astype(BF16),
        g_final=row(g_final),
    )

    n_c = bp + bs
    c_rows = -(-n_c // SUBLANES) * SUBLANES
    c_all = jnp.concatenate([c_prompt, c_sample, jnp.zeros((c_rows - n_c, d), F32)], axis=0)
    mod_l0 = _mm_bias(c_all, w_mod[0], b_mod[0])
    mod_l1 = _mm_bias(c_all, w_mod[1], b_mod[1])
    mod_kv = _mm_bias(c_all, kv_w_mod, kv_b_mod)

    def prompt_mods(m, parts):
        return [a.reshape(bp, 1, d) for a in jnp.split(m[:bp], parts, axis=-1)]

    def sample_mods(m, parts):
        return [jnp.repeat(a, ts, axis=0).reshape(1, bs * ts, d)
                for a in jnp.split(m[bp:n_c], parts, axis=-1)]

    prev_p = jnp.zeros((bp, SUBLANES, D_RNN), F32)
    h0_p = jnp.zeros((bp, 1, D_RNN), F32)
    attend_p = lambda q, kb, vb, lft, fct: _fox_prompt(q, kb, vb, fct)
    y_p, h_p, conv_p, k_p, v_p, lft_p = _trunk(
        x_prompt, prompt_mods(mod_l0, 6), prompt_mods(mod_l1, 6), prompt_mods(mod_kv, 2),
        prev_p, h0_p, w, attend_p, tm=TOKEN_TILE, seg=TOKEN_TILE, carry=True)

    n_s = bs * ts
    prev_s = jnp.pad(state_conv[0], ((0, 0), (ts - (CONV_WIDTH - 1), 0), (0, 0))).reshape(1, n_s, D_RNN)
    h0_s = jnp.repeat(state_h[0], ts, axis=0).reshape(1, n_s, D_RNN)
    n_pool, page_size = cache_k.shape[0], cache_k.shape[1]
    cache_k2 = cache_k.reshape(n_pool, page_size, hd)
    cache_v2 = cache_v.reshape(n_pool, page_size, hd)
    cache_lft = jnp.transpose(cache_logf, (0, 2, 1))

    def attend_s(q, kb, vb, lft, fct):
        c_new = fct.reshape(N_HEADS, bs, ts)
        cq = jnp.transpose(c_new, (1, 2, 0)).reshape(bs, ts * N_HEADS, 1)
        ck = jnp.broadcast_to(jnp.transpose(c_new, (1, 0, 2))[:, None], (bs, ts, N_HEADS, ts))
        ck = jnp.pad(ck.reshape(bs, ts * N_HEADS, ts), ((0, 0), (0, 0), (0, page_size - ts)))
        o = _fox_decode(q.reshape(bs, ts, d), kb.reshape(bs, ts, d), vb.reshape(bs, ts, d), cq, ck,
                        cache_k2, cache_v2, cache_lft, page_table)
        return o.reshape(1, n_s, d)

    y_s, h_s, conv_s, k_s, v_s, lft_s = _trunk(
        x_sample.reshape(1, n_s, d), sample_mods(mod_l0, 6), sample_mods(mod_l1, 6),
        sample_mods(mod_kv, 2), prev_s, h0_s, w, attend_s, tm=n_s, seg=ts, carry=False)

    n_conv = CONV_WIDTH - 1
    return (
        y_p,
        y_s.reshape(bs, ts, d),
        h_p.reshape(1, bp, D_RNN),
        conv_p[:, SUBLANES - n_conv:].reshape(1, bp, n_conv, D_RNN),
        k_p.reshape(bp, tp, N_HEADS, HEAD_DIM),
        v_p.reshape(bp, tp, N_HEADS, HEAD_DIM),
        jnp.transpose(lft_p, (0, 2, 1)),
        h_s.reshape(bs, ts, D_RNN)[:, ts - 1].reshape(1, bs, D_RNN),
        conv_s.reshape(bs, ts, D_RNN)[:, ts - n_conv:].reshape(1, bs, n_conv, D_RNN),
        k_s.reshape(bs, ts, N_HEADS, HEAD_DIM),
        v_s.reshape(bs, ts, N_HEADS, HEAD_DIM),
        jnp.transpose(lft_s.reshape(N_HEADS, bs, ts), (1, 2, 0)),
    )
```

```python
import functools
import math

import jax
import jax.numpy as jnp
from jax import lax
from jax.experimental import pallas as pl
from jax.experimental.pallas import tpu as pltpu

F32 = jnp.float32
BF16 = jnp.bfloat16
I32 = jnp.int32

D_MODEL = 1024
D_RNN = D_MODEL
N_LRU_BLOCKS = 8
LRU_BLOCK = D_RNN // N_LRU_BLOCKS
CONV_WIDTH = 4
LRU_C = 8.0
N_HEADS = 8
HEAD_DIM = D_MODEL // N_HEADS
N_GROUPS = 4
EXPERTS_PER_GROUP = 4
N_EXPERTS = N_GROUPS * EXPERTS_PER_GROUP
D_EXPERT = D_MODEL // 2
EPS = 1e-6
N_PAIRS = 6
N_CLASSES = N_GROUPS * N_PAIRS
PAIR_LO = (0, 0, 0, 1, 1, 2)
PAIR_HI = (1, 2, 3, 2, 3, 3)

SUBLANES = 8
LANES = 128
VMEM_LIMIT = 48 * 1024 * 1024

TOKEN_TILE = 256
EXPERT_TILE = 256
ROUTER_ROWS = 32
ATTN_TILE = 512
PAGES_PER_STEP = 8
MOD_TILE = 512

HIGHEST = lax.Precision.HIGHEST
NT_DIMS = (((1,), (1,)), ((), ()))


def _params(*semantics):
    return pltpu.CompilerParams(dimension_semantics=semantics, vmem_limit_bytes=VMEM_LIMIT)


def _norm_mod(x, g, shift, scale):
    y = x * lax.rsqrt(jnp.mean(x * x, axis=-1, keepdims=True) + EPS)
    return (y * g) * (1.0 + scale) + shift


def _log_sigmoid(x):
    return jnp.minimum(x, 0.0) - jnp.log1p(jnp.exp(-jnp.abs(x)))


def _sigmoid(x):
    return 1.0 / (1.0 + jnp.exp(-x))


def _gelu_tanh(x):
    c = math.sqrt(2.0 / math.pi)
    return x * (0.5 * (1.0 + jnp.tanh(c * (x + 0.044715 * (x * x * x)))))


def _mod_spec(arr, tm):
    if arr.shape[1] == 1:
        return pl.BlockSpec((1, 1, arr.shape[2]), lambda b, t: (b, 0, 0))
    return pl.BlockSpec((1, tm, arr.shape[2]), lambda b, t: (b, t, 0))


def _full_spec(arr):
    zeros = (0,) * arr.ndim
    return pl.BlockSpec(arr.shape, lambda *_: zeros)


def _mm_bias_kernel(x_ref, w_ref, b_ref, o_ref):
    o_ref[...] = jnp.dot(x_ref[...].astype(BF16), w_ref[...].astype(BF16),
                         preferred_element_type=F32) + b_ref[...]


def _mm_bias(x, w, b):
    m, d = x.shape
    n = w.shape[1]
    return pl.pallas_call(
        _mm_bias_kernel,
        grid=(n // MOD_TILE,),
        in_specs=[pl.BlockSpec((m, d), lambda j: (0, 0)),
                  pl.BlockSpec((d, MOD_TILE), lambda j: (0, j)),
                  pl.BlockSpec((1, MOD_TILE), lambda j: (0, j))],
        out_specs=pl.BlockSpec((m, MOD_TILE), lambda j: (0, j)),
        out_shape=jax.ShapeDtypeStruct((m, n), F32),
        compiler_params=_params("arbitrary"),
        name="mod_vectors",
    )(x, w, b.reshape(1, n))


def _lru_kernel(x_ref, sh_ref, sc_ref, gt_ref, gmix_ref, win_ref, cw_ref, cb_ref, wa_ref, ba_ref,
                wx_ref, bx_ref, lam_ref, wout_ref, prev_ref, h0_ref,
                xo_ref, hs_ref, xb_ref, *scratch, tm, seg, carry):
    x = x_ref[0]
    hn = _norm_mod(x, gmix_ref[...], sh_ref[0], sc_ref[0])
    xg = jnp.dot(hn.astype(BF16), win_ref[...], preferred_element_type=F32)
    xb = xg[:, :D_RNN]
    gb = xg[:, D_RNN:]

    if carry:
        prev_scr, h_scr = scratch

        @pl.when(pl.program_id(1) == 0)
        def _():
            prev_scr[...] = prev_ref[0]
            h_scr[...] = h0_ref[0]

        prev = prev_scr[...]
        hprev = h_scr[...]
    else:
        prev = prev_ref[0]
        hprev = h0_ref[0]

    row = lax.broadcasted_iota(I32, (tm, 1), 0)
    rseg = row & (seg - 1)
    nprev = prev.shape[0]

    cw = cw_ref[...]
    xc = cb_ref[...] + cw[CONV_WIDTH - 1:CONV_WIDTH] * xb
    for d in range(1, CONV_WIDTH):
        rolled = pltpu.roll(xb, d, axis=0)
        rp = pltpu.roll(prev, nprev - SUBLANES + d, axis=0)
        if carry:
            head = jnp.where(row[:SUBLANES] < d, rp, rolled[:SUBLANES])
            shifted = head if tm == SUBLANES else jnp.concatenate([head, rolled[SUBLANES:]], axis=0)
        else:
            shifted = jnp.where(rseg < d, rp, rolled)
        xc = xc + cw[CONV_WIDTH - 1 - d:CONV_WIDTH - d] * shifted

    xcb = xc.astype(BF16)
    ra = jnp.concatenate(
        [jnp.dot(xcb[:, n * LRU_BLOCK:(n + 1) * LRU_BLOCK], wa_ref[n], preferred_element_type=F32)
         for n in range(N_LRU_BLOCKS)], axis=1)
    rx = jnp.concatenate(
        [jnp.dot(xcb[:, n * LRU_BLOCK:(n + 1) * LRU_BLOCK], wx_ref[n], preferred_element_type=F32)
         for n in range(N_LRU_BLOCKS)], axis=1)
    r = _sigmoid(ra + ba_ref[...])
    ig = _sigmoid(rx + bx_ref[...])
    log_a = (LRU_C * r) * _log_sigmoid(lam_ref[...])
    a = jnp.exp(log_a)
    bt = jnp.sqrt(-jnp.tanh(log_a) * (a * a + 1.0)) * (ig * xc)

    s = 1
    while s < seg:
        inside = rseg >= s
        a_sh = jnp.where(inside, pltpu.roll(a, s, axis=0), 1.0)
        b_sh = jnp.where(inside, pltpu.roll(bt, s, axis=0), 0.0)
        bt = a * b_sh + bt
        a = a * a_sh
        s *= 2
    hs = bt + a * hprev

    if carry:
        h_scr[...] = hs[tm - 1:tm]
        prev_scr[...] = xb[tm - SUBLANES:]
        hs_ref[0] = hs[tm - 1:tm]
        xb_ref[0] = xb[tm - SUBLANES:]
    else:
        hs_ref[0] = hs
        xb_ref[0] = xb

    y = (hs * _gelu_tanh(gb)).astype(BF16)
    out = jnp.dot(y, wout_ref[...], preferred_element_type=F32)
    xo_ref[0] = x + gt_ref[0] * out


def _lru_layer(x, sh, sc, gt, p, prev, h0, *, tm, seg, carry):
    bsz, t_len, d = x.shape
    r = D_RNN
    nt = t_len // tm
    tok = pl.BlockSpec((1, tm, d), lambda b, t: (b, t, 0))
    if carry:
        state_specs = [pl.BlockSpec((1, SUBLANES, r), lambda b, t: (b, 0, 0)),
                       pl.BlockSpec((1, 1, r), lambda b, t: (b, 0, 0))]
        out_specs = [tok,
                     pl.BlockSpec((1, 1, r), lambda b, t: (b, 0, 0)),
                     pl.BlockSpec((1, SUBLANES, r), lambda b, t: (b, 0, 0))]
        out_shape = [jax.ShapeDtypeStruct((bsz, t_len, d), F32),
                     jax.ShapeDtypeStruct((bsz, 1, r), F32),
                     jax.ShapeDtypeStruct((bsz, SUBLANES, r), F32)]
        scratch = [pltpu.VMEM((SUBLANES, r), F32), pltpu.VMEM((1, r), F32)]
    else:
        rtok = pl.BlockSpec((1, tm, r), lambda b, t: (b, t, 0))
        state_specs = [rtok, rtok]
        out_specs = [tok, rtok, rtok]
        out_shape = [jax.ShapeDtypeStruct((bsz, t_len, d), F32),
                     jax.ShapeDtypeStruct((bsz, t_len, r), F32),
                     jax.ShapeDtypeStruct((bsz, t_len, r), F32)]
        scratch = []
    weights = [p["g_mix"], p["w_in"], p["conv_w"], p["conv_b"], p["w_a"], p["b_a"], p["w_x"],
               p["b_x"], p["lam"], p["w_out"]]
    return pl.pallas_call(
        functools.partial(_lru_kernel, tm=tm, seg=seg, carry=carry),
        grid=(bsz, nt),
        in_specs=[tok, _mod_spec(sh, tm), _mod_spec(sc, tm), _mod_spec(gt, tm)]
                 + [_full_spec(w) for w in weights] + state_specs,
        out_specs=out_specs,
        out_shape=out_shape,
        scratch_shapes=scratch,
        compiler_params=_params("arbitrary", "arbitrary"),
        name="rglru_layer",
    )(x, sh, sc, gt, *weights, prev, h0)


def _router_kernel(x_ref, sh_ref, sc_ref, g_ref, wr_ref, br_ref,
                   hn_ref, cls_ref, rank_ref, glo_ref, ghi_ref, cnt_ref, carry_scr, *, tm):
    @pl.when((pl.program_id(0) == 0) & (pl.program_id(1) == 0))
    def _():
        carry_scr[...] = jnp.zeros_like(carry_scr)

    hn = _norm_mod(x_ref[0], g_ref[...], sh_ref[0], sc_ref[0])
    hn_ref[0] = hn.astype(BF16)
    lt = lax.dot_general(wr_ref[...], hn, NT_DIMS, precision=HIGHEST,
                         preferred_element_type=F32) + br_ref[...]

    gl = [lt[k:k + 1] for k in range(N_GROUPS)]
    gmax = jnp.maximum(jnp.maximum(gl[0], gl[1]), jnp.maximum(gl[2], gl[3]))
    g_sel = jnp.where(gl[0] >= gmax, 0, jnp.where(gl[1] >= gmax, 1, jnp.where(gl[2] >= gmax, 2, 3)))
    p_grp = 1.0 / (jnp.exp(gl[0] - gmax) + jnp.exp(gl[1] - gmax)
                   + jnp.exp(gl[2] - gmax) + jnp.exp(gl[3] - gmax))

    def expert_logit(e):
        rows = [lt[N_GROUPS + g * EXPERTS_PER_GROUP + e:N_GROUPS + g * EXPERTS_PER_GROUP + e + 1]
                for g in range(N_GROUPS)]
        return jnp.where(g_sel == 0, rows[0],
                         jnp.where(g_sel == 1, rows[1], jnp.where(g_sel == 2, rows[2], rows[3])))

    es = [expert_logit(e) for e in range(EXPERTS_PER_GROUP)]

    def first_argmax(vals):
        vmax = jnp.maximum(jnp.maximum(vals[0], vals[1]), jnp.maximum(vals[2], vals[3]))
        idx = jnp.where(vals[0] >= vmax, 0,
                        jnp.where(vals[1] >= vmax, 1, jnp.where(vals[2] >= vmax, 2, 3)))
        return vmax, idx

    v1, i1 = first_argmax(es)
    rest = [jnp.where(i1 == e, -jnp.inf, es[e]) for e in range(EXPERTS_PER_GROUP)]
    v2, i2 = first_argmax(rest)
    e21 = jnp.exp(v2 - v1)
    w1 = (1.0 / (1.0 + e21)) * p_grp
    w2 = (e21 / (1.0 + e21)) * p_grp
    first_lower = i1 < i2
    lo = jnp.where(first_lower, i1, i2)
    hi = jnp.where(first_lower, i2, i1)
    glo_ref[0] = jnp.where(first_lower, w1, w2)
    ghi_ref[0] = jnp.where(first_lower, w2, w1)
    pair = jnp.where(lo == 0, hi - 1, jnp.where(lo == 1, hi + 1, 5))
    cls = g_sel * N_PAIRS + pair
    cls_ref[0] = cls

    crow = lax.broadcasted_iota(I32, (ROUTER_ROWS, tm), 0)
    onehot = jnp.where(crow == cls, 1.0, 0.0)
    ki = lax.broadcasted_iota(I32, (tm, tm), 0)
    kj = lax.broadcasted_iota(I32, (tm, tm), 1)
    upper = jnp.where(ki <= kj, 1.0, 0.0).astype(BF16)
    cum = jnp.dot(onehot.astype(BF16), upper, preferred_element_type=F32)
    carry = carry_scr[...]
    rank = jnp.sum(onehot * (cum - 1.0 + carry), axis=0, keepdims=True)
    rank_ref[0] = rank.astype(I32)
    carry = carry + cum[:, tm - 1:tm]
    carry_scr[...] = carry
    cnt_ref[...] = carry


def _router(x, sh, sc, g, wr_t, br, *, tm):
    bsz, t_len, d = x.shape
    nt = t_len // tm
    n_tiles = bsz * nt
    tok = pl.BlockSpec((1, tm, d), lambda b, t: (b, t, 0))
    lane_row = pl.BlockSpec((1, 1, tm), lambda b, t: (b * nt + t, 0, 0))
    row_shape = jax.ShapeDtypeStruct((n_tiles, 1, tm), F32)
    row_shape_i = jax.ShapeDtypeStruct((n_tiles, 1, tm), I32)
    return pl.pallas_call(
        functools.partial(_router_kernel, tm=tm),
        grid=(bsz, nt),
        in_specs=[tok, _mod_spec(sh, tm), _mod_spec(sc, tm), _full_spec(g), _full_spec(wr_t),
                  _full_spec(br)],
        out_specs=[tok, lane_row, lane_row, lane_row, lane_row,
                   pl.BlockSpec((ROUTER_ROWS, 1), lambda b, t: (0, 0))],
        out_shape=[jax.ShapeDtypeStruct((bsz, t_len, d), BF16), row_shape_i, row_shape_i,
                   row_shape, row_shape, jax.ShapeDtypeStruct((ROUTER_ROWS, 1), F32)],
        scratch_shapes=[pltpu.VMEM((ROUTER_ROWS, 1), F32)],
        compiler_params=_params("arbitrary", "arbitrary"),
        name="moe_router",
    )(x, sh, sc, g, wr_t, br)


def _expert_kernel(e1_ref, e2_ref, valid_ref, xblk_ref, oblk_ref,
                   xs_ref, glo_ref, ghi_ref, wu1_ref, wd1_ref, wu2_ref, wd2_ref, o_ref):
    del e1_ref, e2_ref, xblk_ref, oblk_ref
    j = pl.program_id(0)

    @pl.when(valid_ref[j] == 1)
    def _():
        x = xs_ref[...]

        def ffn(wu_ref, wd_ref, gate):
            gu = jnp.dot(x, wu_ref[0], preferred_element_type=F32)
            g_ = gu[:, :D_EXPERT]
            u_ = gu[:, D_EXPERT:]
            act = ((g_ * _sigmoid(g_)) * u_).astype(BF16)
            return gate * jnp.dot(act, wd_ref[0], preferred_element_type=F32)

        o_ref[...] = ffn(wu1_ref, wd1_ref, glo_ref[...]) + ffn(wu2_ref, wd2_ref, ghi_ref[...])

    @pl.when(valid_ref[j] == 0)
    def _():
        o_ref[...] = jnp.zeros_like(o_ref)


def _experts(xs, glo_s, ghi_s, w_up, w_down, e1, e2, valid, xblk, oblk, n_tiles):
    d = xs.shape[1]
    tmx = EXPERT_TILE
    grid_spec = pltpu.PrefetchScalarGridSpec(
        num_scalar_prefetch=5,
        grid=(n_tiles,),
        in_specs=[
            pl.BlockSpec((tmx, d), lambda j, e1, e2, v, xb, ob: (xb[j], 0)),
            pl.BlockSpec((tmx, 1), lambda j, e1, e2, v, xb, ob: (xb[j], 0)),
            pl.BlockSpec((tmx, 1), lambda j, e1, e2, v, xb, ob: (xb[j], 0)),
            pl.BlockSpec((1, d, 2 * D_EXPERT), lambda j, e1, e2, v, xb, ob: (e1[j], 0, 0)),
            pl.BlockSpec((1, D_EXPERT, d), lambda j, e1, e2, v, xb, ob: (e1[j], 0, 0)),
            pl.BlockSpec((1, d, 2 * D_EXPERT), lambda j, e1, e2, v, xb, ob: (e2[j], 0, 0)),
            pl.BlockSpec((1, D_EXPERT, d), lambda j, e1, e2, v, xb, ob: (e2[j], 0, 0)),
        ],
        out_specs=pl.BlockSpec((tmx, d), lambda j, e1, e2, v, xb, ob: (ob[j], 0)),
    )
    return pl.pallas_call(
        _expert_kernel,
        grid_spec=grid_spec,
        out_shape=jax.ShapeDtypeStruct(((n_tiles + 1) * tmx, d), F32),
        compiler_params=_params("arbitrary"),
        name="moe_experts",
    )(e1, e2, valid, xblk, oblk, xs, glo_s, ghi_s, w_up, w_down, w_up, w_down)


def _moe(x, sh, sc, p, *, tm):
    bsz, t_len, d = x.shape
    n = bsz * t_len
    tmx = EXPERT_TILE
    hn, cls, rank, glo, ghi, cnt = _router(x, sh, sc, p["g_moe"], p["wr_t"], p["br"], tm=tm)
    cls = cls.reshape(n)
    rank = rank.reshape(n)
    counts = cnt[:N_CLASSES, 0].astype(I32)
    padded = ((counts + tmx - 1) // tmx) * tmx
    ends = jnp.cumsum(padded)
    offs = ends - padded
    pos = offs[cls] + rank
    n_tiles = n // tmx + N_CLASSES
    n_rows = n_tiles * tmx
    total = ends[N_CLASSES - 1]
    tile_start = jnp.arange(n_tiles, dtype=I32) * tmx
    valid = (tile_start < total).astype(I32)
    last_tile = jnp.maximum(total // tmx - 1, 0)
    xblk = jnp.minimum(jnp.arange(n_tiles, dtype=I32), last_tile)
    oblk = jnp.where(valid == 1, jnp.arange(n_tiles, dtype=I32), n_tiles)
    tile_cls = jnp.minimum(jnp.sum((ends[None, :] <= (xblk * tmx)[:, None]).astype(I32), axis=1),
                           N_CLASSES - 1)
    grp = tile_cls // N_PAIRS
    pair = tile_cls % N_PAIRS
    e1 = grp * EXPERTS_PER_GROUP + jnp.asarray(PAIR_LO, I32)[pair]
    e2 = grp * EXPERTS_PER_GROUP + jnp.asarray(PAIR_HI, I32)[pair]
    src = jnp.zeros((n_rows,), I32).at[pos].set(jnp.arange(n, dtype=I32))
    xs = jnp.take(hn.reshape(n, d), src, axis=0)
    glo_s = jnp.zeros((n_rows, 1), F32).at[pos, 0].set(glo.reshape(n))
    ghi_s = jnp.zeros((n_rows, 1), F32).at[pos, 0].set(ghi.reshape(n))
    ys = _experts(xs, glo_s, ghi_s, p["w_up"], p["w_down"], e1, e2, valid, xblk, oblk, n_tiles)
    return jnp.take(ys, pos, axis=0).reshape(bsz, t_len, d)


AUG = 2 * HEAD_DIM
N_PIECES = 3


def _bf16_pieces(x):
    pieces = []
    rest = x
    for _ in range(N_PIECES):
        piece = rest.astype(BF16).astype(F32)
        pieces.append(piece)
        rest = rest - piece
    return pieces


def _post0_kernel(x_ref, y_ref, g2_ref, shk_ref, sck_ref, kvg_ref, wk_ref, wv_ref, wft_ref, bf_ref,
                  shq_ref, scq_ref, gq_ref, wq_ref, *refs, tm, seg, carry, prompt_layout):
    if prompt_layout:
        wvt_ref, x1_ref, k_ref, v_ref, lft_ref, fct_ref, qa_ref, ka_ref, vt_ref = refs[:9]
        scratch = refs[9:]
    else:
        x1_ref, k_ref, v_ref, lft_ref, fct_ref, q_ref, kb_ref, vb_ref = refs[:8]
        scratch = refs[8:]
    scale = HEAD_DIM ** -0.5
    x1 = x_ref[0] + g2_ref[0] * y_ref[0]
    x1_ref[0] = x1
    hk = _norm_mod(x1, kvg_ref[...], shk_ref[0], sck_ref[0])
    hkb = hk.astype(BF16)
    k = jnp.dot(hkb, wk_ref[...], preferred_element_type=F32)
    v = jnp.dot(hkb, wv_ref[...], preferred_element_type=F32)
    k_ref[0] = k
    v_ref[0] = v
    f = lax.dot_general(wft_ref[...], hk, NT_DIMS, precision=HIGHEST,
                        preferred_element_type=F32) + bf_ref[...]
    lf = _log_sigmoid(f)
    lft_ref[0] = lf
    col = lax.broadcasted_iota(I32, (1, tm), 1)
    cseg = col & (seg - 1)
    c = lf
    s = 1
    while s < seg:
        c = c + jnp.where(cseg >= s, pltpu.roll(c, s, axis=1), 0.0)
        s *= 2
    if carry:
        (f_scr,) = scratch

        @pl.when(pl.program_id(1) == 0)
        def _():
            f_scr[...] = jnp.zeros_like(f_scr)

        c = c + f_scr[...]
        f_scr[...] = c[:, tm - 1:tm]
    fct_ref[0] = c
    hqb = _norm_mod(x1, gq_ref[...], shq_ref[0], scq_ref[0]).astype(BF16)

    if not prompt_layout:
        q_ref[0] = (jnp.dot(hqb, wq_ref[...], preferred_element_type=F32) * scale).astype(BF16)
        kb_ref[0] = k.astype(BF16)
        vb_ref[0] = v.astype(BF16)
        return

    qt = lax.dot_general(wq_ref[...], hqb, NT_DIMS, preferred_element_type=F32) * scale
    vt_ref[0] = lax.dot_general(wvt_ref[...], hkb, NT_DIMS, preferred_element_type=F32).astype(BF16)
    hi, mid, lo = _bf16_pieces(c)
    sub = lax.broadcasted_iota(I32, (SUBLANES, tm), 0)
    pad = jnp.zeros((HEAD_DIM - SUBLANES, tm), F32)
    q_rows, k_rows = [], []
    for h in range(N_HEADS):
        hs = slice(h, h + 1)
        eq = jnp.where(sub < 3, -1.0, jnp.where(sub == 3, hi[hs], jnp.where(
            sub == 4, mid[hs], jnp.where(sub == 5, lo[hs], 0.0))))
        ek = jnp.where(sub == 0, hi[hs], jnp.where(sub == 1, mid[hs], jnp.where(
            sub == 2, lo[hs], jnp.where(sub < 6, 1.0, 0.0))))
        q_rows += [qt[h * HEAD_DIM:(h + 1) * HEAD_DIM], eq, pad]
        k_rows += [ek, pad]
    qa_ref[0] = jnp.concatenate(q_rows, axis=0).astype(BF16)
    ekt = jnp.concatenate(k_rows, axis=0).astype(BF16)
    ri = lax.broadcasted_iota(I32, (tm, tm), 0)
    ci = lax.broadcasted_iota(I32, (tm, tm), 1)
    eye = jnp.where(ri == ci, 1.0, 0.0).astype(BF16)
    extra = lax.dot_general(eye, ekt, NT_DIMS, preferred_element_type=F32)
    k_cols = []
    for h in range(N_HEADS):
        hs = slice(h * HEAD_DIM, (h + 1) * HEAD_DIM)
        k_cols += [k[:, hs], extra[:, hs]]
    ka_ref[0] = jnp.concatenate(k_cols, axis=1).astype(BF16)


def _post0(x, y, g2, shk, sck, shq, scq, p, *, tm, seg, carry, prompt_layout):
    bsz, t_len, d = x.shape
    tok = pl.BlockSpec((1, tm, d), lambda b, t: (b, t, 0))
    headrow = pl.BlockSpec((1, N_HEADS, tm), lambda b, t: (b, 0, t))
    weights_kv = [p["kv_g"], p["w_k"], p["w_v"], p["w_f_t"], p["b_f"]]
    tok_f32 = jax.ShapeDtypeStruct((bsz, t_len, d), F32)
    head_f32 = jax.ShapeDtypeStruct((bsz, N_HEADS, t_len), F32)
    if prompt_layout:
        weights_q = [p["g_mix1"], p["w_q_t"], p["w_v_t"]]
        out_specs = [tok, tok, tok, headrow, headrow,
                     pl.BlockSpec((1, N_HEADS * AUG, tm), lambda b, t: (b, 0, t)),
                     pl.BlockSpec((1, tm, N_HEADS * AUG), lambda b, t: (b, t, 0)),
                     pl.BlockSpec((1, d, tm), lambda b, t: (b, 0, t))]
        out_shape = [tok_f32] * 3 + [head_f32] * 2 + [
            jax.ShapeDtypeStruct((bsz, N_HEADS * AUG, t_len), BF16),
            jax.ShapeDtypeStruct((bsz, t_len, N_HEADS * AUG), BF16),
            jax.ShapeDtypeStruct((bsz, d, t_len), BF16)]
    else:
        weights_q = [p["g_mix1"], p["w_q"]]
        out_specs = [tok, tok, tok, headrow, headrow, tok, tok, tok]
        out_shape = [tok_f32] * 3 + [head_f32] * 2 + [jax.ShapeDtypeStruct((bsz, t_len, d), BF16)] * 3
    return pl.pallas_call(
        functools.partial(_post0_kernel, tm=tm, seg=seg, carry=carry, prompt_layout=prompt_layout),
        grid=(bsz, t_len // tm),
        in_specs=[tok, tok, _mod_spec(g2, tm), _mod_spec(shk, tm), _mod_spec(sck, tm)]
                 + [_full_spec(w) for w in weights_kv]
                 + [_mod_spec(shq, tm), _mod_spec(scq, tm)] + [_full_spec(w) for w in weights_q],
        out_specs=out_specs,
        out_shape=out_shape,
        scratch_shapes=[pltpu.VMEM((N_HEADS, 1), F32)] if carry else [],
        compiler_params=_params("arbitrary", "arbitrary"),
        name="shared_kv_and_q",
    )(x, y, g2, shk, sck, *weights_kv, shq, scq, *weights_q)


def _fox_prompt_kernel(qi_ref, kj_ref, qa_ref, ka_ref, vt_ref, o_ref, m_scr, l_scr, acc_scr, *, tq):
    p = pl.program_id(1)
    i = qi_ref[p]
    j = kj_ref[p]

    @pl.when(j == 0)
    def _():
        m_scr[...] = jnp.full_like(m_scr, -jnp.inf)
        l_scr[...] = jnp.zeros_like(l_scr)
        acc_scr[...] = jnp.zeros_like(acc_scr)

    def step(masked):
        if masked:
            key = lax.broadcasted_iota(I32, (tq, tq), 0)
            qry = lax.broadcasted_iota(I32, (tq, tq), 1)
        for h in range(N_HEADS):
            hs = slice(h * HEAD_DIM, (h + 1) * HEAD_DIM)
            s = jnp.dot(ka_ref[0, :, h * AUG:(h + 1) * AUG], qa_ref[0, h * AUG:(h + 1) * AUG, :],
                        preferred_element_type=F32)
            if masked:
                s = jnp.where(key <= qry, s, -jnp.inf)
            m_old = m_scr[h]
            m_new = jnp.maximum(m_old, jnp.max(s, axis=0, keepdims=True))
            alpha = jnp.exp(m_old - m_new)
            pr = jnp.exp(s - m_new)
            l_scr[h] = alpha * l_scr[h] + jnp.sum(pr, axis=0, keepdims=True)
            acc_scr[hs, :] = alpha * acc_scr[hs, :] + jnp.dot(
                vt_ref[0, hs, :], pr.astype(BF16), preferred_element_type=F32)
            m_scr[h] = m_new

    @pl.when(j < i)
    def _():
        step(False)

    @pl.when(j == i)
    def _():
        step(True)
        for h in range(N_HEADS):
            hs = slice(h * HEAD_DIM, (h + 1) * HEAD_DIM)
            o_ref[0, :, hs] = jnp.transpose(acc_scr[hs, :] / l_scr[h]).astype(BF16)


def _fox_prompt(qa, ka, vt):
    bsz, d, t_len = vt.shape
    tq = ATTN_TILE
    nb = t_len // tq
    pairs = [(i, j) for i in range(nb) for j in range(i + 1)]
    qi = jnp.asarray([a for a, _ in pairs], I32)
    kj = jnp.asarray([b for _, b in pairs], I32)
    grid_spec = pltpu.PrefetchScalarGridSpec(
        num_scalar_prefetch=2,
        grid=(bsz, len(pairs)),
        in_specs=[
            pl.BlockSpec((1, N_HEADS * AUG, tq), lambda b, p, qi, kj: (b, 0, qi[p])),
            pl.BlockSpec((1, tq, N_HEADS * AUG), lambda b, p, qi, kj: (b, kj[p], 0)),
            pl.BlockSpec((1, d, tq), lambda b, p, qi, kj: (b, 0, kj[p])),
        ],
        out_specs=pl.BlockSpec((1, tq, d), lambda b, p, qi, kj: (b, qi[p], 0)),
        scratch_shapes=[pltpu.VMEM((N_HEADS, 1, tq), F32), pltpu.VMEM((N_HEADS, 1, tq), F32),
                        pltpu.VMEM((d, tq), F32)],
    )
    return pl.pallas_call(
        functools.partial(_fox_prompt_kernel, tq=tq),
        grid_spec=grid_spec,
        out_shape=jax.ShapeDtypeStruct((bsz, t_len, d), BF16),
        compiler_params=_params("arbitrary", "arbitrary"),
        name="fox_prompt_attention",
    )(qi, kj, qa, ka, vt)


def _page_suffix(lf):
    lane = lax.broadcasted_iota(I32, lf.shape, 1)
    sub = lax.broadcasted_iota(I32, lf.shape, 0)
    x = lf
    sh = N_HEADS
    while sh < LANES:
        x = x + jnp.where(lane + sh < LANES, pltpu.roll(x, LANES - sh, axis=1), 0.0)
        sh *= 2
    y = jnp.where(lane < N_HEADS, x, 0.0)
    sh = N_HEADS
    while sh < LANES:
        y = y + pltpu.roll(y, sh, axis=1)
        sh *= 2
    z = y
    sh = 1
    while sh < SUBLANES:
        z = z + jnp.where(sub + sh < SUBLANES, pltpu.roll(z, SUBLANES - sh, axis=0), 0.0)
        sh *= 2
    return x + (z - y) - lf, z[0:1]


def _fox_decode_kernel(pt_ref, q_ref, kn_ref, vn_ref, cq_ref, ck_ref, *refs, pages):
    del pt_ref
    k_refs = refs[:pages]
    v_refs = refs[pages:2 * pages]
    lf_refs = refs[2 * pages:3 * pages]
    o_ref = refs[3 * pages]
    qm_scr, m_scr, l_scr, acc_scr, r_scr = refs[3 * pages + 1:]
    step = pl.program_id(1)
    n_tok = q_ref.shape[1]
    rows = n_tok * N_HEADS
    page_size = k_refs[0].shape[1]
    lane = lax.broadcasted_iota(I32, (rows, LANES), 1)
    row = lax.broadcasted_iota(I32, (rows, LANES), 0)
    row_head = row // n_tok
    row_tok = row - row_head * n_tok
    cq = cq_ref[0]

    def by_head(x):
        return jnp.concatenate(
            [x[:, h * HEAD_DIM:(h + 1) * HEAD_DIM] for h in range(N_HEADS)], axis=0)

    def attend(u, v_bf):
        m_old = m_scr[...]
        m_new = jnp.maximum(m_old, jnp.max(u, axis=1, keepdims=True) + cq)
        alpha = jnp.exp(m_old - m_new)
        pr = jnp.exp(u - (m_new - cq))
        l_scr[...] = alpha * l_scr[...] + jnp.sum(pr, axis=1, keepdims=True)
        acc_scr[...] = alpha * acc_scr[...] + jnp.dot(pr.astype(BF16), v_bf,
                                                      preferred_element_type=F32)
        m_scr[...] = m_new

    @pl.when(step == 0)
    def _():
        qm = by_head(q_ref[0].astype(F32)).astype(BF16)
        qm_scr[...] = qm
        m_scr[...] = jnp.full_like(m_scr, -jnp.inf)
        l_scr[...] = jnp.zeros_like(l_scr)
        acc_scr[...] = jnp.zeros_like(acc_scr)
        r_scr[...] = jnp.zeros_like(r_scr)
        pad = jnp.zeros((LANES - rows, HEAD_DIM), F32)
        kn = jnp.concatenate([by_head(kn_ref[0].astype(F32)), pad], axis=0).astype(BF16)
        vn = jnp.concatenate([by_head(vn_ref[0].astype(F32)), pad], axis=0).astype(BF16)
        s = lax.dot_general(qm, kn, NT_DIMS, preferred_element_type=F32)
        key_head = lane // n_tok
        key_tok = lane - key_head * n_tok
        u = jnp.where(key_head == row_head, jnp.where(key_tok <= row_tok, s - ck_ref[0], -jnp.inf),
                      -jnp.inf)
        attend(u, vn)

    own_head = (lane & (N_HEADS - 1)) == row_head
    qm = qm_scr[...]
    after_sum = r_scr[...]
    u_parts, v_parts = [], []
    for pg in range(pages):
        later, total = _page_suffix(lf_refs[pg][0])
        after = after_sum + later
        after_sum = after_sum + total
        k2 = k_refs[pg][0].reshape(page_size * N_HEADS, HEAD_DIM).astype(BF16)
        v_parts.append(v_refs[pg][0].reshape(page_size * N_HEADS, HEAD_DIM).astype(BF16))
        s = lax.dot_general(qm, k2, NT_DIMS, preferred_element_type=F32)
        u_parts += [jnp.where(own_head, s[:, c * LANES:(c + 1) * LANES] + after[c:c + 1], -jnp.inf)
                    for c in range(page_size * N_HEADS // LANES)]
    r_scr[...] = after_sum
    attend(jnp.concatenate(u_parts, axis=1), jnp.concatenate(v_parts, axis=0))

    @pl.when(step == pl.num_programs(1) - 1)
    def _():
        o = acc_scr[...] / l_scr[...]
        o_ref[0] = jnp.concatenate(
            [o[h * n_tok:(h + 1) * n_tok] for h in range(N_HEADS)], axis=1).astype(BF16)


def _fox_decode(q, k_new, v_new, cq, ck, cache_k, cache_v, cache_lf, page_table):
    bsz, n_tok, d = q.shape
    n_pages = page_table.shape[1]
    page_size = cache_k.shape[1]
    pages = PAGES_PER_STEP
    rows = n_tok * N_HEADS

    def page_map(pg, ndim):
        def index_map(b, s, pt):
            return (pt[b, n_pages - 1 - (s * pages + pg)],) + (0,) * (ndim - 1)
        return index_map

    seq = lambda b, s, pt: (b, 0, 0)
    in_specs = [pl.BlockSpec((1, n_tok, d), seq), pl.BlockSpec((1, n_tok, d), seq),
                pl.BlockSpec((1, n_tok, d), seq), pl.BlockSpec((1, rows, 1), seq),
                pl.BlockSpec((1, 1, LANES), seq)]
    kv_block = (1, page_size, N_HEADS, HEAD_DIM)
    in_specs += [pl.BlockSpec(kv_block, page_map(pg, 4)) for pg in range(pages)]
    in_specs += [pl.BlockSpec(kv_block, page_map(pg, 4)) for pg in range(pages)]
    in_specs += [pl.BlockSpec((1, SUBLANES, LANES), page_map(pg, 3)) for pg in range(pages)]
    grid_spec = pltpu.PrefetchScalarGridSpec(
        num_scalar_prefetch=1,
        grid=(bsz, n_pages // pages),
        in_specs=in_specs,
        out_specs=pl.BlockSpec((1, n_tok, d), seq),
        scratch_shapes=[pltpu.VMEM((rows, HEAD_DIM), BF16), pltpu.VMEM((rows, 1), F32),
                        pltpu.VMEM((rows, 1), F32), pltpu.VMEM((rows, HEAD_DIM), F32),
                        pltpu.VMEM((1, LANES), F32)],
    )
    return pl.pallas_call(
        functools.partial(_fox_decode_kernel, pages=pages),
        grid_spec=grid_spec,
        out_shape=jax.ShapeDtypeStruct((bsz, n_tok, d), BF16),
        compiler_params=_params("arbitrary", "arbitrary"),
        name="fox_decode_attention",
    )(page_table, q, k_new, v_new, cq, ck, *([cache_k] * pages), *([cache_v] * pages),
      *([cache_lf] * pages))


def _oproj_kernel(x_ref, o_ref, g1_ref, wo_ref, xo_ref):
    xo_ref[0] = x_ref[0] + g1_ref[0] * jnp.dot(o_ref[0], wo_ref[...], preferred_element_type=F32)


def _oproj(x, o, g1, w_o, *, tm):
    bsz, t_len, d = x.shape
    tok = pl.BlockSpec((1, tm, d), lambda b, t: (b, t, 0))
    return pl.pallas_call(
        _oproj_kernel,
        grid=(bsz, t_len // tm),
        in_specs=[tok, tok, _mod_spec(g1, tm), _full_spec(w_o)],
        out_specs=tok,
        out_shape=jax.ShapeDtypeStruct((bsz, t_len, d), F32),
        compiler_params=_params("arbitrary", "arbitrary"),
        name="attn_out_proj",
    )(x, o, g1, w_o)


def _final_kernel(x_ref, y_ref, g2_ref, g_ref, o_ref):
    x = x_ref[0] + g2_ref[0] * y_ref[0]
    o_ref[0] = (x * lax.rsqrt(jnp.mean(x * x, axis=-1, keepdims=True) + EPS)) * g_ref[...]


def _final(x, y, g2, g_final, *, tm):
    bsz, t_len, d = x.shape
    tok = pl.BlockSpec((1, tm, d), lambda b, t: (b, t, 0))
    return pl.pallas_call(
        _final_kernel,
        grid=(bsz, t_len // tm),
        in_specs=[tok, tok, _mod_spec(g2, tm), _full_spec(g_final)],
        out_specs=tok,
        out_shape=jax.ShapeDtypeStruct((bsz, t_len, d), F32),
        compiler_params=_params("arbitrary", "arbitrary"),
        name="final_norm",
    )(x, y, g2, g_final)


def _trunk(x, mods0, mods1, kvmods, prev, h0, w, attend, *, tm, seg, carry):
    sh1, sc1, g1, sh2, sc2, g2 = mods0
    x, h_out, conv_out = _lru_layer(x, sh1, sc1, g1, w["lru"], prev, h0, tm=tm, seg=seg, carry=carry)
    y = _moe(x, sh2, sc2, w["moe0"], tm=tm)
    sh1b, sc1b, g1b, sh2b, sc2b, g2b = mods1
    shk, sck = kvmods
    x, k, v, lft, fct, *operands = _post0(x, y, g2, shk, sck, sh1b, sc1b, w["post0"],
                                          tm=tm, seg=seg, carry=carry, prompt_layout=carry)
    o = attend(fct, *operands)
    x = _oproj(x, o, g1b, w["w_o"], tm=tm)
    y = _moe(x, sh2b, sc2b, w["moe1"], tm=tm)
    y_out = _final(x, y, g2b, w["g_final"], tm=tm)
    return y_out, h_out, conv_out, k, v, lft


def kernel(x_prompt, x_sample, state_h, state_conv, cache_k, cache_v, cache_logf, page_table,
           c_prompt, c_sample, w_mod, b_mod, g_mix, g_moe, lru_w_in, lru_conv_w, lru_conv_b,
           lru_w_a, lru_b_a, lru_w_x, lru_b_x, lru_lambda, lru_w_out, kv_g, kv_w_mod, kv_b_mod,
           kv_w, kv_b_f, attn_w_q, attn_w_o, moe_w_grp, moe_b_grp, moe_w_exp, moe_b_exp,
           moe_w_up, moe_w_down, g_final):
    d = D_MODEL
    bp, tp, _ = x_prompt.shape
    bs, ts, _ = x_sample.shape
    hd = N_HEADS * HEAD_DIM
    row = lambda a: a.reshape(1, -1)

    def moe_weights(layer):
        wr_t = jnp.concatenate(
            [moe_w_grp[layer].T, moe_w_exp[layer].T,
             jnp.zeros((ROUTER_ROWS - N_GROUPS - N_EXPERTS, d), F32)], axis=0)
        br = jnp.concatenate(
            [moe_b_grp[layer], moe_b_exp[layer],
             jnp.zeros((ROUTER_ROWS - N_GROUPS - N_EXPERTS,), F32)]).reshape(ROUTER_ROWS, 1)
        return dict(g_moe=row(g_moe[layer]), wr_t=wr_t, br=br,
                    w_up=moe_w_up[layer].astype(BF16), w_down=moe_w_down[layer].astype(BF16))

    w = dict(
        lru=dict(g_mix=row(g_mix[0]), w_in=lru_w_in[0].astype(BF16), conv_w=lru_conv_w[0],
                 conv_b=row(lru_conv_b[0]), w_a=lru_w_a[0].astype(BF16), b_a=row(lru_b_a[0]),
                 w_x=lru_w_x[0].astype(BF16), b_x=row(lru_b_x[0]), lam=row(lru_lambda[0]),
                 w_out=lru_w_out[0].astype(BF16)),
        moe0=moe_weights(0),
        moe1=moe_weights(1),
        post0=dict(kv_g=row(kv_g), w_k=kv_w[:, :hd].astype(BF16),
                   w_v=kv_w[:, hd:2 * hd].astype(BF16), w_f_t=kv_w[:, 2 * hd:].T,
                   w_v_t=kv_w[:, hd:2 * hd].T.astype(BF16), b_f=kv_b_f.reshape(N_HEADS, 1),
                   g_mix1=row(g_mix[1]), w_q=attn_w_q[0].astype(BF16),
                   w_q_t=attn_w_q[0].T.astype(BF16)),
        w_o=attn_w_o[0].astype(BF16),
        g_final=row(g_final),
    )

    n_c = bp + bs
    c_rows = -(-n_c // SUBLANES) * SUBLANES
    c_all = jnp.concatenate([c_prompt, c_sample, jnp.zeros((c_rows - n_c, d), F32)], axis=0)
    mod_l0 = _mm_bias(c_all, w_mod[0], b_mod[0])
    mod_l1 = _mm_bias(c_all, w_mod[1], b_mod[1])
    mod_kv = _mm_bias(c_all, kv_w_mod, kv_b_mod)

    def prompt_mods(m, parts):
        return [a.reshape(bp, 1, d) for a in jnp.split(m[:bp], parts, axis=-1)]

    def sample_mods(m, parts):
        return [jnp.repeat(a, ts, axis=0).reshape(1, bs * ts, d)
                for a in jnp.split(m[bp:n_c], parts, axis=-1)]

    prev_p = jnp.zeros((bp, SUBLANES, D_RNN), F32)
    h0_p = jnp.zeros((bp, 1, D_RNN), F32)
    attend_p = lambda fct, qa, ka, vt: _fox_prompt(qa, ka, vt)
    y_p, h_p, conv_p, k_p, v_p, lft_p = _trunk(
        x_prompt, prompt_mods(mod_l0, 6), prompt_mods(mod_l1, 6), prompt_mods(mod_kv, 2),
        prev_p, h0_p, w, attend_p, tm=TOKEN_TILE, seg=TOKEN_TILE, carry=True)

    n_s = bs * ts
    prev_s = jnp.pad(state_conv[0], ((0, 0), (ts - (CONV_WIDTH - 1), 0), (0, 0))).reshape(1, n_s, D_RNN)
    h0_s = jnp.repeat(state_h[0], ts, axis=0).reshape(1, n_s, D_RNN)
    n_pool, page_size = cache_k.shape[0], cache_k.shape[1]
    cache_lf = cache_logf.reshape(n_pool, SUBLANES, page_size * N_HEADS // SUBLANES)

    def attend_s(fct, q, kb, vb):
        c_new = jnp.transpose(fct.reshape(N_HEADS, bs, ts), (1, 0, 2)).reshape(bs, N_HEADS * ts)
        cq = c_new.reshape(bs, N_HEADS * ts, 1)
        ck = jnp.pad(c_new, ((0, 0), (0, LANES - N_HEADS * ts))).reshape(bs, 1, LANES)
        o = _fox_decode(q.reshape(bs, ts, d), kb.reshape(bs, ts, d), vb.reshape(bs, ts, d), cq, ck,
                        cache_k, cache_v, cache_lf, page_table)
        return o.reshape(1, n_s, d)

    y_s, h_s, conv_s, k_s, v_s, lft_s = _trunk(
        x_sample.reshape(1, n_s, d), sample_mods(mod_l0, 6), sample_mods(mod_l1, 6),
        sample_mods(mod_kv, 2), prev_s, h0_s, w, attend_s, tm=n_s, seg=ts, carry=False)

    n_conv = CONV_WIDTH - 1
    return (
        y_p,
        y_s.reshape(bs, ts, d),
        h_p.reshape(1, bp, D_RNN),
        conv_p[:, SUBLANES - n_conv:].reshape(1, bp, n_conv, D_RNN),
        k_p.reshape(bp, tp, N_HEADS, HEAD_DIM),
        v_p.reshape(bp, tp, N_HEADS, HEAD_DIM),
        jnp.transpose(lft_p, (0, 2, 1)),
        h_s.reshape(bs, ts, D_RNN)[:, ts - 1].reshape(1, bs, D_RNN),
        conv_s.reshape(bs, ts, D_RNN)[:, ts - n_conv:].reshape(1, bs, n_conv, D_RNN),
        k_s.reshape(bs, ts, N_HEADS, HEAD_DIM),
        v_s.reshape(bs, ts, N_HEADS, HEAD_DIM),
        jnp.transpose(lft_s.reshape(N_HEADS, bs, ts), (1, 2, 0)),
    )
```

```python
import functools
import math

import jax
import jax.numpy as jnp
from jax import lax
from jax.experimental import pallas as pl
from jax.experimental.pallas import tpu as pltpu

F32 = jnp.float32
BF16 = jnp.bfloat16
I32 = jnp.int32

D_MODEL = 1024
D_RNN = D_MODEL
N_LRU_BLOCKS = 8
LRU_BLOCK = D_RNN // N_LRU_BLOCKS
CONV_WIDTH = 4
LRU_C = 8.0
N_HEADS = 8
HEAD_DIM = D_MODEL // N_HEADS
N_GROUPS = 4
EXPERTS_PER_GROUP = 4
N_EXPERTS = N_GROUPS * EXPERTS_PER_GROUP
D_EXPERT = D_MODEL // 2
EPS = 1e-6
N_PAIRS = 6
N_CLASSES = N_GROUPS * N_PAIRS
PAIR_LO = (0, 0, 0, 1, 1, 2)
PAIR_HI = (1, 2, 3, 2, 3, 3)

SUBLANES = 8
LANES = 128
VMEM_LIMIT = 48 * 1024 * 1024

TOKEN_TILE = 256
EXPERT_TILE = 256
SAMPLE_EXPERT_TILE = 32
ROUTER_ROWS = 32
ATTN_TILE = 512
PAGES_PER_STEP = 8
MOD_TILE = 512

HIGHEST = lax.Precision.HIGHEST
NT_DIMS = (((1,), (1,)), ((), ()))


def _params(*semantics):
    return pltpu.CompilerParams(dimension_semantics=semantics, vmem_limit_bytes=VMEM_LIMIT)


def _norm_mod(x, g, shift, scale):
    y = x * lax.rsqrt(jnp.mean(x * x, axis=-1, keepdims=True) + EPS)
    return (y * g) * (1.0 + scale) + shift


def _log_sigmoid(x):
    return jnp.minimum(x, 0.0) - jnp.log1p(jnp.exp(-jnp.abs(x)))


def _sigmoid(x):
    return 1.0 / (1.0 + jnp.exp(-x))


def _gelu_tanh(x):
    c = math.sqrt(2.0 / math.pi)
    return x * (0.5 * (1.0 + jnp.tanh(c * (x + 0.044715 * (x * x * x)))))


def _mod_spec(arr, tm):
    if arr.shape[1] == 1:
        return pl.BlockSpec((1, 1, arr.shape[2]), lambda b, t: (b, 0, 0))
    return pl.BlockSpec((1, tm, arr.shape[2]), lambda b, t: (b, t, 0))


def _row_spec(nt, tm, d):
    return pl.BlockSpec((tm, d), lambda b, t: (b * nt + t, 0))


def _full_spec(arr):
    zeros = (0,) * arr.ndim
    return pl.BlockSpec(arr.shape, lambda *_: zeros)


def _mm_bias_kernel(x_ref, w_ref, b_ref, o_ref):
    o_ref[...] = jnp.dot(x_ref[...].astype(BF16), w_ref[...].astype(BF16),
                         preferred_element_type=F32) + b_ref[...]


def _mm_bias(x, w, b):
    m, d = x.shape
    n = w.shape[1]
    return pl.pallas_call(
        _mm_bias_kernel,
        grid=(n // MOD_TILE,),
        in_specs=[pl.BlockSpec((m, d), lambda j: (0, 0)),
                  pl.BlockSpec((d, MOD_TILE), lambda j: (0, j)),
                  pl.BlockSpec((1, MOD_TILE), lambda j: (0, j))],
        out_specs=pl.BlockSpec((m, MOD_TILE), lambda j: (0, j)),
        out_shape=jax.ShapeDtypeStruct((m, n), F32),
        compiler_params=_params("arbitrary"),
        name="mod_vectors",
    )(x, w, b.reshape(1, n))


def _lru_kernel(x_ref, sh_ref, sc_ref, gt_ref, gmix_ref, win_ref, cw_ref, cb_ref, wa_ref, ba_ref,
                wx_ref, bx_ref, lam_ref, wout_ref, prev_ref, h0_ref,
                xo_ref, hs_ref, xb_ref, *scratch, tm, seg, carry):
    x = x_ref[0]
    hn = _norm_mod(x, gmix_ref[...], sh_ref[0], sc_ref[0])
    xg = jnp.dot(hn.astype(BF16), win_ref[...], preferred_element_type=F32)
    xb = xg[:, :D_RNN]
    gb = xg[:, D_RNN:]

    if carry:
        prev_scr, h_scr = scratch

        @pl.when(pl.program_id(1) == 0)
        def _():
            prev_scr[...] = prev_ref[0]
            h_scr[...] = h0_ref[0]

        prev = prev_scr[...]
        hprev = h_scr[...]
    else:
        prev = prev_ref[0]
        hprev = h0_ref[0]

    row = lax.broadcasted_iota(I32, (tm, 1), 0)
    rseg = row & (seg - 1)
    nprev = prev.shape[0]

    cw = cw_ref[...]
    xc = cb_ref[...] + cw[CONV_WIDTH - 1:CONV_WIDTH] * xb
    for d in range(1, CONV_WIDTH):
        rolled = pltpu.roll(xb, d, axis=0)
        rp = pltpu.roll(prev, nprev - SUBLANES + d, axis=0)
        if carry:
            head = jnp.where(row[:SUBLANES] < d, rp, rolled[:SUBLANES])
            shifted = head if tm == SUBLANES else jnp.concatenate([head, rolled[SUBLANES:]], axis=0)
        else:
            shifted = jnp.where(rseg < d, rp, rolled)
        xc = xc + cw[CONV_WIDTH - 1 - d:CONV_WIDTH - d] * shifted

    xcb = xc.astype(BF16)
    ra = jnp.concatenate(
        [jnp.dot(xcb[:, n * LRU_BLOCK:(n + 1) * LRU_BLOCK], wa_ref[n], preferred_element_type=F32)
         for n in range(N_LRU_BLOCKS)], axis=1)
    rx = jnp.concatenate(
        [jnp.dot(xcb[:, n * LRU_BLOCK:(n + 1) * LRU_BLOCK], wx_ref[n], preferred_element_type=F32)
         for n in range(N_LRU_BLOCKS)], axis=1)
    r = _sigmoid(ra + ba_ref[...])
    ig = _sigmoid(rx + bx_ref[...])
    log_a = (LRU_C * r) * _log_sigmoid(lam_ref[...])
    a = jnp.exp(log_a)
    bt = jnp.sqrt(-jnp.tanh(log_a) * (a * a + 1.0)) * (ig * xc)

    s = 1
    while s < seg:
        inside = rseg >= s
        a_sh = jnp.where(inside, pltpu.roll(a, s, axis=0), 1.0)
        b_sh = jnp.where(inside, pltpu.roll(bt, s, axis=0), 0.0)
        bt = a * b_sh + bt
        a = a * a_sh
        s *= 2
    hs = bt + a * hprev

    if carry:
        h_scr[...] = hs[tm - 1:tm]
        prev_scr[...] = xb[tm - SUBLANES:]
        hs_ref[0] = hs[tm - 1:tm]
        xb_ref[0] = xb[tm - SUBLANES:]
    else:
        hs_ref[0] = hs
        xb_ref[0] = xb

    y = (hs * _gelu_tanh(gb)).astype(BF16)
    out = jnp.dot(y, wout_ref[...], preferred_element_type=F32)
    xo_ref[0] = x + gt_ref[0] * out


def _lru_layer(x, sh, sc, gt, p, prev, h0, *, tm, seg, carry):
    bsz, t_len, d = x.shape
    r = D_RNN
    nt = t_len // tm
    tok = pl.BlockSpec((1, tm, d), lambda b, t: (b, t, 0))
    if carry:
        state_specs = [pl.BlockSpec((1, SUBLANES, r), lambda b, t: (b, 0, 0)),
                       pl.BlockSpec((1, 1, r), lambda b, t: (b, 0, 0))]
        out_specs = [tok,
                     pl.BlockSpec((1, 1, r), lambda b, t: (b, 0, 0)),
                     pl.BlockSpec((1, SUBLANES, r), lambda b, t: (b, 0, 0))]
        out_shape = [jax.ShapeDtypeStruct((bsz, t_len, d), F32),
                     jax.ShapeDtypeStruct((bsz, 1, r), F32),
                     jax.ShapeDtypeStruct((bsz, SUBLANES, r), F32)]
        scratch = [pltpu.VMEM((SUBLANES, r), F32), pltpu.VMEM((1, r), F32)]
    else:
        rtok = pl.BlockSpec((1, tm, r), lambda b, t: (b, t, 0))
        state_specs = [rtok, rtok]
        out_specs = [tok, rtok, rtok]
        out_shape = [jax.ShapeDtypeStruct((bsz, t_len, d), F32),
                     jax.ShapeDtypeStruct((bsz, t_len, r), F32),
                     jax.ShapeDtypeStruct((bsz, t_len, r), F32)]
        scratch = []
    weights = [p["g_mix"], p["w_in"], p["conv_w"], p["conv_b"], p["w_a"], p["b_a"], p["w_x"],
               p["b_x"], p["lam"], p["w_out"]]
    return pl.pallas_call(
        functools.partial(_lru_kernel, tm=tm, seg=seg, carry=carry),
        grid=(bsz, nt),
        in_specs=[tok, _mod_spec(sh, tm), _mod_spec(sc, tm), _mod_spec(gt, tm)]
                 + [_full_spec(w) for w in weights] + state_specs,
        out_specs=out_specs,
        out_shape=out_shape,
        scratch_shapes=scratch,
        compiler_params=_params("arbitrary", "arbitrary"),
        name="rglru_layer",
    )(x, sh, sc, gt, *weights, prev, h0)


def _router_kernel(x_ref, sh_ref, sc_ref, g_ref, wr_ref, br_ref,
                   hn_ref, cls_ref, rank_ref, glo_ref, ghi_ref, cnt_ref, carry_scr, *, tm):
    @pl.when((pl.program_id(0) == 0) & (pl.program_id(1) == 0))
    def _():
        carry_scr[...] = jnp.zeros_like(carry_scr)

    hn = _norm_mod(x_ref[0], g_ref[...], sh_ref[0], sc_ref[0])
    hn_ref[...] = hn
    lt = lax.dot_general(wr_ref[...], hn, NT_DIMS, precision=HIGHEST,
                         preferred_element_type=F32) + br_ref[...]

    gl = [lt[k:k + 1] for k in range(N_GROUPS)]
    gmax = jnp.maximum(jnp.maximum(gl[0], gl[1]), jnp.maximum(gl[2], gl[3]))
    g_sel = jnp.where(gl[0] >= gmax, 0, jnp.where(gl[1] >= gmax, 1, jnp.where(gl[2] >= gmax, 2, 3)))
    p_grp = 1.0 / (jnp.exp(gl[0] - gmax) + jnp.exp(gl[1] - gmax)
                   + jnp.exp(gl[2] - gmax) + jnp.exp(gl[3] - gmax))

    def expert_logit(e):
        rows = [lt[N_GROUPS + g * EXPERTS_PER_GROUP + e:N_GROUPS + g * EXPERTS_PER_GROUP + e + 1]
                for g in range(N_GROUPS)]
        return jnp.where(g_sel == 0, rows[0],
                         jnp.where(g_sel == 1, rows[1], jnp.where(g_sel == 2, rows[2], rows[3])))

    es = [expert_logit(e) for e in range(EXPERTS_PER_GROUP)]

    def first_argmax(vals):
        vmax = jnp.maximum(jnp.maximum(vals[0], vals[1]), jnp.maximum(vals[2], vals[3]))
        idx = jnp.where(vals[0] >= vmax, 0,
                        jnp.where(vals[1] >= vmax, 1, jnp.where(vals[2] >= vmax, 2, 3)))
        return vmax, idx

    v1, i1 = first_argmax(es)
    rest = [jnp.where(i1 == e, -jnp.inf, es[e]) for e in range(EXPERTS_PER_GROUP)]
    v2, i2 = first_argmax(rest)
    e21 = jnp.exp(v2 - v1)
    w1 = (1.0 / (1.0 + e21)) * p_grp
    w2 = (e21 / (1.0 + e21)) * p_grp
    first_lower = i1 < i2
    lo = jnp.where(first_lower, i1, i2)
    hi = jnp.where(first_lower, i2, i1)
    glo_ref[0] = jnp.where(first_lower, w1, w2)
    ghi_ref[0] = jnp.where(first_lower, w2, w1)
    pair = jnp.where(lo == 0, hi - 1, jnp.where(lo == 1, hi + 1, 5))
    cls = g_sel * N_PAIRS + pair
    cls_ref[0] = cls

    crow = lax.broadcasted_iota(I32, (ROUTER_ROWS, tm), 0)
    onehot = jnp.where(crow == cls, 1.0, 0.0)
    ki = lax.broadcasted_iota(I32, (tm, tm), 0)
    kj = lax.broadcasted_iota(I32, (tm, tm), 1)
    upper = jnp.where(ki <= kj, 1.0, 0.0).astype(BF16)
    cum = jnp.dot(onehot.astype(BF16), upper, preferred_element_type=F32)
    carry = carry_scr[...]
    rank = jnp.sum(onehot * (cum - 1.0 + carry), axis=0, keepdims=True)
    rank_ref[0] = rank.astype(I32)
    carry = carry + cum[:, tm - 1:tm]
    carry_scr[...] = carry
    cnt_ref[...] = carry


def _router(x, sh, sc, g, wr_t, br, *, tm):
    bsz, t_len, d = x.shape
    nt = t_len // tm
    n_tiles = bsz * nt
    tok = pl.BlockSpec((1, tm, d), lambda b, t: (b, t, 0))
    lane_row = pl.BlockSpec((1, 1, tm), lambda b, t: (b * nt + t, 0, 0))
    row_shape = jax.ShapeDtypeStruct((n_tiles, 1, tm), F32)
    row_shape_i = jax.ShapeDtypeStruct((n_tiles, 1, tm), I32)
    return pl.pallas_call(
        functools.partial(_router_kernel, tm=tm),
        grid=(bsz, nt),
        in_specs=[tok, _mod_spec(sh, tm), _mod_spec(sc, tm), _full_spec(g), _full_spec(wr_t),
                  _full_spec(br)],
        out_specs=[pl.BlockSpec((tm, d), lambda b, t: (b * nt + t, 0)),
                   lane_row, lane_row, lane_row, lane_row,
                   pl.BlockSpec((ROUTER_ROWS, 1), lambda b, t: (0, 0))],
        out_shape=[jax.ShapeDtypeStruct((bsz * t_len, d), F32), row_shape_i, row_shape_i,
                   row_shape, row_shape, jax.ShapeDtypeStruct((ROUTER_ROWS, 1), F32)],
        scratch_shapes=[pltpu.VMEM((ROUTER_ROWS, 1), F32)],
        compiler_params=_params("arbitrary", "arbitrary"),
        name="moe_router",
    )(x, sh, sc, g, wr_t, br)


def _expert_kernel(e1_ref, e2_ref, valid_ref, src0_ref, srcn_ref, dst_ref, glo_ref, ghi_ref,
                   hn_hbm, wu1_ref, wd1_ref, wu2_ref, wd2_ref, y_hbm,
                   xbuf, obuf, sem_g, sem_s, *, tmx):
    del e1_ref, e2_ref
    j = pl.program_id(0)
    last = pl.num_programs(0) - 1
    slot = lax.rem(j, 2)

    def start_gather(idx_ref, to_slot):
        for r in range(tmx):
            pltpu.make_async_copy(hn_hbm.at[pl.ds(idx_ref[0, 0, r], 1)],
                                  xbuf.at[to_slot, pl.ds(r, 1)], sem_g.at[to_slot]).start()

    def wait_gather(of_slot):
        pltpu.make_async_copy(hn_hbm.at[pl.ds(0, tmx)], xbuf.at[of_slot], sem_g.at[of_slot]).wait()

    def wait_scatter():
        pltpu.make_async_copy(obuf.at[0], y_hbm.at[pl.ds(0, tmx)], sem_s.at[0]).wait()

    @pl.when(j == 0)
    def _():
        start_gather(src0_ref, 0)
        obuf[1] = jnp.zeros((tmx, obuf.shape[2]), F32)
        spare = pltpu.make_async_copy(obuf.at[1], y_hbm.at[pl.ds(y_hbm.shape[0] - tmx, tmx)],
                                      sem_s.at[0])
        spare.start()
        spare.wait()

    @pl.when(valid_ref[j] == 1)
    def _():
        wait_gather(slot)
        start_gather(srcn_ref, 1 - slot)
        x = xbuf[slot].astype(BF16)
        ri = lax.broadcasted_iota(I32, (tmx, tmx), 0)
        ci = lax.broadcasted_iota(I32, (tmx, tmx), 1)

        def column(row_ref):
            return jnp.sum(jnp.where(ri == ci, row_ref[0], 0.0), axis=1, keepdims=True)

        def ffn(wu_ref, wd_ref, gate):
            gu = jnp.dot(x, wu_ref[0], preferred_element_type=F32)
            g_ = gu[:, :D_EXPERT]
            u_ = gu[:, D_EXPERT:]
            act = ((g_ * _sigmoid(g_)) * u_).astype(BF16)
            return gate * jnp.dot(act, wd_ref[0], preferred_element_type=F32)

        obuf[slot] = (ffn(wu1_ref, wd1_ref, column(glo_ref)) + ffn(wu2_ref, wd2_ref, column(ghi_ref)))

        @pl.when(j > 0)
        def _():
            wait_scatter()

        for r in range(tmx):
            pltpu.make_async_copy(obuf.at[slot, pl.ds(r, 1)],
                                  y_hbm.at[pl.ds(dst_ref[0, 0, r], 1)], sem_s.at[0]).start()

        @pl.when((j == last) | (valid_ref[jnp.minimum(j + 1, last)] == 0))
        def _():
            wait_scatter()
            wait_gather(1 - slot)


def _experts(hn, src, dst, glo_s, ghi_s, w_up, w_down, e1, e2, valid, n_tiles, tmx):
    n, d = hn.shape
    idx_block = (1, 1, tmx)
    at_tile = lambda j, e1, e2, v: (j, 0, 0)
    grid_spec = pltpu.PrefetchScalarGridSpec(
        num_scalar_prefetch=3,
        grid=(n_tiles,),
        in_specs=[
            pl.BlockSpec(idx_block, lambda j, e1, e2, v: (0, 0, 0), memory_space=pltpu.SMEM),
            pl.BlockSpec(idx_block, lambda j, e1, e2, v: (jnp.minimum(j + 1, n_tiles - 1), 0, 0),
                         memory_space=pltpu.SMEM),
            pl.BlockSpec(idx_block, at_tile, memory_space=pltpu.SMEM),
            pl.BlockSpec(idx_block, at_tile),
            pl.BlockSpec(idx_block, at_tile),
            pl.BlockSpec(memory_space=pl.ANY),
            pl.BlockSpec((1, d, 2 * D_EXPERT), lambda j, e1, e2, v: (e1[j], 0, 0)),
            pl.BlockSpec((1, D_EXPERT, d), lambda j, e1, e2, v: (e1[j], 0, 0)),
            pl.BlockSpec((1, d, 2 * D_EXPERT), lambda j, e1, e2, v: (e2[j], 0, 0)),
            pl.BlockSpec((1, D_EXPERT, d), lambda j, e1, e2, v: (e2[j], 0, 0)),
        ],
        out_specs=pl.BlockSpec(memory_space=pl.ANY),
        scratch_shapes=[pltpu.VMEM((2, tmx, d), F32), pltpu.VMEM((2, tmx, d), F32),
                        pltpu.SemaphoreType.DMA((2,)), pltpu.SemaphoreType.DMA((1,))],
    )
    return pl.pallas_call(
        functools.partial(_expert_kernel, tmx=tmx),
        grid_spec=grid_spec,
        out_shape=jax.ShapeDtypeStruct((n + tmx, d), F32),
        compiler_params=_params("arbitrary"),
        name="moe_experts",
    )(e1, e2, valid, src, src, dst, glo_s, ghi_s, hn, w_up, w_down, w_up, w_down)


def _moe(x, sh, sc, p, *, tm, tmx):
    bsz, t_len, d = x.shape
    n = bsz * t_len
    hn, cls, rank, glo, ghi, cnt = _router(x, sh, sc, p["g_moe"], p["wr_t"], p["br"], tm=tm)
    cls = cls.reshape(n)
    rank = rank.reshape(n)
    counts = cnt[:N_CLASSES, 0].astype(I32)
    padded = ((counts + tmx - 1) // tmx) * tmx
    ends = jnp.cumsum(padded)
    offs = ends - padded
    pos = offs[cls] + rank
    n_tiles = n // tmx + N_CLASSES
    n_rows = n_tiles * tmx
    total = ends[N_CLASSES - 1]
    tile_start = jnp.arange(n_tiles, dtype=I32) * tmx
    valid = (tile_start < total).astype(I32)
    last_tile = jnp.maximum(total // tmx - 1, 0)
    tile_cls = jnp.minimum(
        jnp.sum((ends[None, :] <= jnp.minimum(tile_start, last_tile * tmx)[:, None]).astype(I32),
                axis=1), N_CLASSES - 1)
    grp = tile_cls // N_PAIRS
    pair = tile_cls % N_PAIRS
    e1 = grp * EXPERTS_PER_GROUP + jnp.asarray(PAIR_LO, I32)[pair]
    e2 = grp * EXPERTS_PER_GROUP + jnp.asarray(PAIR_HI, I32)[pair]
    token_info = jnp.stack([jnp.arange(n, dtype=I32), lax.bitcast_convert_type(glo.reshape(n), I32),
                            lax.bitcast_convert_type(ghi.reshape(n), I32), jnp.ones((n,), I32)], axis=1)
    row_info = jnp.zeros((n_rows, 4), I32).at[pos].set(token_info)
    src = row_info[:, 0]
    dst = jnp.where(row_info[:, 3] == 1, src, n + jnp.arange(n_rows, dtype=I32) % tmx)
    as_tiles = lambda a: a.reshape(n_tiles, 1, tmx)
    glo_s = as_tiles(lax.bitcast_convert_type(row_info[:, 1], F32))
    ghi_s = as_tiles(lax.bitcast_convert_type(row_info[:, 2], F32))
    return _experts(hn, as_tiles(src), as_tiles(dst), glo_s, ghi_s, p["w_up"], p["w_down"],
                    e1, e2, valid, n_tiles, tmx)


AUG = 2 * HEAD_DIM
N_PIECES = 3


def _bf16_pieces(x):
    pieces = []
    rest = x
    for _ in range(N_PIECES):
        piece = rest.astype(BF16).astype(F32)
        pieces.append(piece)
        rest = rest - piece
    return pieces


def _post0_kernel(x_ref, y_ref, g2_ref, shk_ref, sck_ref, kvg_ref, wk_ref, wv_ref, wft_ref, bf_ref,
                  shq_ref, scq_ref, gq_ref, wq_ref, *refs, tm, seg, carry, prompt_layout):
    if prompt_layout:
        wvt_ref, x1_ref, k_ref, v_ref, lft_ref, fct_ref, qa_ref, ka_ref, vt_ref = refs[:9]
        scratch = refs[9:]
    else:
        x1_ref, k_ref, v_ref, lft_ref, fct_ref, q_ref, kb_ref, vb_ref = refs[:8]
        scratch = refs[8:]
    scale = HEAD_DIM ** -0.5
    x1 = x_ref[0] + g2_ref[0] * y_ref[...]
    x1_ref[0] = x1
    hk = _norm_mod(x1, kvg_ref[...], shk_ref[0], sck_ref[0])
    hkb = hk.astype(BF16)
    k = jnp.dot(hkb, wk_ref[...], preferred_element_type=F32)
    v = jnp.dot(hkb, wv_ref[...], preferred_element_type=F32)
    k_ref[0] = k
    v_ref[0] = v
    f = lax.dot_general(wft_ref[...], hk, NT_DIMS, precision=HIGHEST,
                        preferred_element_type=F32) + bf_ref[...]
    lf = _log_sigmoid(f)
    lft_ref[0] = lf
    col = lax.broadcasted_iota(I32, (1, tm), 1)
    cseg = col & (seg - 1)
    c = lf
    s = 1
    while s < seg:
        c = c + jnp.where(cseg >= s, pltpu.roll(c, s, axis=1), 0.0)
        s *= 2
    if carry:
        (f_scr,) = scratch

        @pl.when(pl.program_id(1) == 0)
        def _():
            f_scr[...] = jnp.zeros_like(f_scr)

        c = c + f_scr[...]
        f_scr[...] = c[:, tm - 1:tm]
    fct_ref[0] = c
    hqb = _norm_mod(x1, gq_ref[...], shq_ref[0], scq_ref[0]).astype(BF16)

    if not prompt_layout:
        q_ref[0] = (jnp.dot(hqb, wq_ref[...], preferred_element_type=F32) * scale).astype(BF16)
        kb_ref[0] = k.astype(BF16)
        vb_ref[0] = v.astype(BF16)
        return

    qt = lax.dot_general(wq_ref[...], hqb, NT_DIMS, preferred_element_type=F32) * scale
    vt_ref[0] = lax.dot_general(wvt_ref[...], hkb, NT_DIMS, preferred_element_type=F32).astype(BF16)
    hi, mid, lo = _bf16_pieces(c)
    sub = lax.broadcasted_iota(I32, (SUBLANES, tm), 0)
    pad = jnp.zeros((HEAD_DIM - SUBLANES, tm), F32)
    q_rows, k_rows = [], []
    for h in range(N_HEADS):
        hs = slice(h, h + 1)
        eq = jnp.where(sub < 3, -1.0, jnp.where(sub == 3, hi[hs], jnp.where(
            sub == 4, mid[hs], jnp.where(sub == 5, lo[hs], 0.0))))
        ek = jnp.where(sub == 0, hi[hs], jnp.where(sub == 1, mid[hs], jnp.where(
            sub == 2, lo[hs], jnp.where(sub < 6, 1.0, 0.0))))
        q_rows += [qt[h * HEAD_DIM:(h + 1) * HEAD_DIM], eq, pad]
        k_rows += [ek, pad]
    qa_ref[0] = jnp.concatenate(q_rows, axis=0).astype(BF16)
    ekt = jnp.concatenate(k_rows, axis=0).astype(BF16)
    ri = lax.broadcasted_iota(I32, (tm, tm), 0)
    ci = lax.broadcasted_iota(I32, (tm, tm), 1)
    eye = jnp.where(ri == ci, 1.0, 0.0).astype(BF16)
    extra = lax.dot_general(eye, ekt, NT_DIMS, preferred_element_type=F32)
    k_cols = []
    for h in range(N_HEADS):
        hs = slice(h * HEAD_DIM, (h + 1) * HEAD_DIM)
        k_cols += [k[:, hs], extra[:, hs]]
    ka_ref[0] = jnp.concatenate(k_cols, axis=1).astype(BF16)


def _post0(x, y, g2, shk, sck, shq, scq, p, *, tm, seg, carry, prompt_layout):
    bsz, t_len, d = x.shape
    tok = pl.BlockSpec((1, tm, d), lambda b, t: (b, t, 0))
    headrow = pl.BlockSpec((1, N_HEADS, tm), lambda b, t: (b, 0, t))
    weights_kv = [p["kv_g"], p["w_k"], p["w_v"], p["w_f_t"], p["b_f"]]
    tok_f32 = jax.ShapeDtypeStruct((bsz, t_len, d), F32)
    head_f32 = jax.ShapeDtypeStruct((bsz, N_HEADS, t_len), F32)
    if prompt_layout:
        weights_q = [p["g_mix1"], p["w_q_t"], p["w_v_t"]]
        out_specs = [tok, tok, tok, headrow, headrow,
                     pl.BlockSpec((1, N_HEADS * AUG, tm), lambda b, t: (b, 0, t)),
                     pl.BlockSpec((1, tm, N_HEADS * AUG), lambda b, t: (b, t, 0)),
                     pl.BlockSpec((1, d, tm), lambda b, t: (b, 0, t))]
        out_shape = [tok_f32] * 3 + [head_f32] * 2 + [
            jax.ShapeDtypeStruct((bsz, N_HEADS * AUG, t_len), BF16),
            jax.ShapeDtypeStruct((bsz, t_len, N_HEADS * AUG), BF16),
            jax.ShapeDtypeStruct((bsz, d, t_len), BF16)]
    else:
        weights_q = [p["g_mix1"], p["w_q"]]
        out_specs = [tok, tok, tok, headrow, headrow, tok, tok, tok]
        out_shape = [tok_f32] * 3 + [head_f32] * 2 + [jax.ShapeDtypeStruct((bsz, t_len, d), BF16)] * 3
    return pl.pallas_call(
        functools.partial(_post0_kernel, tm=tm, seg=seg, carry=carry, prompt_layout=prompt_layout),
        grid=(bsz, t_len // tm),
        in_specs=[tok, _row_spec(t_len // tm, tm, d), _mod_spec(g2, tm), _mod_spec(shk, tm),
                  _mod_spec(sck, tm)]
                 + [_full_spec(w) for w in weights_kv]
                 + [_mod_spec(shq, tm), _mod_spec(scq, tm)] + [_full_spec(w) for w in weights_q],
        out_specs=out_specs,
        out_shape=out_shape,
        scratch_shapes=[pltpu.VMEM((N_HEADS, 1), F32)] if carry else [],
        compiler_params=_params("arbitrary", "arbitrary"),
        name="shared_kv_and_q",
    )(x, y, g2, shk, sck, *weights_kv, shq, scq, *weights_q)


def _fox_prompt_kernel(qi_ref, kj_ref, qa_ref, ka_ref, vt_ref, o_ref, m_scr, l_scr, acc_scr, *, tq):
    p = pl.program_id(1)
    i = qi_ref[p]
    j = kj_ref[p]

    @pl.when(j == 0)
    def _():
        m_scr[...] = jnp.full_like(m_scr, -jnp.inf)
        l_scr[...] = jnp.zeros_like(l_scr)
        acc_scr[...] = jnp.zeros_like(acc_scr)

    def step(masked):
        if masked:
            key = lax.broadcasted_iota(I32, (tq, tq), 0)
            qry = lax.broadcasted_iota(I32, (tq, tq), 1)
        for h in range(N_HEADS):
            hs = slice(h * HEAD_DIM, (h + 1) * HEAD_DIM)
            s = jnp.dot(ka_ref[0, :, h * AUG:(h + 1) * AUG], qa_ref[0, h * AUG:(h + 1) * AUG, :],
                        preferred_element_type=F32)
            if masked:
                s = jnp.where(key <= qry, s, -jnp.inf)
            m_old = m_scr[h]
            m_new = jnp.maximum(m_old, jnp.max(s, axis=0, keepdims=True))
            alpha = jnp.exp(m_old - m_new)
            pr = jnp.exp(s - m_new)
            l_scr[h] = alpha * l_scr[h] + jnp.sum(pr, axis=0, keepdims=True)
            acc_scr[hs, :] = alpha * acc_scr[hs, :] + jnp.dot(
                vt_ref[0, hs, :], pr.astype(BF16), preferred_element_type=F32)
            m_scr[h] = m_new

    @pl.when(j < i)
    def _():
        step(False)

    @pl.when(j == i)
    def _():
        step(True)
        for h in range(N_HEADS):
            hs = slice(h * HEAD_DIM, (h + 1) * HEAD_DIM)
            o_ref[0, :, hs] = jnp.transpose(acc_scr[hs, :] / l_scr[h]).astype(BF16)


def _fox_prompt(qa, ka, vt):
    bsz, d, t_len = vt.shape
    tq = ATTN_TILE
    nb = t_len // tq
    pairs = [(i, j) for i in range(nb) for j in range(i + 1)]
    qi = jnp.asarray([a for a, _ in pairs], I32)
    kj = jnp.asarray([b for _, b in pairs], I32)
    grid_spec = pltpu.PrefetchScalarGridSpec(
        num_scalar_prefetch=2,
        grid=(bsz, len(pairs)),
        in_specs=[
            pl.BlockSpec((1, N_HEADS * AUG, tq), lambda b, p, qi, kj: (b, 0, qi[p])),
            pl.BlockSpec((1, tq, N_HEADS * AUG), lambda b, p, qi, kj: (b, kj[p], 0)),
            pl.BlockSpec((1, d, tq), lambda b, p, qi, kj: (b, 0, kj[p])),
        ],
        out_specs=pl.BlockSpec((1, tq, d), lambda b, p, qi, kj: (b, qi[p], 0)),
        scratch_shapes=[pltpu.VMEM((N_HEADS, 1, tq), F32), pltpu.VMEM((N_HEADS, 1, tq), F32),
                        pltpu.VMEM((d, tq), F32)],
    )
    return pl.pallas_call(
        functools.partial(_fox_prompt_kernel, tq=tq),
        grid_spec=grid_spec,
        out_shape=jax.ShapeDtypeStruct((bsz, t_len, d), BF16),
        compiler_params=_params("arbitrary", "arbitrary"),
        name="fox_prompt_attention",
    )(qi, kj, qa, ka, vt)


def _page_suffix(lf):
    lane = lax.broadcasted_iota(I32, lf.shape, 1)
    sub = lax.broadcasted_iota(I32, lf.shape, 0)
    x = lf
    sh = N_HEADS
    while sh < LANES:
        x = x + jnp.where(lane + sh < LANES, pltpu.roll(x, LANES - sh, axis=1), 0.0)
        sh *= 2
    y = jnp.where(lane < N_HEADS, x, 0.0)
    sh = N_HEADS
    while sh < LANES:
        y = y + pltpu.roll(y, sh, axis=1)
        sh *= 2
    z = y
    sh = 1
    while sh < SUBLANES:
        z = z + jnp.where(sub + sh < SUBLANES, pltpu.roll(z, SUBLANES - sh, axis=0), 0.0)
        sh *= 2
    return x + (z - y) - lf, z[0:1]


def _fox_decode_kernel(pt_ref, q_ref, kn_ref, vn_ref, cq_ref, ck_ref, *refs, pages):
    del pt_ref
    k_refs = refs[:pages]
    v_refs = refs[pages:2 * pages]
    lf_refs = refs[2 * pages:3 * pages]
    o_ref = refs[3 * pages]
    qm_scr, m_scr, l_scr, acc_scr, r_scr = refs[3 * pages + 1:]
    step = pl.program_id(1)
    n_tok = q_ref.shape[1]
    rows = n_tok * N_HEADS
    page_size = k_refs[0].shape[1]
    lane = lax.broadcasted_iota(I32, (rows, LANES), 1)
    row = lax.broadcasted_iota(I32, (rows, LANES), 0)
    row_head = row // n_tok
    row_tok = row - row_head * n_tok
    cq = cq_ref[0]

    def by_head(x):
        return jnp.concatenate(
            [x[:, h * HEAD_DIM:(h + 1) * HEAD_DIM] for h in range(N_HEADS)], axis=0)

    def attend(u, v_bf):
        m_old = m_scr[...]
        m_new = jnp.maximum(m_old, jnp.max(u, axis=1, keepdims=True) + cq)
        alpha = jnp.exp(m_old - m_new)
        pr = jnp.exp(u - (m_new - cq))
        l_scr[...] = alpha * l_scr[...] + jnp.sum(pr, axis=1, keepdims=True)
        acc_scr[...] = alpha * acc_scr[...] + jnp.dot(pr.astype(BF16), v_bf,
                                                      preferred_element_type=F32)
        m_scr[...] = m_new

    @pl.when(step == 0)
    def _():
        qm = by_head(q_ref[0].astype(F32)).astype(BF16)
        qm_scr[...] = qm
        m_scr[...] = jnp.full_like(m_scr, -jnp.inf)
        l_scr[...] = jnp.zeros_like(l_scr)
        acc_scr[...] = jnp.zeros_like(acc_scr)
        r_scr[...] = jnp.zeros_like(r_scr)
        pad = jnp.zeros((LANES - rows, HEAD_DIM), F32)
        kn = jnp.concatenate([by_head(kn_ref[0].astype(F32)), pad], axis=0).astype(BF16)
        vn = jnp.concatenate([by_head(vn_ref[0].astype(F32)), pad], axis=0).astype(BF16)
        s = lax.dot_general(qm, kn, NT_DIMS, preferred_element_type=F32)
        key_head = lane // n_tok
        key_tok = lane - key_head * n_tok
        u = jnp.where(key_head == row_head, jnp.where(key_tok <= row_tok, s - ck_ref[0], -jnp.inf),
                      -jnp.inf)
        attend(u, vn)

    own_head = (lane & (N_HEADS - 1)) == row_head
    qm = qm_scr[...]
    after_sum = r_scr[...]
    u_parts, v_parts = [], []
    for pg in range(pages):
        later, total = _page_suffix(lf_refs[pg][0])
        after = after_sum + later
        after_sum = after_sum + total
        k2 = k_refs[pg][0].reshape(page_size * N_HEADS, HEAD_DIM).astype(BF16)
        v_parts.append(v_refs[pg][0].reshape(page_size * N_HEADS, HEAD_DIM).astype(BF16))
        s = lax.dot_general(qm, k2, NT_DIMS, preferred_element_type=F32)
        u_parts += [jnp.where(own_head, s[:, c * LANES:(c + 1) * LANES] + after[c:c + 1], -jnp.inf)
                    for c in range(page_size * N_HEADS // LANES)]
    r_scr[...] = after_sum
    attend(jnp.concatenate(u_parts, axis=1), jnp.concatenate(v_parts, axis=0))

    @pl.when(step == pl.num_programs(1) - 1)
    def _():
        o = acc_scr[...] / l_scr[...]
        o_ref[0] = jnp.concatenate(
            [o[h * n_tok:(h + 1) * n_tok] for h in range(N_HEADS)], axis=1).astype(BF16)


def _fox_decode(q, k_new, v_new, cq, ck, cache_k, cache_v, cache_lf, page_table):
    bsz, n_tok, d = q.shape
    n_pages = page_table.shape[1]
    page_size = cache_k.shape[1]
    pages = PAGES_PER_STEP
    rows = n_tok * N_HEADS

    def page_map(pg, ndim):
        def index_map(b, s, pt):
            return (pt[b, n_pages - 1 - (s * pages + pg)],) + (0,) * (ndim - 1)
        return index_map

    seq = lambda b, s, pt: (b, 0, 0)
    in_specs = [pl.BlockSpec((1, n_tok, d), seq), pl.BlockSpec((1, n_tok, d), seq),
                pl.BlockSpec((1, n_tok, d), seq), pl.BlockSpec((1, rows, 1), seq),
                pl.BlockSpec((1, 1, LANES), seq)]
    kv_block = (1, page_size, N_HEADS, HEAD_DIM)
    in_specs += [pl.BlockSpec(kv_block, page_map(pg, 4)) for pg in range(pages)]
    in_specs += [pl.BlockSpec(kv_block, page_map(pg, 4)) for pg in range(pages)]
    in_specs += [pl.BlockSpec((1, SUBLANES, LANES), page_map(pg, 3)) for pg in range(pages)]
    grid_spec = pltpu.PrefetchScalarGridSpec(
        num_scalar_prefetch=1,
        grid=(bsz, n_pages // pages),
        in_specs=in_specs,
        out_specs=pl.BlockSpec((1, n_tok, d), seq),
        scratch_shapes=[pltpu.VMEM((rows, HEAD_DIM), BF16), pltpu.VMEM((rows, 1), F32),
                        pltpu.VMEM((rows, 1), F32), pltpu.VMEM((rows, HEAD_DIM), F32),
                        pltpu.VMEM((1, LANES), F32)],
    )
    return pl.pallas_call(
        functools.partial(_fox_decode_kernel, pages=pages),
        grid_spec=grid_spec,
        out_shape=jax.ShapeDtypeStruct((bsz, n_tok, d), BF16),
        compiler_params=_params("arbitrary", "arbitrary"),
        name="fox_decode_attention",
    )(page_table, q, k_new, v_new, cq, ck, *([cache_k] * pages), *([cache_v] * pages),
      *([cache_lf] * pages))


def _oproj_kernel(x_ref, o_ref, g1_ref, wo_ref, xo_ref):
    xo_ref[0] = x_ref[0] + g1_ref[0] * jnp.dot(o_ref[0], wo_ref[...], preferred_element_type=F32)


def _oproj(x, o, g1, w_o, *, tm):
    bsz, t_len, d = x.shape
    tok = pl.BlockSpec((1, tm, d), lambda b, t: (b, t, 0))
    return pl.pallas_call(
        _oproj_kernel,
        grid=(bsz, t_len // tm),
        in_specs=[tok, tok, _mod_spec(g1, tm), _full_spec(w_o)],
        out_specs=tok,
        out_shape=jax.ShapeDtypeStruct((bsz, t_len, d), F32),
        compiler_params=_params("arbitrary", "arbitrary"),
        name="attn_out_proj",
    )(x, o, g1, w_o)


def _final_kernel(x_ref, y_ref, g2_ref, g_ref, o_ref):
    x = x_ref[0] + g2_ref[0] * y_ref[...]
    o_ref[0] = (x * lax.rsqrt(jnp.mean(x * x, axis=-1, keepdims=True) + EPS)) * g_ref[...]


def _final(x, y, g2, g_final, *, tm):
    bsz, t_len, d = x.shape
    tok = pl.BlockSpec((1, tm, d), lambda b, t: (b, t, 0))
    return pl.pallas_call(
        _final_kernel,
        grid=(bsz, t_len // tm),
        in_specs=[tok, _row_spec(t_len // tm, tm, d), _mod_spec(g2, tm), _full_spec(g_final)],
        out_specs=tok,
        out_shape=jax.ShapeDtypeStruct((bsz, t_len, d), F32),
        compiler_params=_params("arbitrary", "arbitrary"),
        name="final_norm",
    )(x, y, g2, g_final)


def _trunk(x, mods0, mods1, kvmods, prev, h0, w, attend, *, tm, tmx, seg, carry):
    sh1, sc1, g1, sh2, sc2, g2 = mods0
    x, h_out, conv_out = _lru_layer(x, sh1, sc1, g1, w["lru"], prev, h0, tm=tm, seg=seg, carry=carry)
    y = _moe(x, sh2, sc2, w["moe0"], tm=tm, tmx=tmx)
    sh1b, sc1b, g1b, sh2b, sc2b, g2b = mods1
    shk, sck = kvmods
    x, k, v, lft, fct, *operands = _post0(x, y, g2, shk, sck, sh1b, sc1b, w["post0"],
                                          tm=tm, seg=seg, carry=carry, prompt_layout=carry)
    o = attend(fct, *operands)
    x = _oproj(x, o, g1b, w["w_o"], tm=tm)
    y = _moe(x, sh2b, sc2b, w["moe1"], tm=tm, tmx=tmx)
    y_out = _final(x, y, g2b, w["g_final"], tm=tm)
    return y_out, h_out, conv_out, k, v, lft


def kernel(x_prompt, x_sample, state_h, state_conv, cache_k, cache_v, cache_logf, page_table,
           c_prompt, c_sample, w_mod, b_mod, g_mix, g_moe, lru_w_in, lru_conv_w, lru_conv_b,
           lru_w_a, lru_b_a, lru_w_x, lru_b_x, lru_lambda, lru_w_out, kv_g, kv_w_mod, kv_b_mod,
           kv_w, kv_b_f, attn_w_q, attn_w_o, moe_w_grp, moe_b_grp, moe_w_exp, moe_b_exp,
           moe_w_up, moe_w_down, g_final):
    d = D_MODEL
    bp, tp, _ = x_prompt.shape
    bs, ts, _ = x_sample.shape
    hd = N_HEADS * HEAD_DIM
    row = lambda a: a.reshape(1, -1)

    def moe_weights(layer):
        wr_t = jnp.concatenate(
            [moe_w_grp[layer].T, moe_w_exp[layer].T,
             jnp.zeros((ROUTER_ROWS - N_GROUPS - N_EXPERTS, d), F32)], axis=0)
        br = jnp.concatenate(
            [moe_b_grp[layer], moe_b_exp[layer],
             jnp.zeros((ROUTER_ROWS - N_GROUPS - N_EXPERTS,), F32)]).reshape(ROUTER_ROWS, 1)
        return dict(g_moe=row(g_moe[layer]), wr_t=wr_t, br=br,
                    w_up=moe_w_up[layer].astype(BF16), w_down=moe_w_down[layer].astype(BF16))

    w = dict(
        lru=dict(g_mix=row(g_mix[0]), w_in=lru_w_in[0].astype(BF16), conv_w=lru_conv_w[0],
                 conv_b=row(lru_conv_b[0]), w_a=lru_w_a[0].astype(BF16), b_a=row(lru_b_a[0]),
                 w_x=lru_w_x[0].astype(BF16), b_x=row(lru_b_x[0]), lam=row(lru_lambda[0]),
                 w_out=lru_w_out[0].astype(BF16)),
        moe0=moe_weights(0),
        moe1=moe_weights(1),
        post0=dict(kv_g=row(kv_g), w_k=kv_w[:, :hd].astype(BF16),
                   w_v=kv_w[:, hd:2 * hd].astype(BF16), w_f_t=kv_w[:, 2 * hd:].T,
                   w_v_t=kv_w[:, hd:2 * hd].T.astype(BF16), b_f=kv_b_f.reshape(N_HEADS, 1),
                   g_mix1=row(g_mix[1]), w_q=attn_w_q[0].astype(BF16),
                   w_q_t=attn_w_q[0].T.astype(BF16)),
        w_o=attn_w_o[0].astype(BF16),
        g_final=row(g_final),
    )

    n_c = bp + bs
    c_rows = -(-n_c // SUBLANES) * SUBLANES
    c_all = jnp.concatenate([c_prompt, c_sample, jnp.zeros((c_rows - n_c, d), F32)], axis=0)
    mod_l0 = _mm_bias(c_all, w_mod[0], b_mod[0])
    mod_l1 = _mm_bias(c_all, w_mod[1], b_mod[1])
    mod_kv = _mm_bias(c_all, kv_w_mod, kv_b_mod)

    def prompt_mods(m, parts):
        return [a.reshape(bp, 1, d) for a in jnp.split(m[:bp], parts, axis=-1)]

    def sample_mods(m, parts):
        return [jnp.repeat(a, ts, axis=0).reshape(1, bs * ts, d)
                for a in jnp.split(m[bp:n_c], parts, axis=-1)]

    prev_p = jnp.zeros((bp, SUBLANES, D_RNN), F32)
    h0_p = jnp.zeros((bp, 1, D_RNN), F32)
    attend_p = lambda fct, qa, ka, vt: _fox_prompt(qa, ka, vt)
    y_p, h_p, conv_p, k_p, v_p, lft_p = _trunk(
        x_prompt, prompt_mods(mod_l0, 6), prompt_mods(mod_l1, 6), prompt_mods(mod_kv, 2),
        prev_p, h0_p, w, attend_p, tm=TOKEN_TILE, tmx=EXPERT_TILE, seg=TOKEN_TILE, carry=True)

    n_s = bs * ts
    prev_s = jnp.pad(state_conv[0], ((0, 0), (ts - (CONV_WIDTH - 1), 0), (0, 0))).reshape(1, n_s, D_RNN)
    h0_s = jnp.repeat(state_h[0], ts, axis=0).reshape(1, n_s, D_RNN)
    n_pool, page_size = cache_k.shape[0], cache_k.shape[1]
    cache_lf = cache_logf.reshape(n_pool, SUBLANES, page_size * N_HEADS // SUBLANES)

    def attend_s(fct, q, kb, vb):
        c_new = jnp.transpose(fct.reshape(N_HEADS, bs, ts), (1, 0, 2)).reshape(bs, N_HEADS * ts)
        cq = c_new.reshape(bs, N_HEADS * ts, 1)
        ck = jnp.pad(c_new, ((0, 0), (0, LANES - N_HEADS * ts))).reshape(bs, 1, LANES)
        o = _fox_decode(q.reshape(bs, ts, d), kb.reshape(bs, ts, d), vb.reshape(bs, ts, d), cq, ck,
                        cache_k, cache_v, cache_lf, page_table)
        return o.reshape(1, n_s, d)

    y_s, h_s, conv_s, k_s, v_s, lft_s = _trunk(
        x_sample.reshape(1, n_s, d), sample_mods(mod_l0, 6), sample_mods(mod_l1, 6),
        sample_mods(mod_kv, 2), prev_s, h0_s, w, attend_s, tm=n_s, tmx=SAMPLE_EXPERT_TILE, seg=ts,
        carry=False)

    n_conv = CONV_WIDTH - 1
    return (
        y_p,
        y_s.reshape(bs, ts, d),
        h_p.reshape(1, bp, D_RNN),
        conv_p[:, SUBLANES - n_conv:].reshape(1, bp, n_conv, D_RNN),
        k_p.reshape(bp, tp, N_HEADS, HEAD_DIM),
        v_p.reshape(bp, tp, N_HEADS, HEAD_DIM),
        jnp.transpose(lft_p, (0, 2, 1)),
        h_s.reshape(bs, ts, D_RNN)[:, ts - 1].reshape(1, bs, D_RNN),
        conv_s.reshape(bs, ts, D_RNN)[:, ts - n_conv:].reshape(1, bs, n_conv, D_RNN),
        k_s.reshape(bs, ts, N_HEADS, HEAD_DIM),
        v_s.reshape(bs, ts, N_HEADS, HEAD_DIM),
        jnp.transpose(lft_s.reshape(N_HEADS, bs, ts), (1, 2, 0)),
    )
```

```python
import functools
import math

import jax
import jax.numpy as jnp
from jax import lax
from jax.experimental import pallas as pl
from jax.experimental.pallas import tpu as pltpu

F32 = jnp.float32
BF16 = jnp.bfloat16
I32 = jnp.int32

D_MODEL = 1024
D_RNN = D_MODEL
N_LRU_BLOCKS = 8
LRU_BLOCK = D_RNN // N_LRU_BLOCKS
CONV_WIDTH = 4
LRU_C = 8.0
N_HEADS = 8
HEAD_DIM = D_MODEL // N_HEADS
N_GROUPS = 4
EXPERTS_PER_GROUP = 4
N_EXPERTS = N_GROUPS * EXPERTS_PER_GROUP
D_EXPERT = D_MODEL // 2
EPS = 1e-6
N_PAIRS = 6
N_CLASSES = N_GROUPS * N_PAIRS
PAIR_LO = (0, 0, 0, 1, 1, 2)
PAIR_HI = (1, 2, 3, 2, 3, 3)

SUBLANES = 8
LANES = 128
VMEM_LIMIT = 48 * 1024 * 1024
DMA_THREADS = 2

TOKEN_TILE = 256
EXPERT_TILE = 256
SAMPLE_EXPERT_TILE = 32
ROUTER_ROWS = 32
ATTN_TILE = 512
PAGES_PER_STEP = 8
MOD_TILE = 512

HIGHEST = lax.Precision.HIGHEST
NT_DIMS = (((1,), (1,)), ((), ()))


def _params(*semantics):
    return pltpu.CompilerParams(dimension_semantics=semantics, vmem_limit_bytes=VMEM_LIMIT)


def _norm_mod(x, g, shift, scale):
    y = x * lax.rsqrt(jnp.mean(x * x, axis=-1, keepdims=True) + EPS)
    return (y * g) * (1.0 + scale) + shift


def _log_sigmoid(x):
    return jnp.minimum(x, 0.0) - jnp.log1p(jnp.exp(-jnp.abs(x)))


def _sigmoid(x):
    return 1.0 / (1.0 + jnp.exp(-x))


def _gelu_tanh(x):
    c = math.sqrt(2.0 / math.pi)
    return x * (0.5 * (1.0 + jnp.tanh(c * (x + 0.044715 * (x * x * x)))))


def _mod_spec(arr, tm):
    if arr.shape[1] == 1:
        return pl.BlockSpec((1, 1, arr.shape[2]), lambda b, t: (b, 0, 0))
    return pl.BlockSpec((1, tm, arr.shape[2]), lambda b, t: (b, t, 0))


def _row_spec(nt, tm, d):
    return pl.BlockSpec((tm, d), lambda b, t: (b * nt + t, 0))


def _full_spec(arr):
    zeros = (0,) * arr.ndim
    return pl.BlockSpec(arr.shape, lambda *_: zeros)


def _mm_bias_kernel(x_ref, w_ref, b_ref, o_ref):
    o_ref[...] = jnp.dot(x_ref[...].astype(BF16), w_ref[...].astype(BF16),
                         preferred_element_type=F32) + b_ref[...]


def _mm_bias(x, w, b):
    m, d = x.shape
    n = w.shape[1]
    return pl.pallas_call(
        _mm_bias_kernel,
        grid=(n // MOD_TILE,),
        in_specs=[pl.BlockSpec((m, d), lambda j: (0, 0)),
                  pl.BlockSpec((d, MOD_TILE), lambda j: (0, j)),
                  pl.BlockSpec((1, MOD_TILE), lambda j: (0, j))],
        out_specs=pl.BlockSpec((m, MOD_TILE), lambda j: (0, j)),
        out_shape=jax.ShapeDtypeStruct((m, n), F32),
        compiler_params=_params("arbitrary"),
        name="mod_vectors",
    )(x, w, b.reshape(1, n))


def _lru_kernel(x_ref, sh_ref, sc_ref, gt_ref, gmix_ref, win_ref, cw_ref, cb_ref, wa_ref, ba_ref,
                wx_ref, bx_ref, lam_ref, wout_ref, prev_ref, h0_ref,
                xo_ref, hs_ref, xb_ref, *scratch, tm, seg, carry):
    x = x_ref[0]
    hn = _norm_mod(x, gmix_ref[...], sh_ref[0], sc_ref[0])
    xg = jnp.dot(hn.astype(BF16), win_ref[...], preferred_element_type=F32)
    xb = xg[:, :D_RNN]
    gb = xg[:, D_RNN:]

    if carry:
        prev_scr, h_scr = scratch

        @pl.when(pl.program_id(1) == 0)
        def _():
            prev_scr[...] = prev_ref[0]
            h_scr[...] = h0_ref[0]

        prev = prev_scr[...]
        hprev = h_scr[...]
    else:
        prev = prev_ref[0]
        hprev = h0_ref[0]

    row = lax.broadcasted_iota(I32, (tm, 1), 0)
    rseg = row & (seg - 1)
    nprev = prev.shape[0]

    cw = cw_ref[...]
    xc = cb_ref[...] + cw[CONV_WIDTH - 1:CONV_WIDTH] * xb
    for d in range(1, CONV_WIDTH):
        rolled = pltpu.roll(xb, d, axis=0)
        rp = pltpu.roll(prev, nprev - SUBLANES + d, axis=0)
        if carry:
            head = jnp.where(row[:SUBLANES] < d, rp, rolled[:SUBLANES])
            shifted = head if tm == SUBLANES else jnp.concatenate([head, rolled[SUBLANES:]], axis=0)
        else:
            shifted = jnp.where(rseg < d, rp, rolled)
        xc = xc + cw[CONV_WIDTH - 1 - d:CONV_WIDTH - d] * shifted

    xcb = xc.astype(BF16)
    ra = jnp.concatenate(
        [jnp.dot(xcb[:, n * LRU_BLOCK:(n + 1) * LRU_BLOCK], wa_ref[n], preferred_element_type=F32)
         for n in range(N_LRU_BLOCKS)], axis=1)
    rx = jnp.concatenate(
        [jnp.dot(xcb[:, n * LRU_BLOCK:(n + 1) * LRU_BLOCK], wx_ref[n], preferred_element_type=F32)
         for n in range(N_LRU_BLOCKS)], axis=1)
    r = _sigmoid(ra + ba_ref[...])
    ig = _sigmoid(rx + bx_ref[...])
    log_a = (LRU_C * r) * _log_sigmoid(lam_ref[...])
    a = jnp.exp(log_a)
    bt = jnp.sqrt(-jnp.tanh(log_a) * (a * a + 1.0)) * (ig * xc)

    s = 1
    while s < seg:
        inside = rseg >= s
        a_sh = jnp.where(inside, pltpu.roll(a, s, axis=0), 1.0)
        b_sh = jnp.where(inside, pltpu.roll(bt, s, axis=0), 0.0)
        bt = a * b_sh + bt
        a = a * a_sh
        s *= 2
    hs = bt + a * hprev

    if carry:
        h_scr[...] = hs[tm - 1:tm]
        prev_scr[...] = xb[tm - SUBLANES:]
        hs_ref[0] = hs[tm - 1:tm]
        xb_ref[0] = xb[tm - SUBLANES:]
    else:
        hs_ref[0] = hs
        xb_ref[0] = xb

    y = (hs * _gelu_tanh(gb)).astype(BF16)
    out = jnp.dot(y, wout_ref[...], preferred_element_type=F32)
    xo_ref[0] = x + gt_ref[0] * out


def _lru_layer(x, sh, sc, gt, p, prev, h0, *, tm, seg, carry):
    bsz, t_len, d = x.shape
    r = D_RNN
    nt = t_len // tm
    tok = pl.BlockSpec((1, tm, d), lambda b, t: (b, t, 0))
    if carry:
        state_specs = [pl.BlockSpec((1, SUBLANES, r), lambda b, t: (b, 0, 0)),
                       pl.BlockSpec((1, 1, r), lambda b, t: (b, 0, 0))]
        out_specs = [tok,
                     pl.BlockSpec((1, 1, r), lambda b, t: (b, 0, 0)),
                     pl.BlockSpec((1, SUBLANES, r), lambda b, t: (b, 0, 0))]
        out_shape = [jax.ShapeDtypeStruct((bsz, t_len, d), F32),
                     jax.ShapeDtypeStruct((bsz, 1, r), F32),
                     jax.ShapeDtypeStruct((bsz, SUBLANES, r), F32)]
        scratch = [pltpu.VMEM((SUBLANES, r), F32), pltpu.VMEM((1, r), F32)]
    else:
        rtok = pl.BlockSpec((1, tm, r), lambda b, t: (b, t, 0))
        state_specs = [rtok, rtok]
        out_specs = [tok, rtok, rtok]
        out_shape = [jax.ShapeDtypeStruct((bsz, t_len, d), F32),
                     jax.ShapeDtypeStruct((bsz, t_len, r), F32),
                     jax.ShapeDtypeStruct((bsz, t_len, r), F32)]
        scratch = []
    weights = [p["g_mix"], p["w_in"], p["conv_w"], p["conv_b"], p["w_a"], p["b_a"], p["w_x"],
               p["b_x"], p["lam"], p["w_out"]]
    return pl.pallas_call(
        functools.partial(_lru_kernel, tm=tm, seg=seg, carry=carry),
        grid=(bsz, nt),
        in_specs=[tok, _mod_spec(sh, tm), _mod_spec(sc, tm), _mod_spec(gt, tm)]
                 + [_full_spec(w) for w in weights] + state_specs,
        out_specs=out_specs,
        out_shape=out_shape,
        scratch_shapes=scratch,
        compiler_params=_params("arbitrary", "arbitrary"),
        name="rglru_layer",
    )(x, sh, sc, gt, *weights, prev, h0)


def _router_kernel(x_ref, sh_ref, sc_ref, g_ref, wr_ref, br_ref,
                   hn_ref, cls_ref, rank_ref, glo_ref, ghi_ref, cnt_ref, carry_scr, *, tm):
    @pl.when((pl.program_id(0) == 0) & (pl.program_id(1) == 0))
    def _():
        carry_scr[...] = jnp.zeros_like(carry_scr)

    hn = _norm_mod(x_ref[0], g_ref[...], sh_ref[0], sc_ref[0])
    hn_ref[...] = hn
    lt = lax.dot_general(wr_ref[...], hn, NT_DIMS, precision=HIGHEST,
                         preferred_element_type=F32) + br_ref[...]

    gl = [lt[k:k + 1] for k in range(N_GROUPS)]
    gmax = jnp.maximum(jnp.maximum(gl[0], gl[1]), jnp.maximum(gl[2], gl[3]))
    g_sel = jnp.where(gl[0] >= gmax, 0, jnp.where(gl[1] >= gmax, 1, jnp.where(gl[2] >= gmax, 2, 3)))
    p_grp = 1.0 / (jnp.exp(gl[0] - gmax) + jnp.exp(gl[1] - gmax)
                   + jnp.exp(gl[2] - gmax) + jnp.exp(gl[3] - gmax))

    def expert_logit(e):
        rows = [lt[N_GROUPS + g * EXPERTS_PER_GROUP + e:N_GROUPS + g * EXPERTS_PER_GROUP + e + 1]
                for g in range(N_GROUPS)]
        return jnp.where(g_sel == 0, rows[0],
                         jnp.where(g_sel == 1, rows[1], jnp.where(g_sel == 2, rows[2], rows[3])))

    es = [expert_logit(e) for e in range(EXPERTS_PER_GROUP)]

    def first_argmax(vals):
        vmax = jnp.maximum(jnp.maximum(vals[0], vals[1]), jnp.maximum(vals[2], vals[3]))
        idx = jnp.where(vals[0] >= vmax, 0,
                        jnp.where(vals[1] >= vmax, 1, jnp.where(vals[2] >= vmax, 2, 3)))
        return vmax, idx

    v1, i1 = first_argmax(es)
    rest = [jnp.where(i1 == e, -jnp.inf, es[e]) for e in range(EXPERTS_PER_GROUP)]
    v2, i2 = first_argmax(rest)
    e21 = jnp.exp(v2 - v1)
    w1 = (1.0 / (1.0 + e21)) * p_grp
    w2 = (e21 / (1.0 + e21)) * p_grp
    first_lower = i1 < i2
    lo = jnp.where(first_lower, i1, i2)
    hi = jnp.where(first_lower, i2, i1)
    glo_ref[0] = jnp.where(first_lower, w1, w2)
    ghi_ref[0] = jnp.where(first_lower, w2, w1)
    pair = jnp.where(lo == 0, hi - 1, jnp.where(lo == 1, hi + 1, 5))
    cls = g_sel * N_PAIRS + pair
    cls_ref[0] = cls

    crow = lax.broadcasted_iota(I32, (ROUTER_ROWS, tm), 0)
    onehot = jnp.where(crow == cls, 1.0, 0.0)
    ki = lax.broadcasted_iota(I32, (tm, tm), 0)
    kj = lax.broadcasted_iota(I32, (tm, tm), 1)
    upper = jnp.where(ki <= kj, 1.0, 0.0).astype(BF16)
    cum = jnp.dot(onehot.astype(BF16), upper, preferred_element_type=F32)
    carry = carry_scr[...]
    rank = jnp.sum(onehot * (cum - 1.0 + carry), axis=0, keepdims=True)
    rank_ref[0] = rank.astype(I32)
    carry = carry + cum[:, tm - 1:tm]
    carry_scr[...] = carry
    cnt_ref[...] = carry


def _router(x, sh, sc, g, wr_t, br, *, tm):
    bsz, t_len, d = x.shape
    nt = t_len // tm
    n_tiles = bsz * nt
    tok = pl.BlockSpec((1, tm, d), lambda b, t: (b, t, 0))
    lane_row = pl.BlockSpec((1, 1, tm), lambda b, t: (b * nt + t, 0, 0))
    row_shape = jax.ShapeDtypeStruct((n_tiles, 1, tm), F32)
    row_shape_i = jax.ShapeDtypeStruct((n_tiles, 1, tm), I32)
    return pl.pallas_call(
        functools.partial(_router_kernel, tm=tm),
        grid=(bsz, nt),
        in_specs=[tok, _mod_spec(sh, tm), _mod_spec(sc, tm), _full_spec(g), _full_spec(wr_t),
                  _full_spec(br)],
        out_specs=[pl.BlockSpec((tm, d), lambda b, t: (b * nt + t, 0)),
                   lane_row, lane_row, lane_row, lane_row,
                   pl.BlockSpec((ROUTER_ROWS, 1), lambda b, t: (0, 0))],
        out_shape=[jax.ShapeDtypeStruct((bsz * t_len, d), F32), row_shape_i, row_shape_i,
                   row_shape, row_shape, jax.ShapeDtypeStruct((ROUTER_ROWS, 1), F32)],
        scratch_shapes=[pltpu.VMEM((ROUTER_ROWS, 1), F32)],
        compiler_params=_params("arbitrary", "arbitrary"),
        name="moe_router",
    )(x, sh, sc, g, wr_t, br)


def _expert_kernel(e1_ref, e2_ref, valid_ref, src0_ref, srcn_ref, dst_ref, glo_ref, ghi_ref,
                   hn_hbm, wu1_ref, wd1_ref, wu2_ref, wd2_ref, y_hbm,
                   xbuf, obuf, sem_g, sem_s, *, tmx):
    del e1_ref, e2_ref
    j = pl.program_id(0)
    last = pl.num_programs(0) - 1
    slot = lax.rem(j, 2)

    def start_gather(idx_ref, to_slot):
        for r in range(tmx):
            pltpu.make_async_copy(hn_hbm.at[pl.ds(idx_ref[0, 0, r], 1)],
                                  xbuf.at[to_slot, pl.ds(r, 1)],
                                  sem_g.at[to_slot]).start(priority=r % DMA_THREADS)

    def wait_gather(of_slot):
        pltpu.make_async_copy(hn_hbm.at[pl.ds(0, tmx)], xbuf.at[of_slot], sem_g.at[of_slot]).wait()

    def wait_scatter():
        pltpu.make_async_copy(obuf.at[0], y_hbm.at[pl.ds(0, tmx)], sem_s.at[0]).wait()

    @pl.when(j == 0)
    def _():
        start_gather(src0_ref, 0)
        obuf[1] = jnp.zeros((tmx, obuf.shape[2]), F32)
        spare = pltpu.make_async_copy(obuf.at[1], y_hbm.at[pl.ds(y_hbm.shape[0] - tmx, tmx)],
                                      sem_s.at[0])
        spare.start()
        spare.wait()

    @pl.when(valid_ref[j] == 1)
    def _():
        wait_gather(slot)
        start_gather(srcn_ref, 1 - slot)
        x = xbuf[slot].astype(BF16)
        ri = lax.broadcasted_iota(I32, (tmx, tmx), 0)
        ci = lax.broadcasted_iota(I32, (tmx, tmx), 1)

        def column(row_ref):
            return jnp.sum(jnp.where(ri == ci, row_ref[0], 0.0), axis=1, keepdims=True)

        def ffn(wu_ref, wd_ref, gate):
            gu = jnp.dot(x, wu_ref[0], preferred_element_type=F32)
            g_ = gu[:, :D_EXPERT]
            u_ = gu[:, D_EXPERT:]
            act = ((g_ * _sigmoid(g_)) * u_).astype(BF16)
            return gate * jnp.dot(act, wd_ref[0], preferred_element_type=F32)

        obuf[slot] = (ffn(wu1_ref, wd1_ref, column(glo_ref)) + ffn(wu2_ref, wd2_ref, column(ghi_ref)))

        @pl.when(j > 0)
        def _():
            wait_scatter()

        for r in range(tmx):
            pltpu.make_async_copy(obuf.at[slot, pl.ds(r, 1)],
                                  y_hbm.at[pl.ds(dst_ref[0, 0, r], 1)],
                                  sem_s.at[0]).start(priority=r % DMA_THREADS)

        @pl.when((j == last) | (valid_ref[jnp.minimum(j + 1, last)] == 0))
        def _():
            wait_scatter()
            wait_gather(1 - slot)


def _experts(hn, src, dst, glo_s, ghi_s, w_up, w_down, e1, e2, valid, n_tiles, tmx):
    n, d = hn.shape
    idx_block = (1, 1, tmx)
    at_tile = lambda j, e1, e2, v: (j, 0, 0)
    grid_spec = pltpu.PrefetchScalarGridSpec(
        num_scalar_prefetch=3,
        grid=(n_tiles,),
        in_specs=[
            pl.BlockSpec(idx_block, lambda j, e1, e2, v: (0, 0, 0), memory_space=pltpu.SMEM),
            pl.BlockSpec(idx_block, lambda j, e1, e2, v: (jnp.minimum(j + 1, n_tiles - 1), 0, 0),
                         memory_space=pltpu.SMEM),
            pl.BlockSpec(idx_block, at_tile, memory_space=pltpu.SMEM),
            pl.BlockSpec(idx_block, at_tile),
            pl.BlockSpec(idx_block, at_tile),
            pl.BlockSpec(memory_space=pl.ANY),
            pl.BlockSpec((1, d, 2 * D_EXPERT), lambda j, e1, e2, v: (e1[j], 0, 0)),
            pl.BlockSpec((1, D_EXPERT, d), lambda j, e1, e2, v: (e1[j], 0, 0)),
            pl.BlockSpec((1, d, 2 * D_EXPERT), lambda j, e1, e2, v: (e2[j], 0, 0)),
            pl.BlockSpec((1, D_EXPERT, d), lambda j, e1, e2, v: (e2[j], 0, 0)),
        ],
        out_specs=pl.BlockSpec(memory_space=pl.ANY),
        scratch_shapes=[pltpu.VMEM((2, tmx, d), F32), pltpu.VMEM((2, tmx, d), F32),
                        pltpu.SemaphoreType.DMA((2,)), pltpu.SemaphoreType.DMA((1,))],
    )
    return pl.pallas_call(
        functools.partial(_expert_kernel, tmx=tmx),
        grid_spec=grid_spec,
        out_shape=jax.ShapeDtypeStruct((n + tmx, d), F32),
        compiler_params=_params("arbitrary"),
        name="moe_experts",
    )(e1, e2, valid, src, src, dst, glo_s, ghi_s, hn, w_up, w_down, w_up, w_down)


def _moe(x, sh, sc, p, *, tm, tmx):
    bsz, t_len, d = x.shape
    n = bsz * t_len
    hn, cls, rank, glo, ghi, cnt = _router(x, sh, sc, p["g_moe"], p["wr_t"], p["br"], tm=tm)
    cls = cls.reshape(n)
    rank = rank.reshape(n)
    counts = cnt[:N_CLASSES, 0].astype(I32)
    padded = ((counts + tmx - 1) // tmx) * tmx
    ends = jnp.cumsum(padded)
    offs = ends - padded
    pos = offs[cls] + rank
    n_tiles = n // tmx + N_CLASSES
    n_rows = n_tiles * tmx
    total = ends[N_CLASSES - 1]
    tile_start = jnp.arange(n_tiles, dtype=I32) * tmx
    valid = (tile_start < total).astype(I32)
    last_tile = jnp.maximum(total // tmx - 1, 0)
    tile_cls = jnp.minimum(
        jnp.sum((ends[None, :] <= jnp.minimum(tile_start, last_tile * tmx)[:, None]).astype(I32),
                axis=1), N_CLASSES - 1)
    grp = tile_cls // N_PAIRS
    pair = tile_cls % N_PAIRS
    e1 = grp * EXPERTS_PER_GROUP + jnp.asarray(PAIR_LO, I32)[pair]
    e2 = grp * EXPERTS_PER_GROUP + jnp.asarray(PAIR_HI, I32)[pair]
    token_info = jnp.stack([jnp.arange(n, dtype=I32).astype(F32), glo.reshape(n), ghi.reshape(n),
                            jnp.ones((n,), F32)], axis=1)
    row_info = jnp.zeros((n_rows, 4), F32).at[pos].set(token_info)
    src = row_info[:, 0].astype(I32)
    dst = jnp.where(row_info[:, 3] == 1.0, src, n + jnp.arange(n_rows, dtype=I32) % tmx)
    as_tiles = lambda a: a.reshape(n_tiles, 1, tmx)
    return _experts(hn, as_tiles(src), as_tiles(dst), as_tiles(row_info[:, 1]),
                    as_tiles(row_info[:, 2]), p["w_up"], p["w_down"], e1, e2, valid, n_tiles, tmx)


AUG = 2 * HEAD_DIM
N_PIECES = 3
VT_ROWS = HEAD_DIM + 16
LOG2E = 1.0 / math.log(2.0)


def _bf16_pieces(x):
    pieces = []
    rest = x
    for _ in range(N_PIECES):
        piece = rest.astype(BF16).astype(F32)
        pieces.append(piece)
        rest = rest - piece
    return pieces


def _post0_kernel(x_ref, y_ref, g2_ref, shk_ref, sck_ref, kvg_ref, wk_ref, wv_ref, wft_ref, bf_ref,
                  shq_ref, scq_ref, gq_ref, wq_ref, *refs, tm, seg, carry, prompt_layout):
    if prompt_layout:
        wvt_ref, x1_ref, k_ref, v_ref, lft_ref, fct_ref, qa_ref, ka_ref, vt_ref = refs[:9]
        scratch = refs[9:]
    else:
        x1_ref, k_ref, v_ref, lft_ref, fct_ref, q_ref, kb_ref, vb_ref = refs[:8]
        scratch = refs[8:]
    scale = HEAD_DIM ** -0.5
    x1 = x_ref[0] + g2_ref[0] * y_ref[...]
    x1_ref[0] = x1
    hk = _norm_mod(x1, kvg_ref[...], shk_ref[0], sck_ref[0])
    hkb = hk.astype(BF16)
    k = jnp.dot(hkb, wk_ref[...], preferred_element_type=F32)
    v = jnp.dot(hkb, wv_ref[...], preferred_element_type=F32)
    for h in range(N_HEADS):
        k_ref[0, :, h, :] = k[:, h * HEAD_DIM:(h + 1) * HEAD_DIM]
        v_ref[0, :, h, :] = v[:, h * HEAD_DIM:(h + 1) * HEAD_DIM]
    f = lax.dot_general(wft_ref[...], hk, NT_DIMS, precision=HIGHEST,
                        preferred_element_type=F32) + bf_ref[...]
    lf = _log_sigmoid(f)
    lft_ref[0] = lf
    col = lax.broadcasted_iota(I32, (1, tm), 1)
    cseg = col & (seg - 1)
    c = lf
    s = 1
    while s < seg:
        c = c + jnp.where(cseg >= s, pltpu.roll(c, s, axis=1), 0.0)
        s *= 2
    if carry:
        (f_scr,) = scratch

        @pl.when(pl.program_id(1) == 0)
        def _():
            f_scr[...] = jnp.zeros_like(f_scr)

        c = c + f_scr[...]
        f_scr[...] = c[:, tm - 1:tm]
    fct_ref[0] = c
    hqb = _norm_mod(x1, gq_ref[...], shq_ref[0], scq_ref[0]).astype(BF16)

    if not prompt_layout:
        q_ref[0] = (jnp.dot(hqb, wq_ref[...], preferred_element_type=F32) * scale).astype(BF16)
        kb_ref[0] = k.astype(BF16)
        vb_ref[0] = v.astype(BF16)
        return

    qt = lax.dot_general(wq_ref[...], hqb, NT_DIMS, preferred_element_type=F32) * (scale * LOG2E)
    vt = lax.dot_general(wvt_ref[...], hkb, NT_DIMS, preferred_element_type=F32)
    ones_rows = jnp.where(lax.broadcasted_iota(I32, (VT_ROWS - HEAD_DIM, tm), 0) == 0, 1.0, 0.0)
    v_rows = []
    for h in range(N_HEADS):
        v_rows += [vt[h * HEAD_DIM:(h + 1) * HEAD_DIM], ones_rows]
    vt_ref[0] = jnp.concatenate(v_rows, axis=0).astype(BF16)
    hi, mid, lo = _bf16_pieces(c * LOG2E)
    sub = lax.broadcasted_iota(I32, (SUBLANES, tm), 0)
    pad = jnp.zeros((HEAD_DIM - SUBLANES, tm), F32)
    q_rows, k_rows = [], []
    for h in range(N_HEADS):
        hs = slice(h, h + 1)
        eq = jnp.where(sub < 3, -1.0, jnp.where(sub == 3, hi[hs], jnp.where(
            sub == 4, mid[hs], jnp.where(sub == 5, lo[hs], 0.0))))
        ek = jnp.where(sub == 0, hi[hs], jnp.where(sub == 1, mid[hs], jnp.where(
            sub == 2, lo[hs], jnp.where(sub < 6, 1.0, 0.0))))
        q_rows += [qt[h * HEAD_DIM:(h + 1) * HEAD_DIM], eq, pad]
        k_rows += [ek, pad]
    qa_ref[0] = jnp.concatenate(q_rows, axis=0).astype(BF16)
    ekt = jnp.concatenate(k_rows, axis=0).astype(BF16)
    ri = lax.broadcasted_iota(I32, (tm, tm), 0)
    ci = lax.broadcasted_iota(I32, (tm, tm), 1)
    eye = jnp.where(ri == ci, 1.0, 0.0).astype(BF16)
    extra = lax.dot_general(eye, ekt, NT_DIMS, preferred_element_type=F32)
    k_cols = []
    for h in range(N_HEADS):
        hs = slice(h * HEAD_DIM, (h + 1) * HEAD_DIM)
        k_cols += [k[:, hs], extra[:, hs]]
    ka_ref[0] = jnp.concatenate(k_cols, axis=1).astype(BF16)


def _post0(x, y, g2, shk, sck, shq, scq, p, *, tm, seg, carry, prompt_layout):
    bsz, t_len, d = x.shape
    tok = pl.BlockSpec((1, tm, d), lambda b, t: (b, t, 0))
    headrow = pl.BlockSpec((1, N_HEADS, tm), lambda b, t: (b, 0, t))
    weights_kv = [p["kv_g"], p["w_k"], p["w_v"], p["w_f_t"], p["b_f"]]
    heads = pl.BlockSpec((1, tm, N_HEADS, HEAD_DIM), lambda b, t: (b, t, 0, 0))
    tok_f32 = jax.ShapeDtypeStruct((bsz, t_len, d), F32)
    heads_f32 = jax.ShapeDtypeStruct((bsz, t_len, N_HEADS, HEAD_DIM), F32)
    head_f32 = jax.ShapeDtypeStruct((bsz, N_HEADS, t_len), F32)
    if prompt_layout:
        weights_q = [p["g_mix1"], p["w_q_t"], p["w_v_t"]]
        out_specs = [tok, heads, heads, headrow, headrow,
                     pl.BlockSpec((1, N_HEADS * AUG, tm), lambda b, t: (b, 0, t)),
                     pl.BlockSpec((1, tm, N_HEADS * AUG), lambda b, t: (b, t, 0)),
                     pl.BlockSpec((1, N_HEADS * VT_ROWS, tm), lambda b, t: (b, 0, t))]
        out_shape = [tok_f32, heads_f32, heads_f32] + [head_f32] * 2 + [
            jax.ShapeDtypeStruct((bsz, N_HEADS * AUG, t_len), BF16),
            jax.ShapeDtypeStruct((bsz, t_len, N_HEADS * AUG), BF16),
            jax.ShapeDtypeStruct((bsz, N_HEADS * VT_ROWS, t_len), BF16)]
    else:
        weights_q = [p["g_mix1"], p["w_q"]]
        out_specs = [tok, heads, heads, headrow, headrow, tok, tok, tok]
        out_shape = [tok_f32, heads_f32, heads_f32] + [head_f32] * 2 + [
            jax.ShapeDtypeStruct((bsz, t_len, d), BF16)] * 3
    return pl.pallas_call(
        functools.partial(_post0_kernel, tm=tm, seg=seg, carry=carry, prompt_layout=prompt_layout),
        grid=(bsz, t_len // tm),
        in_specs=[tok, _row_spec(t_len // tm, tm, d), _mod_spec(g2, tm), _mod_spec(shk, tm),
                  _mod_spec(sck, tm)]
                 + [_full_spec(w) for w in weights_kv]
                 + [_mod_spec(shq, tm), _mod_spec(scq, tm)] + [_full_spec(w) for w in weights_q],
        out_specs=out_specs,
        out_shape=out_shape,
        scratch_shapes=[pltpu.VMEM((N_HEADS, 1), F32)] if carry else [],
        compiler_params=_params("arbitrary", "arbitrary"),
        name="shared_kv_and_q",
    )(x, y, g2, shk, sck, *weights_kv, shq, scq, *weights_q)


def _fox_prompt_kernel(qi_ref, kj_ref, qa_ref, ka_ref, vt_ref, o_ref, m_scr, acc_scr, *, tq):
    p = pl.program_id(1)
    i = qi_ref[p]
    j = kj_ref[p]

    @pl.when(j == 0)
    def _():
        m_scr[...] = jnp.full_like(m_scr, -jnp.inf)
        acc_scr[...] = jnp.zeros_like(acc_scr)

    def step(masked):
        if masked:
            key = lax.broadcasted_iota(I32, (tq, tq), 0)
            qry = lax.broadcasted_iota(I32, (tq, tq), 1)
        for h in range(N_HEADS):
            hv = slice(h * VT_ROWS, (h + 1) * VT_ROWS)
            s = jnp.dot(ka_ref[0, :, h * AUG:(h + 1) * AUG], qa_ref[0, h * AUG:(h + 1) * AUG, :],
                        preferred_element_type=F32)
            if masked:
                s = jnp.where(key <= qry, s, -jnp.inf)
            m_old = m_scr[h]
            m_new = jnp.maximum(m_old, jnp.max(s, axis=0, keepdims=True))
            alpha = jnp.exp2(m_old - m_new)
            pr = jnp.exp2(s - m_new).astype(BF16)
            acc_scr[hv, :] = alpha * acc_scr[hv, :] + jnp.dot(vt_ref[0, hv, :], pr,
                                                              preferred_element_type=F32)
            m_scr[h] = m_new

    @pl.when(j < i)
    def _():
        step(False)

    @pl.when(j == i)
    def _():
        step(True)
        for h in range(N_HEADS):
            base = h * VT_ROWS
            out = acc_scr[base:base + HEAD_DIM, :] / acc_scr[base + HEAD_DIM:base + HEAD_DIM + 1, :]
            o_ref[0, :, h * HEAD_DIM:(h + 1) * HEAD_DIM] = jnp.transpose(out).astype(BF16)


def _fox_prompt(qa, ka, vt):
    bsz, _, t_len = vt.shape
    d = N_HEADS * HEAD_DIM
    tq = ATTN_TILE
    nb = t_len // tq
    pairs = [(i, j) for i in range(nb) for j in range(i + 1)]
    qi = jnp.asarray([a for a, _ in pairs], I32)
    kj = jnp.asarray([b for _, b in pairs], I32)
    grid_spec = pltpu.PrefetchScalarGridSpec(
        num_scalar_prefetch=2,
        grid=(bsz, len(pairs)),
        in_specs=[
            pl.BlockSpec((1, N_HEADS * AUG, tq), lambda b, p, qi, kj: (b, 0, qi[p])),
            pl.BlockSpec((1, tq, N_HEADS * AUG), lambda b, p, qi, kj: (b, kj[p], 0)),
            pl.BlockSpec((1, N_HEADS * VT_ROWS, tq), lambda b, p, qi, kj: (b, 0, kj[p])),
        ],
        out_specs=pl.BlockSpec((1, tq, d), lambda b, p, qi, kj: (b, qi[p], 0)),
        scratch_shapes=[pltpu.VMEM((N_HEADS, 1, tq), F32),
                        pltpu.VMEM((N_HEADS * VT_ROWS, tq), F32)],
    )
    return pl.pallas_call(
        functools.partial(_fox_prompt_kernel, tq=tq),
        grid_spec=grid_spec,
        out_shape=jax.ShapeDtypeStruct((bsz, t_len, d), BF16),
        compiler_params=_params("arbitrary", "arbitrary"),
        name="fox_prompt_attention",
    )(qi, kj, qa, ka, vt)


def _page_suffix(lf):
    lane = lax.broadcasted_iota(I32, lf.shape, 1)
    sub = lax.broadcasted_iota(I32, lf.shape, 0)
    x = lf
    sh = N_HEADS
    while sh < LANES:
        x = x + jnp.where(lane + sh < LANES, pltpu.roll(x, LANES - sh, axis=1), 0.0)
        sh *= 2
    y = jnp.where(lane < N_HEADS, x, 0.0)
    sh = N_HEADS
    while sh < LANES:
        y = y + pltpu.roll(y, sh, axis=1)
        sh *= 2
    z = y
    sh = 1
    while sh < SUBLANES:
        z = z + jnp.where(sub + sh < SUBLANES, pltpu.roll(z, SUBLANES - sh, axis=0), 0.0)
        sh *= 2
    return x + (z - y) - lf, z[0:1]


def _fox_decode_kernel(pt_ref, q_ref, kn_ref, vn_ref, cq_ref, ck_ref, *refs, pages):
    del pt_ref
    k_refs = refs[:pages]
    v_refs = refs[pages:2 * pages]
    lf_refs = refs[2 * pages:3 * pages]
    o_ref = refs[3 * pages]
    qm_scr, m_scr, l_scr, acc_scr, r_scr = refs[3 * pages + 1:]
    step = pl.program_id(1)
    n_tok = q_ref.shape[1]
    rows = n_tok * N_HEADS
    page_size = k_refs[0].shape[1]
    lane = lax.broadcasted_iota(I32, (rows, LANES), 1)
    row = lax.broadcasted_iota(I32, (rows, LANES), 0)
    row_head = row // n_tok
    row_tok = row - row_head * n_tok
    cq = cq_ref[0]

    def by_head(x):
        return jnp.concatenate(
            [x[:, h * HEAD_DIM:(h + 1) * HEAD_DIM] for h in range(N_HEADS)], axis=0)

    def attend(u, v_bf):
        m_old = m_scr[...]
        m_new = jnp.maximum(m_old, jnp.max(u, axis=1, keepdims=True) + cq)
        alpha = jnp.exp(m_old - m_new)
        pr = jnp.exp(u - (m_new - cq))
        l_scr[...] = alpha * l_scr[...] + jnp.sum(pr, axis=1, keepdims=True)
        acc_scr[...] = alpha * acc_scr[...] + jnp.dot(pr.astype(BF16), v_bf,
                                                      preferred_element_type=F32)
        m_scr[...] = m_new

    @pl.when(step == 0)
    def _():
        qm = by_head(q_ref[0].astype(F32)).astype(BF16)
        qm_scr[...] = qm
        m_scr[...] = jnp.full_like(m_scr, -jnp.inf)
        l_scr[...] = jnp.zeros_like(l_scr)
        acc_scr[...] = jnp.zeros_like(acc_scr)
        r_scr[...] = jnp.zeros_like(r_scr)
        pad = jnp.zeros((LANES - rows, HEAD_DIM), F32)
        kn = jnp.concatenate([by_head(kn_ref[0].astype(F32)), pad], axis=0).astype(BF16)
        vn = jnp.concatenate([by_head(vn_ref[0].astype(F32)), pad], axis=0).astype(BF16)
        s = lax.dot_general(qm, kn, NT_DIMS, preferred_element_type=F32)
        key_head = lane // n_tok
        key_tok = lane - key_head * n_tok
        u = jnp.where(key_head == row_head, jnp.where(key_tok <= row_tok, s - ck_ref[0], -jnp.inf),
                      -jnp.inf)
        attend(u, vn)

    own_head = (lane & (N_HEADS - 1)) == row_head
    qm = qm_scr[...]
    after_sum = r_scr[...]
    u_parts, v_parts = [], []
    for pg in range(pages):
        later, total = _page_suffix(lf_refs[pg][0])
        after = after_sum + later
        after_sum = after_sum + total
        k2 = k_refs[pg][0].reshape(page_size * N_HEADS, HEAD_DIM).astype(BF16)
        v_parts.append(v_refs[pg][0].reshape(page_size * N_HEADS, HEAD_DIM).astype(BF16))
        s = lax.dot_general(qm, k2, NT_DIMS, preferred_element_type=F32)
        u_parts += [jnp.where(own_head, s[:, c * LANES:(c + 1) * LANES] + after[c:c + 1], -jnp.inf)
                    for c in range(page_size * N_HEADS // LANES)]
    r_scr[...] = after_sum
    attend(jnp.concatenate(u_parts, axis=1), jnp.concatenate(v_parts, axis=0))

    @pl.when(step == pl.num_programs(1) - 1)
    def _():
        o = acc_scr[...] / l_scr[...]
        o_ref[0] = jnp.concatenate(
            [o[h * n_tok:(h + 1) * n_tok] for h in range(N_HEADS)], axis=1).astype(BF16)


def _fox_decode(q, k_new, v_new, cq, ck, cache_k, cache_v, cache_lf, page_table):
    bsz, n_tok, d = q.shape
    n_pages = page_table.shape[1]
    page_size = cache_k.shape[1]
    pages = PAGES_PER_STEP
    rows = n_tok * N_HEADS

    def page_map(pg, ndim):
        def index_map(b, s, pt):
            return (pt[b, n_pages - 1 - (s * pages + pg)],) + (0,) * (ndim - 1)
        return index_map

    seq = lambda b, s, pt: (b, 0, 0)
    in_specs = [pl.BlockSpec((1, n_tok, d), seq), pl.BlockSpec((1, n_tok, d), seq),
                pl.BlockSpec((1, n_tok, d), seq), pl.BlockSpec((1, rows, 1), seq),
                pl.BlockSpec((1, 1, LANES), seq)]
    kv_block = (1, page_size, N_HEADS, HEAD_DIM)
    in_specs += [pl.BlockSpec(kv_block, page_map(pg, 4)) for pg in range(pages)]
    in_specs += [pl.BlockSpec(kv_block, page_map(pg, 4)) for pg in range(pages)]
    in_specs += [pl.BlockSpec((1, SUBLANES, LANES), page_map(pg, 3)) for pg in range(pages)]
    grid_spec = pltpu.PrefetchScalarGridSpec(
        num_scalar_prefetch=1,
        grid=(bsz, n_pages // pages),
        in_specs=in_specs,
        out_specs=pl.BlockSpec((1, n_tok, d), seq),
        scratch_shapes=[pltpu.VMEM((rows, HEAD_DIM), BF16), pltpu.VMEM((rows, 1), F32),
                        pltpu.VMEM((rows, 1), F32), pltpu.VMEM((rows, HEAD_DIM), F32),
                        pltpu.VMEM((1, LANES), F32)],
    )
    return pl.pallas_call(
        functools.partial(_fox_decode_kernel, pages=pages),
        grid_spec=grid_spec,
        out_shape=jax.ShapeDtypeStruct((bsz, n_tok, d), BF16),
        compiler_params=_params("arbitrary", "arbitrary"),
        name="fox_decode_attention",
    )(page_table, q, k_new, v_new, cq, ck, *([cache_k] * pages), *([cache_v] * pages),
      *([cache_lf] * pages))


def _oproj_kernel(x_ref, o_ref, g1_ref, wo_ref, xo_ref):
    xo_ref[0] = x_ref[0] + g1_ref[0] * jnp.dot(o_ref[0], wo_ref[...], preferred_element_type=F32)


def _oproj(x, o, g1, w_o, *, tm):
    bsz, t_len, d = x.shape
    tok = pl.BlockSpec((1, tm, d), lambda b, t: (b, t, 0))
    return pl.pallas_call(
        _oproj_kernel,
        grid=(bsz, t_len // tm),
        in_specs=[tok, tok, _mod_spec(g1, tm), _full_spec(w_o)],
        out_specs=tok,
        out_shape=jax.ShapeDtypeStruct((bsz, t_len, d), F32),
        compiler_params=_params("arbitrary", "arbitrary"),
        name="attn_out_proj",
    )(x, o, g1, w_o)


def _final_kernel(x_ref, y_ref, g2_ref, g_ref, o_ref):
    x = x_ref[0] + g2_ref[0] * y_ref[...]
    o_ref[0] = (x * lax.rsqrt(jnp.mean(x * x, axis=-1, keepdims=True) + EPS)) * g_ref[...]


def _final(x, y, g2, g_final, *, tm):
    bsz, t_len, d = x.shape
    tok = pl.BlockSpec((1, tm, d), lambda b, t: (b, t, 0))
    return pl.pallas_call(
        _final_kernel,
        grid=(bsz, t_len // tm),
        in_specs=[tok, _row_spec(t_len // tm, tm, d), _mod_spec(g2, tm), _full_spec(g_final)],
        out_specs=tok,
        out_shape=jax.ShapeDtypeStruct((bsz, t_len, d), F32),
        compiler_params=_params("arbitrary", "arbitrary"),
        name="final_norm",
    )(x, y, g2, g_final)


def _trunk(x, mods0, mods1, kvmods, prev, h0, w, attend, *, tm, tmx, seg, carry):
    sh1, sc1, g1, sh2, sc2, g2 = mods0
    x, h_out, conv_out = _lru_layer(x, sh1, sc1, g1, w["lru"], prev, h0, tm=tm, seg=seg, carry=carry)
    y = _moe(x, sh2, sc2, w["moe0"], tm=tm, tmx=tmx)
    sh1b, sc1b, g1b, sh2b, sc2b, g2b = mods1
    shk, sck = kvmods
    x, k, v, lft, fct, *operands = _post0(x, y, g2, shk, sck, sh1b, sc1b, w["post0"],
                                          tm=tm, seg=seg, carry=carry, prompt_layout=carry)
    o = attend(fct, *operands)
    x = _oproj(x, o, g1b, w["w_o"], tm=tm)
    y = _moe(x, sh2b, sc2b, w["moe1"], tm=tm, tmx=tmx)
    y_out = _final(x, y, g2b, w["g_final"], tm=tm)
    return y_out, h_out, conv_out, k, v, lft


def kernel(x_prompt, x_sample, state_h, state_conv, cache_k, cache_v, cache_logf, page_table,
           c_prompt, c_sample, w_mod, b_mod, g_mix, g_moe, lru_w_in, lru_conv_w, lru_conv_b,
           lru_w_a, lru_b_a, lru_w_x, lru_b_x, lru_lambda, lru_w_out, kv_g, kv_w_mod, kv_b_mod,
           kv_w, kv_b_f, attn_w_q, attn_w_o, moe_w_grp, moe_b_grp, moe_w_exp, moe_b_exp,
           moe_w_up, moe_w_down, g_final):
    d = D_MODEL
    bp, tp, _ = x_prompt.shape
    bs, ts, _ = x_sample.shape
    hd = N_HEADS * HEAD_DIM
    row = lambda a: a.reshape(1, -1)

    def moe_weights(layer):
        wr_t = jnp.concatenate(
            [moe_w_grp[layer].T, moe_w_exp[layer].T,
             jnp.zeros((ROUTER_ROWS - N_GROUPS - N_EXPERTS, d), F32)], axis=0)
        br = jnp.concatenate(
            [moe_b_grp[layer], moe_b_exp[layer],
             jnp.zeros((ROUTER_ROWS - N_GROUPS - N_EXPERTS,), F32)]).reshape(ROUTER_ROWS, 1)
        return dict(g_moe=row(g_moe[layer]), wr_t=wr_t, br=br,
                    w_up=moe_w_up[layer].astype(BF16), w_down=moe_w_down[layer].astype(BF16))

    w = dict(
        lru=dict(g_mix=row(g_mix[0]), w_in=lru_w_in[0].astype(BF16), conv_w=lru_conv_w[0],
                 conv_b=row(lru_conv_b[0]), w_a=lru_w_a[0].astype(BF16), b_a=row(lru_b_a[0]),
                 w_x=lru_w_x[0].astype(BF16), b_x=row(lru_b_x[0]), lam=row(lru_lambda[0]),
                 w_out=lru_w_out[0].astype(BF16)),
        moe0=moe_weights(0),
        moe1=moe_weights(1),
        post0=dict(kv_g=row(kv_g), w_k=kv_w[:, :hd].astype(BF16),
                   w_v=kv_w[:, hd:2 * hd].astype(BF16), w_f_t=kv_w[:, 2 * hd:].T,
                   w_v_t=kv_w[:, hd:2 * hd].T.astype(BF16), b_f=kv_b_f.reshape(N_HEADS, 1),
                   g_mix1=row(g_mix[1]), w_q=attn_w_q[0].astype(BF16),
                   w_q_t=attn_w_q[0].T.astype(BF16)),
        w_o=attn_w_o[0].astype(BF16),
        g_final=row(g_final),
    )

    n_c = bp + bs
    c_rows = -(-n_c // SUBLANES) * SUBLANES
    c_all = jnp.concatenate([c_prompt, c_sample, jnp.zeros((c_rows - n_c, d), F32)], axis=0)
    mod_l0 = _mm_bias(c_all, w_mod[0], b_mod[0])
    mod_l1 = _mm_bias(c_all, w_mod[1], b_mod[1])
    mod_kv = _mm_bias(c_all, kv_w_mod, kv_b_mod)

    def prompt_mods(m, parts):
        return [a.reshape(bp, 1, d) for a in jnp.split(m[:bp], parts, axis=-1)]

    def sample_mods(m, parts):
        return [jnp.repeat(a, ts, axis=0).reshape(1, bs * ts, d)
                for a in jnp.split(m[bp:n_c], parts, axis=-1)]

    prev_p = jnp.zeros((bp, SUBLANES, D_RNN), F32)
    h0_p = jnp.zeros((bp, 1, D_RNN), F32)
    attend_p = lambda fct, qa, ka, vt: _fox_prompt(qa, ka, vt)
    y_p, h_p, conv_p, k_p, v_p, lft_p = _trunk(
        x_prompt, prompt_mods(mod_l0, 6), prompt_mods(mod_l1, 6), prompt_mods(mod_kv, 2),
        prev_p, h0_p, w, attend_p, tm=TOKEN_TILE, tmx=EXPERT_TILE, seg=TOKEN_TILE, carry=True)

    n_s = bs * ts
    prev_s = jnp.pad(state_conv[0], ((0, 0), (ts - (CONV_WIDTH - 1), 0), (0, 0))).reshape(1, n_s, D_RNN)
    h0_s = jnp.repeat(state_h[0], ts, axis=0).reshape(1, n_s, D_RNN)
    n_pool, page_size = cache_k.shape[0], cache_k.shape[1]
    cache_lf = cache_logf.reshape(n_pool, SUBLANES, page_size * N_HEADS // SUBLANES)

    def attend_s(fct, q, kb, vb):
        c_new = jnp.transpose(fct.reshape(N_HEADS, bs, ts), (1, 0, 2)).reshape(bs, N_HEADS * ts)
        cq = c_new.reshape(bs, N_HEADS * ts, 1)
        ck = jnp.pad(c_new, ((0, 0), (0, LANES - N_HEADS * ts))).reshape(bs, 1, LANES)
        o = _fox_decode(q.reshape(bs, ts, d), kb.reshape(bs, ts, d), vb.reshape(bs, ts, d), cq, ck,
                        cache_k, cache_v, cache_lf, page_table)
        return o.reshape(1, n_s, d)

    y_s, h_s, conv_s, k_s, v_s, lft_s = _trunk(
        x_sample.reshape(1, n_s, d), sample_mods(mod_l0, 6), sample_mods(mod_l1, 6),
        sample_mods(mod_kv, 2), prev_s, h0_s, w, attend_s, tm=n_s, tmx=SAMPLE_EXPERT_TILE, seg=ts,
        carry=False)

    n_conv = CONV_WIDTH - 1
    return (
        y_p,
        y_s.reshape(bs, ts, d),
        h_p.reshape(1, bp, D_RNN),
        conv_p[:, SUBLANES - n_conv:].reshape(1, bp, n_conv, D_RNN),
        k_p,
        v_p,
        jnp.transpose(lft_p, (0, 2, 1)),
        h_s.reshape(bs, ts, D_RNN)[:, ts - 1].reshape(1, bs, D_RNN),
        conv_s.reshape(bs, ts, D_RNN)[:, ts - n_conv:].reshape(1, bs, n_conv, D_RNN),
        k_s.reshape(bs, ts, N_HEADS, HEAD_DIM),
        v_s.reshape(bs, ts, N_HEADS, HEAD_DIM),
        jnp.transpose(lft_s.reshape(N_HEADS, bs, ts), (1, 2, 0)),
    )
```

```python
import functools
import math

import jax
import jax.numpy as jnp
from jax import lax
from jax.experimental import pallas as pl
from jax.experimental.pallas import tpu as pltpu

F32 = jnp.float32
BF16 = jnp.bfloat16
I32 = jnp.int32

D_MODEL = 1024
D_RNN = D_MODEL
N_LRU_BLOCKS = 8
LRU_BLOCK = D_RNN // N_LRU_BLOCKS
CONV_WIDTH = 4
LRU_C = 8.0
N_HEADS = 8
HEAD_DIM = D_MODEL // N_HEADS
N_GROUPS = 4
EXPERTS_PER_GROUP = 4
N_EXPERTS = N_GROUPS * EXPERTS_PER_GROUP
D_EXPERT = D_MODEL // 2
EPS = 1e-6
N_PAIRS = 6
N_CLASSES = N_GROUPS * N_PAIRS
PAIR_LO = (0, 0, 0, 1, 1, 2)
PAIR_HI = (1, 2, 3, 2, 3, 3)

SUBLANES = 8
LANES = 128
VMEM_LIMIT = 48 * 1024 * 1024
DMA_THREADS = 2

TOKEN_TILE = 256
EXPERT_TILE = 256
SAMPLE_EXPERT_TILE = 32
ROUTER_ROWS = 32
ATTN_TILE = 512
PAGES_PER_STEP = 8
MOD_TILE = 512

HIGHEST = lax.Precision.HIGHEST
NT_DIMS = (((1,), (1,)), ((), ()))


def _params(*semantics):
    return pltpu.CompilerParams(dimension_semantics=semantics, vmem_limit_bytes=VMEM_LIMIT)


def _norm_mod(x, g, shift, scale):
    y = x * lax.rsqrt(jnp.mean(x * x, axis=-1, keepdims=True) + EPS)
    return (y * g) * (1.0 + scale) + shift


def _log_sigmoid(x):
    return jnp.minimum(x, 0.0) - jnp.log1p(jnp.exp(-jnp.abs(x)))


def _sigmoid(x):
    return 1.0 / (1.0 + jnp.exp(-x))


def _gelu_tanh(x):
    c = math.sqrt(2.0 / math.pi)
    return x * (0.5 * (1.0 + jnp.tanh(c * (x + 0.044715 * (x * x * x)))))


def _mod_spec(arr, tm):
    if arr.shape[1] == 1:
        return pl.BlockSpec((1, 1, arr.shape[2]), lambda b, t: (b, 0, 0))
    return pl.BlockSpec((1, tm, arr.shape[2]), lambda b, t: (b, t, 0))


ROW_TILES = D_MODEL // LANES


def _row_spec(nt, tm):
    return pl.BlockSpec((tm * ROW_TILES, LANES), lambda b, t: (b * nt + t, 0))


def _to_token_tiles(ref, lead, x):
    rows = x.shape[0]
    for c in range(ROW_TILES):
        ref[(*lead, pl.ds(c, rows, stride=ROW_TILES), slice(None))] = x[:, c * LANES:(c + 1) * LANES]


def _from_token_tiles(ref, lead, rows):
    return jnp.concatenate(
        [ref[(*lead, pl.ds(c, rows, stride=ROW_TILES), slice(None))] for c in range(ROW_TILES)],
        axis=1)


def _full_spec(arr):
    zeros = (0,) * arr.ndim
    return pl.BlockSpec(arr.shape, lambda *_: zeros)


def _mm_bias_kernel(x_ref, w_ref, b_ref, o_ref):
    o_ref[...] = jnp.dot(x_ref[...].astype(BF16), w_ref[...].astype(BF16),
                         preferred_element_type=F32) + b_ref[...]


def _mm_bias(x, w, b):
    m, d = x.shape
    n = w.shape[1]
    return pl.pallas_call(
        _mm_bias_kernel,
        grid=(n // MOD_TILE,),
        in_specs=[pl.BlockSpec((m, d), lambda j: (0, 0)),
                  pl.BlockSpec((d, MOD_TILE), lambda j: (0, j)),
                  pl.BlockSpec((1, MOD_TILE), lambda j: (0, j))],
        out_specs=pl.BlockSpec((m, MOD_TILE), lambda j: (0, j)),
        out_shape=jax.ShapeDtypeStruct((m, n), F32),
        compiler_params=_params("arbitrary"),
        name="mod_vectors",
    )(x, w, b.reshape(1, n))


def _lru_kernel(x_ref, sh_ref, sc_ref, gt_ref, gmix_ref, win_ref, cw_ref, cb_ref, wa_ref, ba_ref,
                wx_ref, bx_ref, lam_ref, wout_ref, prev_ref, h0_ref,
                xo_ref, hs_ref, xb_ref, *scratch, tm, seg, carry):
    x = x_ref[0]
    hn = _norm_mod(x, gmix_ref[...], sh_ref[0], sc_ref[0])
    xg = jnp.dot(hn.astype(BF16), win_ref[...], preferred_element_type=F32)
    xb = xg[:, :D_RNN]
    gb = xg[:, D_RNN:]

    if carry:
        prev_scr, h_scr = scratch

        @pl.when(pl.program_id(1) == 0)
        def _():
            prev_scr[...] = prev_ref[0]
            h_scr[...] = h0_ref[0]

        prev = prev_scr[...]
        hprev = h_scr[...]
    else:
        prev = prev_ref[0]
        hprev = h0_ref[0]

    row = lax.broadcasted_iota(I32, (tm, 1), 0)
    rseg = row & (seg - 1)
    nprev = prev.shape[0]

    cw = cw_ref[...]
    xc = cb_ref[...] + cw[CONV_WIDTH - 1:CONV_WIDTH] * xb
    for d in range(1, CONV_WIDTH):
        rolled = pltpu.roll(xb, d, axis=0)
        rp = pltpu.roll(prev, nprev - SUBLANES + d, axis=0)
        if carry:
            head = jnp.where(row[:SUBLANES] < d, rp, rolled[:SUBLANES])
            shifted = head if tm == SUBLANES else jnp.concatenate([head, rolled[SUBLANES:]], axis=0)
        else:
            shifted = jnp.where(rseg < d, rp, rolled)
        xc = xc + cw[CONV_WIDTH - 1 - d:CONV_WIDTH - d] * shifted

    xcb = xc.astype(BF16)
    ra = jnp.concatenate(
        [jnp.dot(xcb[:, n * LRU_BLOCK:(n + 1) * LRU_BLOCK], wa_ref[n], preferred_element_type=F32)
         for n in range(N_LRU_BLOCKS)], axis=1)
    rx = jnp.concatenate(
        [jnp.dot(xcb[:, n * LRU_BLOCK:(n + 1) * LRU_BLOCK], wx_ref[n], preferred_element_type=F32)
         for n in range(N_LRU_BLOCKS)], axis=1)
    r = _sigmoid(ra + ba_ref[...])
    ig = _sigmoid(rx + bx_ref[...])
    log_a = (LRU_C * r) * _log_sigmoid(lam_ref[...])
    a = jnp.exp(log_a)
    bt = jnp.sqrt(-jnp.tanh(log_a) * (a * a + 1.0)) * (ig * xc)

    s = 1
    while s < seg:
        inside = rseg >= s
        a_sh = jnp.where(inside, pltpu.roll(a, s, axis=0), 1.0)
        b_sh = jnp.where(inside, pltpu.roll(bt, s, axis=0), 0.0)
        bt = a * b_sh + bt
        a = a * a_sh
        s *= 2
    hs = bt + a * hprev

    if carry:
        h_scr[...] = hs[tm - 1:tm]
        prev_scr[...] = xb[tm - SUBLANES:]
        hs_ref[0] = hs[tm - 1:tm]
        xb_ref[0] = xb[tm - SUBLANES:]
    else:
        hs_ref[0] = hs
        xb_ref[0] = xb

    y = (hs * _gelu_tanh(gb)).astype(BF16)
    out = jnp.dot(y, wout_ref[...], preferred_element_type=F32)
    xo_ref[0] = x + gt_ref[0] * out


def _lru_layer(x, sh, sc, gt, p, prev, h0, *, tm, seg, carry):
    bsz, t_len, d = x.shape
    r = D_RNN
    nt = t_len // tm
    tok = pl.BlockSpec((1, tm, d), lambda b, t: (b, t, 0))
    if carry:
        state_specs = [pl.BlockSpec((1, SUBLANES, r), lambda b, t: (b, 0, 0)),
                       pl.BlockSpec((1, 1, r), lambda b, t: (b, 0, 0))]
        out_specs = [tok,
                     pl.BlockSpec((1, 1, r), lambda b, t: (b, 0, 0)),
                     pl.BlockSpec((1, SUBLANES, r), lambda b, t: (b, 0, 0))]
        out_shape = [jax.ShapeDtypeStruct((bsz, t_len, d), F32),
                     jax.ShapeDtypeStruct((bsz, 1, r), F32),
                     jax.ShapeDtypeStruct((bsz, SUBLANES, r), F32)]
        scratch = [pltpu.VMEM((SUBLANES, r), F32), pltpu.VMEM((1, r), F32)]
    else:
        rtok = pl.BlockSpec((1, tm, r), lambda b, t: (b, t, 0))
        state_specs = [rtok, rtok]
        out_specs = [tok, rtok, rtok]
        out_shape = [jax.ShapeDtypeStruct((bsz, t_len, d), F32),
                     jax.ShapeDtypeStruct((bsz, t_len, r), F32),
                     jax.ShapeDtypeStruct((bsz, t_len, r), F32)]
        scratch = []
    weights = [p["g_mix"], p["w_in"], p["conv_w"], p["conv_b"], p["w_a"], p["b_a"], p["w_x"],
               p["b_x"], p["lam"], p["w_out"]]
    return pl.pallas_call(
        functools.partial(_lru_kernel, tm=tm, seg=seg, carry=carry),
        grid=(bsz, nt),
        in_specs=[tok, _mod_spec(sh, tm), _mod_spec(sc, tm), _mod_spec(gt, tm)]
                 + [_full_spec(w) for w in weights] + state_specs,
        out_specs=out_specs,
        out_shape=out_shape,
        scratch_shapes=scratch,
        compiler_params=_params("arbitrary", "arbitrary"),
        name="rglru_layer",
    )(x, sh, sc, gt, *weights, prev, h0)


def _router_kernel(x_ref, sh_ref, sc_ref, g_ref, wr_ref, br_ref,
                   hn_ref, cls_ref, rank_ref, glo_ref, ghi_ref, cnt_ref, carry_scr, *, tm):
    @pl.when((pl.program_id(0) == 0) & (pl.program_id(1) == 0))
    def _():
        carry_scr[...] = jnp.zeros_like(carry_scr)

    hn = _norm_mod(x_ref[0], g_ref[...], sh_ref[0], sc_ref[0])
    _to_token_tiles(hn_ref, (), hn)
    lt = lax.dot_general(wr_ref[...], hn, NT_DIMS, precision=HIGHEST,
                         preferred_element_type=F32) + br_ref[...]

    gl = [lt[k:k + 1] for k in range(N_GROUPS)]
    gmax = jnp.maximum(jnp.maximum(gl[0], gl[1]), jnp.maximum(gl[2], gl[3]))
    g_sel = jnp.where(gl[0] >= gmax, 0, jnp.where(gl[1] >= gmax, 1, jnp.where(gl[2] >= gmax, 2, 3)))
    p_grp = 1.0 / (jnp.exp(gl[0] - gmax) + jnp.exp(gl[1] - gmax)
                   + jnp.exp(gl[2] - gmax) + jnp.exp(gl[3] - gmax))

    def expert_logit(e):
        rows = [lt[N_GROUPS + g * EXPERTS_PER_GROUP + e:N_GROUPS + g * EXPERTS_PER_GROUP + e + 1]
                for g in range(N_GROUPS)]
        return jnp.where(g_sel == 0, rows[0],
                         jnp.where(g_sel == 1, rows[1], jnp.where(g_sel == 2, rows[2], rows[3])))

    es = [expert_logit(e) for e in range(EXPERTS_PER_GROUP)]

    def first_argmax(vals):
        vmax = jnp.maximum(jnp.maximum(vals[0], vals[1]), jnp.maximum(vals[2], vals[3]))
        idx = jnp.where(vals[0] >= vmax, 0,
                        jnp.where(vals[1] >= vmax, 1, jnp.where(vals[2] >= vmax, 2, 3)))
        return vmax, idx

    v1, i1 = first_argmax(es)
    rest = [jnp.where(i1 == e, -jnp.inf, es[e]) for e in range(EXPERTS_PER_GROUP)]
    v2, i2 = first_argmax(rest)
    e21 = jnp.exp(v2 - v1)
    w1 = (1.0 / (1.0 + e21)) * p_grp
    w2 = (e21 / (1.0 + e21)) * p_grp
    first_lower = i1 < i2
    lo = jnp.where(first_lower, i1, i2)
    hi = jnp.where(first_lower, i2, i1)
    glo_ref[0] = jnp.where(first_lower, w1, w2)
    ghi_ref[0] = jnp.where(first_lower, w2, w1)
    pair = jnp.where(lo == 0, hi - 1, jnp.where(lo == 1, hi + 1, 5))
    cls = g_sel * N_PAIRS + pair
    cls_ref[0] = cls

    crow = lax.broadcasted_iota(I32, (ROUTER_ROWS, tm), 0)
    onehot = jnp.where(crow == cls, 1.0, 0.0)
    ki = lax.broadcasted_iota(I32, (tm, tm), 0)
    kj = lax.broadcasted_iota(I32, (tm, tm), 1)
    upper = jnp.where(ki <= kj, 1.0, 0.0).astype(BF16)
    cum = jnp.dot(onehot.astype(BF16), upper, preferred_element_type=F32)
    carry = carry_scr[...]
    rank = jnp.sum(onehot * (cum - 1.0 + carry), axis=0, keepdims=True)
    rank_ref[0] = rank.astype(I32)
    carry = carry + cum[:, tm - 1:tm]
    carry_scr[...] = carry
    cnt_ref[...] = carry


def _router(x, sh, sc, g, wr_t, br, *, tm):
    bsz, t_len, d = x.shape
    nt = t_len // tm
    n_tiles = bsz * nt
    tok = pl.BlockSpec((1, tm, d), lambda b, t: (b, t, 0))
    lane_row = pl.BlockSpec((1, 1, tm), lambda b, t: (b * nt + t, 0, 0))
    row_shape = jax.ShapeDtypeStruct((n_tiles, 1, tm), F32)
    row_shape_i = jax.ShapeDtypeStruct((n_tiles, 1, tm), I32)
    return pl.pallas_call(
        functools.partial(_router_kernel, tm=tm),
        grid=(bsz, nt),
        in_specs=[tok, _mod_spec(sh, tm), _mod_spec(sc, tm), _full_spec(g), _full_spec(wr_t),
                  _full_spec(br)],
        out_specs=[_row_spec(nt, tm), lane_row, lane_row, lane_row, lane_row,
                   pl.BlockSpec((ROUTER_ROWS, 1), lambda b, t: (0, 0))],
        out_shape=[jax.ShapeDtypeStruct((bsz * t_len * ROW_TILES, LANES), F32), row_shape_i, row_shape_i,
                   row_shape, row_shape, jax.ShapeDtypeStruct((ROUTER_ROWS, 1), F32)],
        scratch_shapes=[pltpu.VMEM((ROUTER_ROWS, 1), F32)],
        compiler_params=_params("arbitrary", "arbitrary"),
        name="moe_router",
    )(x, sh, sc, g, wr_t, br)


def _expert_kernel(e1_ref, e2_ref, valid_ref, src0_ref, srcn_ref, dst_ref, glo_ref, ghi_ref,
                   hn_hbm, wu1_ref, wd1_ref, wu2_ref, wd2_ref, y_hbm,
                   xbuf, obuf, sem_g, sem_s, *, tmx):
    del e1_ref, e2_ref
    j = pl.program_id(0)
    last = pl.num_programs(0) - 1
    slot = lax.rem(j, 2)

    tile_rows = tmx * ROW_TILES

    def token(row):
        return pl.ds(pl.multiple_of(row, ROW_TILES), ROW_TILES)

    def start_gather(idx_ref, to_slot):
        for r in range(tmx):
            pltpu.make_async_copy(hn_hbm.at[token(idx_ref[0, 0, r])],
                                  xbuf.at[to_slot, pl.ds(r * ROW_TILES, ROW_TILES)],
                                  sem_g.at[to_slot]).start(priority=r % DMA_THREADS)

    def wait_gather(of_slot):
        pltpu.make_async_copy(hn_hbm.at[pl.ds(0, tile_rows)], xbuf.at[of_slot],
                              sem_g.at[of_slot]).wait()

    def wait_scatter():
        pltpu.make_async_copy(obuf.at[0], y_hbm.at[pl.ds(0, tile_rows)], sem_s.at[0]).wait()

    @pl.when(j == 0)
    def _():
        start_gather(src0_ref, 0)
        obuf[1] = jnp.zeros((tile_rows, LANES), F32)
        spare = pltpu.make_async_copy(
            obuf.at[1], y_hbm.at[pl.ds(y_hbm.shape[0] - tile_rows, tile_rows)], sem_s.at[0])
        spare.start()
        spare.wait()

    @pl.when(valid_ref[j] == 1)
    def _():
        wait_gather(slot)
        start_gather(srcn_ref, 1 - slot)
        x = _from_token_tiles(xbuf, (slot,), tmx).astype(BF16)
        ri = lax.broadcasted_iota(I32, (tmx, tmx), 0)
        ci = lax.broadcasted_iota(I32, (tmx, tmx), 1)

        def column(row_ref):
            return jnp.sum(jnp.where(ri == ci, row_ref[0], 0.0), axis=1, keepdims=True)

        def ffn(wu_ref, wd_ref, gate):
            gu = jnp.dot(x, wu_ref[0], preferred_element_type=F32)
            g_ = gu[:, :D_EXPERT]
            u_ = gu[:, D_EXPERT:]
            act = ((g_ * _sigmoid(g_)) * u_).astype(BF16)
            return gate * jnp.dot(act, wd_ref[0], preferred_element_type=F32)

        _to_token_tiles(obuf, (slot,), ffn(wu1_ref, wd1_ref, column(glo_ref))
                        + ffn(wu2_ref, wd2_ref, column(ghi_ref)))

        @pl.when(j > 0)
        def _():
            wait_scatter()

        for r in range(tmx):
            pltpu.make_async_copy(obuf.at[slot, pl.ds(r * ROW_TILES, ROW_TILES)],
                                  y_hbm.at[token(dst_ref[0, 0, r])],
                                  sem_s.at[0]).start(priority=r % DMA_THREADS)

        @pl.when((j == last) | (valid_ref[jnp.minimum(j + 1, last)] == 0))
        def _():
            wait_scatter()
            wait_gather(1 - slot)


def _experts(hn, src, dst, glo_s, ghi_s, w_up, w_down, e1, e2, valid, n_tiles, tmx):
    n = hn.shape[0] // ROW_TILES
    d = D_MODEL
    idx_block = (1, 1, tmx)
    at_tile = lambda j, e1, e2, v: (j, 0, 0)
    grid_spec = pltpu.PrefetchScalarGridSpec(
        num_scalar_prefetch=3,
        grid=(n_tiles,),
        in_specs=[
            pl.BlockSpec(idx_block, lambda j, e1, e2, v: (0, 0, 0), memory_space=pltpu.SMEM),
            pl.BlockSpec(idx_block, lambda j, e1, e2, v: (jnp.minimum(j + 1, n_tiles - 1), 0, 0),
                         memory_space=pltpu.SMEM),
            pl.BlockSpec(idx_block, at_tile, memory_space=pltpu.SMEM),
            pl.BlockSpec(idx_block, at_tile),
            pl.BlockSpec(idx_block, at_tile),
            pl.BlockSpec(memory_space=pl.ANY),
            pl.BlockSpec((1, d, 2 * D_EXPERT), lambda j, e1, e2, v: (e1[j], 0, 0)),
            pl.BlockSpec((1, D_EXPERT, d), lambda j, e1, e2, v: (e1[j], 0, 0)),
            pl.BlockSpec((1, d, 2 * D_EXPERT), lambda j, e1, e2, v: (e2[j], 0, 0)),
            pl.BlockSpec((1, D_EXPERT, d), lambda j, e1, e2, v: (e2[j], 0, 0)),
        ],
        out_specs=pl.BlockSpec(memory_space=pl.ANY),
        scratch_shapes=[pltpu.VMEM((2, tmx * ROW_TILES, LANES), F32),
                        pltpu.VMEM((2, tmx * ROW_TILES, LANES), F32),
                        pltpu.SemaphoreType.DMA((2,)), pltpu.SemaphoreType.DMA((1,))],
    )
    return pl.pallas_call(
        functools.partial(_expert_kernel, tmx=tmx),
        grid_spec=grid_spec,
        out_shape=jax.ShapeDtypeStruct(((n + tmx) * ROW_TILES, LANES), F32),
        compiler_params=_params("arbitrary"),
        name="moe_experts",
    )(e1, e2, valid, src, src, dst, glo_s, ghi_s, hn, w_up, w_down, w_up, w_down)


def _moe(x, sh, sc, p, *, tm, tmx):
    bsz, t_len, d = x.shape
    n = bsz * t_len
    hn, cls, rank, glo, ghi, cnt = _router(x, sh, sc, p["g_moe"], p["wr_t"], p["br"], tm=tm)
    cls = cls.reshape(n)
    rank = rank.reshape(n)
    counts = cnt[:N_CLASSES, 0].astype(I32)
    padded = ((counts + tmx - 1) // tmx) * tmx
    ends = jnp.cumsum(padded)
    offs = ends - padded
    pos = offs[cls] + rank
    n_tiles = n // tmx + N_CLASSES
    n_rows = n_tiles * tmx
    total = ends[N_CLASSES - 1]
    tile_start = jnp.arange(n_tiles, dtype=I32) * tmx
    valid = (tile_start < total).astype(I32)
    last_tile = jnp.maximum(total // tmx - 1, 0)
    tile_cls = jnp.minimum(
        jnp.sum((ends[None, :] <= jnp.minimum(tile_start, last_tile * tmx)[:, None]).astype(I32),
                axis=1), N_CLASSES - 1)
    grp = tile_cls // N_PAIRS
    pair = tile_cls % N_PAIRS
    e1 = grp * EXPERTS_PER_GROUP + jnp.asarray(PAIR_LO, I32)[pair]
    e2 = grp * EXPERTS_PER_GROUP + jnp.asarray(PAIR_HI, I32)[pair]
    token_info = jnp.stack([jnp.arange(n, dtype=I32).astype(F32), glo.reshape(n), ghi.reshape(n),
                            jnp.ones((n,), F32)], axis=1)
    row_info = jnp.zeros((n_rows, 4), F32).at[pos].set(token_info)
    src = row_info[:, 0].astype(I32)
    dst = jnp.where(row_info[:, 3] == 1.0, src, n + jnp.arange(n_rows, dtype=I32) % tmx)
    as_tiles = lambda a: a.reshape(n_tiles, 1, tmx)
    first_row = lambda tokens: as_tiles(tokens * ROW_TILES)
    return _experts(hn, first_row(src), first_row(dst), as_tiles(row_info[:, 1]),
                    as_tiles(row_info[:, 2]), p["w_up"], p["w_down"], e1, e2, valid, n_tiles, tmx)


AUG = 2 * HEAD_DIM
N_PIECES = 3
VT_ROWS = HEAD_DIM + 16
LOG2E = 1.0 / math.log(2.0)


def _bf16_pieces(x):
    pieces = []
    rest = x
    for _ in range(N_PIECES):
        piece = rest.astype(BF16).astype(F32)
        pieces.append(piece)
        rest = rest - piece
    return pieces


def _post0_kernel(x_ref, y_ref, g2_ref, shk_ref, sck_ref, kvg_ref, wk_ref, wv_ref, wft_ref, bf_ref,
                  shq_ref, scq_ref, gq_ref, wq_ref, *refs, tm, seg, carry, prompt_layout):
    if prompt_layout:
        wvt_ref, x1_ref, k_ref, v_ref, lft_ref, fct_ref, qa_ref, ka_ref, vt_ref = refs[:9]
        scratch = refs[9:]
    else:
        x1_ref, k_ref, v_ref, lft_ref, fct_ref, q_ref, kb_ref, vb_ref = refs[:8]
        scratch = refs[8:]
    scale = HEAD_DIM ** -0.5
    x1 = x_ref[0] + g2_ref[0] * _from_token_tiles(y_ref, (), tm)
    x1_ref[0] = x1
    hk = _norm_mod(x1, kvg_ref[...], shk_ref[0], sck_ref[0])
    hkb = hk.astype(BF16)
    k = jnp.dot(hkb, wk_ref[...], preferred_element_type=F32)
    v = jnp.dot(hkb, wv_ref[...], preferred_element_type=F32)
    k_ref[0] = k
    v_ref[0] = v
    f = lax.dot_general(wft_ref[...], hk, NT_DIMS, precision=HIGHEST,
                        preferred_element_type=F32) + bf_ref[...]
    lf = _log_sigmoid(f)
    lft_ref[0] = lf
    col = lax.broadcasted_iota(I32, (1, tm), 1)
    cseg = col & (seg - 1)
    c = lf
    s = 1
    while s < seg:
        c = c + jnp.where(cseg >= s, pltpu.roll(c, s, axis=1), 0.0)
        s *= 2
    if carry:
        (f_scr,) = scratch

        @pl.when(pl.program_id(1) == 0)
        def _():
            f_scr[...] = jnp.zeros_like(f_scr)

        c = c + f_scr[...]
        f_scr[...] = c[:, tm - 1:tm]
    fct_ref[0] = c
    hqb = _norm_mod(x1, gq_ref[...], shq_ref[0], scq_ref[0]).astype(BF16)

    if not prompt_layout:
        q_ref[0] = (jnp.dot(hqb, wq_ref[...], preferred_element_type=F32) * scale).astype(BF16)
        kb_ref[0] = k.astype(BF16)
        vb_ref[0] = v.astype(BF16)
        return

    qt = lax.dot_general(wq_ref[...], hqb, NT_DIMS, preferred_element_type=F32) * (scale * LOG2E)
    vt = lax.dot_general(wvt_ref[...], hkb, NT_DIMS, preferred_element_type=F32)
    ones_rows = jnp.where(lax.broadcasted_iota(I32, (VT_ROWS - HEAD_DIM, tm), 0) == 0, 1.0, 0.0)
    v_rows = []
    for h in range(N_HEADS):
        v_rows += [vt[h * HEAD_DIM:(h + 1) * HEAD_DIM], ones_rows]
    vt_ref[0] = jnp.concatenate(v_rows, axis=0).astype(BF16)
    hi, mid, lo = _bf16_pieces(c * LOG2E)
    sub = lax.broadcasted_iota(I32, (SUBLANES, tm), 0)
    pad = jnp.zeros((HEAD_DIM - SUBLANES, tm), F32)
    q_rows, k_rows = [], []
    for h in range(N_HEADS):
        hs = slice(h, h + 1)
        eq = jnp.where(sub < 3, -1.0, jnp.where(sub == 3, hi[hs], jnp.where(
            sub == 4, mid[hs], jnp.where(sub == 5, lo[hs], 0.0))))
        ek = jnp.where(sub == 0, hi[hs], jnp.where(sub == 1, mid[hs], jnp.where(
            sub == 2, lo[hs], jnp.where(sub < 6, 1.0, 0.0))))
        q_rows += [qt[h * HEAD_DIM:(h + 1) * HEAD_DIM], eq, pad]
        k_rows += [ek, pad]
    qa_ref[0] = jnp.concatenate(q_rows, axis=0).astype(BF16)
    ekt = jnp.concatenate(k_rows, axis=0).astype(BF16)
    ri = lax.broadcasted_iota(I32, (tm, tm), 0)
    ci = lax.broadcasted_iota(I32, (tm, tm), 1)
    eye = jnp.where(ri == ci, 1.0, 0.0).astype(BF16)
    extra = lax.dot_general(eye, ekt, NT_DIMS, preferred_element_type=F32)
    k_cols = []
    for h in range(N_HEADS):
        hs = slice(h * HEAD_DIM, (h + 1) * HEAD_DIM)
        k_cols += [k[:, hs], extra[:, hs]]
    ka_ref[0] = jnp.concatenate(k_cols, axis=1).astype(BF16)


def _post0(x, y, g2, shk, sck, shq, scq, p, *, tm, seg, carry, prompt_layout):
    bsz, t_len, d = x.shape
    tok = pl.BlockSpec((1, tm, d), lambda b, t: (b, t, 0))
    headrow = pl.BlockSpec((1, N_HEADS, tm), lambda b, t: (b, 0, t))
    weights_kv = [p["kv_g"], p["w_k"], p["w_v"], p["w_f_t"], p["b_f"]]
    tok_f32 = jax.ShapeDtypeStruct((bsz, t_len, d), F32)
    head_f32 = jax.ShapeDtypeStruct((bsz, N_HEADS, t_len), F32)
    if prompt_layout:
        weights_q = [p["g_mix1"], p["w_q_t"], p["w_v_t"]]
        out_specs = [tok, tok, tok, headrow, headrow,
                     pl.BlockSpec((1, N_HEADS * AUG, tm), lambda b, t: (b, 0, t)),
                     pl.BlockSpec((1, tm, N_HEADS * AUG), lambda b, t: (b, t, 0)),
                     pl.BlockSpec((1, N_HEADS * VT_ROWS, tm), lambda b, t: (b, 0, t))]
        out_shape = [tok_f32] * 3 + [head_f32] * 2 + [
            jax.ShapeDtypeStruct((bsz, N_HEADS * AUG, t_len), BF16),
            jax.ShapeDtypeStruct((bsz, t_len, N_HEADS * AUG), BF16),
            jax.ShapeDtypeStruct((bsz, N_HEADS * VT_ROWS, t_len), BF16)]
    else:
        weights_q = [p["g_mix1"], p["w_q"]]
        out_specs = [tok, tok, tok, headrow, headrow, tok, tok, tok]
        out_shape = [tok_f32] * 3 + [head_f32] * 2 + [jax.ShapeDtypeStruct((bsz, t_len, d), BF16)] * 3
    return pl.pallas_call(
        functools.partial(_post0_kernel, tm=tm, seg=seg, carry=carry, prompt_layout=prompt_layout),
        grid=(bsz, t_len // tm),
        in_specs=[tok, _row_spec(t_len // tm, tm), _mod_spec(g2, tm), _mod_spec(shk, tm),
                  _mod_spec(sck, tm)]
                 + [_full_spec(w) for w in weights_kv]
                 + [_mod_spec(shq, tm), _mod_spec(scq, tm)] + [_full_spec(w) for w in weights_q],
        out_specs=out_specs,
        out_shape=out_shape,
        scratch_shapes=[pltpu.VMEM((N_HEADS, 1), F32)] if carry else [],
        compiler_params=_params("arbitrary", "arbitrary"),
        name="shared_kv_and_q",
    )(x, y, g2, shk, sck, *weights_kv, shq, scq, *weights_q)


def _fox_prompt_kernel(qi_ref, kj_ref, qa_ref, ka_ref, vt_ref, o_ref, m_scr, acc_scr, *, tq):
    p = pl.program_id(1)
    i = qi_ref[p]
    j = kj_ref[p]

    @pl.when(j == 0)
    def _():
        m_scr[...] = jnp.full_like(m_scr, -jnp.inf)
        acc_scr[...] = jnp.zeros_like(acc_scr)

    def step(masked):
        if masked:
            key = lax.broadcasted_iota(I32, (tq, tq), 0)
            qry = lax.broadcasted_iota(I32, (tq, tq), 1)
        for h in range(N_HEADS):
            hv = slice(h * VT_ROWS, (h + 1) * VT_ROWS)
            s = jnp.dot(ka_ref[0, :, h * AUG:(h + 1) * AUG], qa_ref[0, h * AUG:(h + 1) * AUG, :],
                        preferred_element_type=F32)
            if masked:
                s = jnp.where(key <= qry, s, -jnp.inf)
            m_old = m_scr[h]
            m_new = jnp.maximum(m_old, jnp.max(s, axis=0, keepdims=True))
            alpha = jnp.exp2(m_old - m_new)
            pr = jnp.exp2(s - m_new).astype(BF16)
            acc_scr[hv, :] = alpha * acc_scr[hv, :] + jnp.dot(vt_ref[0, hv, :], pr,
                                                              preferred_element_type=F32)
            m_scr[h] = m_new

    @pl.when(j < i)
    def _():
        step(False)

    @pl.when(j == i)
    def _():
        step(True)
        for h in range(N_HEADS):
            base = h * VT_ROWS
            out = acc_scr[base:base + HEAD_DIM, :] / acc_scr[base + HEAD_DIM:base + HEAD_DIM + 1, :]
            o_ref[0, :, h * HEAD_DIM:(h + 1) * HEAD_DIM] = jnp.transpose(out).astype(BF16)


def _fox_prompt(qa, ka, vt):
    bsz, _, t_len = vt.shape
    d = N_HEADS * HEAD_DIM
    tq = ATTN_TILE
    nb = t_len // tq
    pairs = [(i, j) for i in range(nb) for j in range(i + 1)]
    qi = jnp.asarray([a for a, _ in pairs], I32)
    kj = jnp.asarray([b for _, b in pairs], I32)
    grid_spec = pltpu.PrefetchScalarGridSpec(
        num_scalar_prefetch=2,
        grid=(bsz, len(pairs)),
        in_specs=[
            pl.BlockSpec((1, N_HEADS * AUG, tq), lambda b, p, qi, kj: (b, 0, qi[p])),
            pl.BlockSpec((1, tq, N_HEADS * AUG), lambda b, p, qi, kj: (b, kj[p], 0)),
            pl.BlockSpec((1, N_HEADS * VT_ROWS, tq), lambda b, p, qi, kj: (b, 0, kj[p])),
        ],
        out_specs=pl.BlockSpec((1, tq, d), lambda b, p, qi, kj: (b, qi[p], 0)),
        scratch_shapes=[pltpu.VMEM((N_HEADS, 1, tq), F32),
                        pltpu.VMEM((N_HEADS * VT_ROWS, tq), F32)],
    )
    return pl.pallas_call(
        functools.partial(_fox_prompt_kernel, tq=tq),
        grid_spec=grid_spec,
        out_shape=jax.ShapeDtypeStruct((bsz, t_len, d), BF16),
        compiler_params=_params("arbitrary", "arbitrary"),
        name="fox_prompt_attention",
    )(qi, kj, qa, ka, vt)


def _page_suffix(lf):
    lane = lax.broadcasted_iota(I32, lf.shape, 1)
    sub = lax.broadcasted_iota(I32, lf.shape, 0)
    x = lf
    sh = N_HEADS
    while sh < LANES:
        x = x + jnp.where(lane + sh < LANES, pltpu.roll(x, LANES - sh, axis=1), 0.0)
        sh *= 2
    y = jnp.where(lane < N_HEADS, x, 0.0)
    sh = N_HEADS
    while sh < LANES:
        y = y + pltpu.roll(y, sh, axis=1)
        sh *= 2
    z = y
    sh = 1
    while sh < SUBLANES:
        z = z + jnp.where(sub + sh < SUBLANES, pltpu.roll(z, SUBLANES - sh, axis=0), 0.0)
        sh *= 2
    return x + (z - y) - lf, z[0:1]


def _fox_decode_kernel(pt_ref, q_ref, kn_ref, vn_ref, cq_ref, ck_ref, *refs, pages):
    del pt_ref
    k_refs = refs[:pages]
    v_refs = refs[pages:2 * pages]
    lf_refs = refs[2 * pages:3 * pages]
    o_ref = refs[3 * pages]
    qm_scr, m_scr, l_scr, acc_scr, r_scr = refs[3 * pages + 1:]
    step = pl.program_id(1)
    n_tok = q_ref.shape[1]
    rows = n_tok * N_HEADS
    page_size = k_refs[0].shape[1]
    lane = lax.broadcasted_iota(I32, (rows, LANES), 1)
    row = lax.broadcasted_iota(I32, (rows, LANES), 0)
    row_head = row // n_tok
    row_tok = row - row_head * n_tok
    cq = cq_ref[0]

    def by_head(x):
        return jnp.concatenate(
            [x[:, h * HEAD_DIM:(h + 1) * HEAD_DIM] for h in range(N_HEADS)], axis=0)

    def attend(u, v_bf):
        m_old = m_scr[...]
        m_new = jnp.maximum(m_old, jnp.max(u, axis=1, keepdims=True) + cq)
        alpha = jnp.exp(m_old - m_new)
        pr = jnp.exp(u - (m_new - cq))
        l_scr[...] = alpha * l_scr[...] + jnp.sum(pr, axis=1, keepdims=True)
        acc_scr[...] = alpha * acc_scr[...] + jnp.dot(pr.astype(BF16), v_bf,
                                                      preferred_element_type=F32)
        m_scr[...] = m_new

    @pl.when(step == 0)
    def _():
        qm = by_head(q_ref[0].astype(F32)).astype(BF16)
        qm_scr[...] = qm
        m_scr[...] = jnp.full_like(m_scr, -jnp.inf)
        l_scr[...] = jnp.zeros_like(l_scr)
        acc_scr[...] = jnp.zeros_like(acc_scr)
        r_scr[...] = jnp.zeros_like(r_scr)
        pad = jnp.zeros((LANES - rows, HEAD_DIM), F32)
        kn = jnp.concatenate([by_head(kn_ref[0].astype(F32)), pad], axis=0).astype(BF16)
        vn = jnp.concatenate([by_head(vn_ref[0].astype(F32)), pad], axis=0).astype(BF16)
        s = lax.dot_general(qm, kn, NT_DIMS, preferred_element_type=F32)
        key_head = lane // n_tok
        key_tok = lane - key_head * n_tok
        u = jnp.where(key_head == row_head, jnp.where(key_tok <= row_tok, s - ck_ref[0], -jnp.inf),
                      -jnp.inf)
        attend(u, vn)

    own_head = (lane & (N_HEADS - 1)) == row_head
    qm = qm_scr[...]
    after_sum = r_scr[...]
    u_parts, v_parts = [], []
    for pg in range(pages):
        later, total = _page_suffix(lf_refs[pg][0])
        after = after_sum + later
        after_sum = after_sum + total
        k2 = k_refs[pg][0].reshape(page_size * N_HEADS, HEAD_DIM).astype(BF16)
        v_parts.append(v_refs[pg][0].reshape(page_size * N_HEADS, HEAD_DIM).astype(BF16))
        s = lax.dot_general(qm, k2, NT_DIMS, preferred_element_type=F32)
        u_parts += [jnp.where(own_head, s[:, c * LANES:(c + 1) * LANES] + after[c:c + 1], -jnp.inf)
                    for c in range(page_size * N_HEADS // LANES)]
    r_scr[...] = after_sum
    attend(jnp.concatenate(u_parts, axis=1), jnp.concatenate(v_parts, axis=0))

    @pl.when(step == pl.num_programs(1) - 1)
    def _():
        o = acc_scr[...] / l_scr[...]
        o_ref[0] = jnp.concatenate(
            [o[h * n_tok:(h + 1) * n_tok] for h in range(N_HEADS)], axis=1).astype(BF16)


def _fox_decode(q, k_new, v_new, cq, ck, cache_k, cache_v, cache_lf, page_table):
    bsz, n_tok, d = q.shape
    n_pages = page_table.shape[1]
    page_size = cache_k.shape[1]
    pages = PAGES_PER_STEP
    rows = n_tok * N_HEADS

    def page_map(pg, ndim):
        def index_map(b, s, pt):
            return (pt[b, n_pages - 1 - (s * pages + pg)],) + (0,) * (ndim - 1)
        return index_map

    seq = lambda b, s, pt: (b, 0, 0)
    in_specs = [pl.BlockSpec((1, n_tok, d), seq), pl.BlockSpec((1, n_tok, d), seq),
                pl.BlockSpec((1, n_tok, d), seq), pl.BlockSpec((1, rows, 1), seq),
                pl.BlockSpec((1, 1, LANES), seq)]
    kv_block = (1, page_size, N_HEADS, HEAD_DIM)
    in_specs += [pl.BlockSpec(kv_block, page_map(pg, 4)) for pg in range(pages)]
    in_specs += [pl.BlockSpec(kv_block, page_map(pg, 4)) for pg in range(pages)]
    in_specs += [pl.BlockSpec((1, SUBLANES, LANES), page_map(pg, 3)) for pg in range(pages)]
    grid_spec = pltpu.PrefetchScalarGridSpec(
        num_scalar_prefetch=1,
        grid=(bsz, n_pages // pages),
        in_specs=in_specs,
        out_specs=pl.BlockSpec((1, n_tok, d), seq),
        scratch_shapes=[pltpu.VMEM((rows, HEAD_DIM), BF16), pltpu.VMEM((rows, 1), F32),
                        pltpu.VMEM((rows, 1), F32), pltpu.VMEM((rows, HEAD_DIM), F32),
                        pltpu.VMEM((1, LANES), F32)],
    )
    return pl.pallas_call(
        functools.partial(_fox_decode_kernel, pages=pages),
        grid_spec=grid_spec,
        out_shape=jax.ShapeDtypeStruct((bsz, n_tok, d), BF16),
        compiler_params=_params("arbitrary", "arbitrary"),
        name="fox_decode_attention",
    )(page_table, q, k_new, v_new, cq, ck, *([cache_k] * pages), *([cache_v] * pages),
      *([cache_lf] * pages))


def _oproj_kernel(x_ref, o_ref, g1_ref, wo_ref, xo_ref):
    xo_ref[0] = x_ref[0] + g1_ref[0] * jnp.dot(o_ref[0], wo_ref[...], preferred_element_type=F32)


def _oproj(x, o, g1, w_o, *, tm):
    bsz, t_len, d = x.shape
    tok = pl.BlockSpec((1, tm, d), lambda b, t: (b, t, 0))
    return pl.pallas_call(
        _oproj_kernel,
        grid=(bsz, t_len // tm),
        in_specs=[tok, tok, _mod_spec(g1, tm), _full_spec(w_o)],
        out_specs=tok,
        out_shape=jax.ShapeDtypeStruct((bsz, t_len, d), F32),
        compiler_params=_params("arbitrary", "arbitrary"),
        name="attn_out_proj",
    )(x, o, g1, w_o)


def _final_kernel(x_ref, y_ref, g2_ref, g_ref, o_ref):
    x = x_ref[0] + g2_ref[0] * _from_token_tiles(y_ref, (), x_ref.shape[1])
    o_ref[0] = (x * lax.rsqrt(jnp.mean(x * x, axis=-1, keepdims=True) + EPS)) * g_ref[...]


def _final(x, y, g2, g_final, *, tm):
    bsz, t_len, d = x.shape
    tok = pl.BlockSpec((1, tm, d), lambda b, t: (b, t, 0))
    return pl.pallas_call(
        _final_kernel,
        grid=(bsz, t_len // tm),
        in_specs=[tok, _row_spec(t_len // tm, tm), _mod_spec(g2, tm), _full_spec(g_final)],
        out_specs=tok,
        out_shape=jax.ShapeDtypeStruct((bsz, t_len, d), F32),
        compiler_params=_params("arbitrary", "arbitrary"),
        name="final_norm",
    )(x, y, g2, g_final)


def _trunk(x, mods0, mods1, kvmods, prev, h0, w, attend, *, tm, tmx, seg, carry):
    sh1, sc1, g1, sh2, sc2, g2 = mods0
    x, h_out, conv_out = _lru_layer(x, sh1, sc1, g1, w["lru"], prev, h0, tm=tm, seg=seg, carry=carry)
    y = _moe(x, sh2, sc2, w["moe0"], tm=tm, tmx=tmx)
    sh1b, sc1b, g1b, sh2b, sc2b, g2b = mods1
    shk, sck = kvmods
    x, k, v, lft, fct, *operands = _post0(x, y, g2, shk, sck, sh1b, sc1b, w["post0"],
                                          tm=tm, seg=seg, carry=carry, prompt_layout=carry)
    o = attend(fct, *operands)
    x = _oproj(x, o, g1b, w["w_o"], tm=tm)
    y = _moe(x, sh2b, sc2b, w["moe1"], tm=tm, tmx=tmx)
    y_out = _final(x, y, g2b, w["g_final"], tm=tm)
    return y_out, h_out, conv_out, k, v, lft


def kernel(x_prompt, x_sample, state_h, state_conv, cache_k, cache_v, cache_logf, page_table,
           c_prompt, c_sample, w_mod, b_mod, g_mix, g_moe, lru_w_in, lru_conv_w, lru_conv_b,
           lru_w_a, lru_b_a, lru_w_x, lru_b_x, lru_lambda, lru_w_out, kv_g, kv_w_mod, kv_b_mod,
           kv_w, kv_b_f, attn_w_q, attn_w_o, moe_w_grp, moe_b_grp, moe_w_exp, moe_b_exp,
           moe_w_up, moe_w_down, g_final):
    d = D_MODEL
    bp, tp, _ = x_prompt.shape
    bs, ts, _ = x_sample.shape
    hd = N_HEADS * HEAD_DIM
    row = lambda a: a.reshape(1, -1)

    def moe_weights(layer):
        wr_t = jnp.concatenate(
            [moe_w_grp[layer].T, moe_w_exp[layer].T,
             jnp.zeros((ROUTER_ROWS - N_GROUPS - N_EXPERTS, d), F32)], axis=0)
        br = jnp.concatenate(
            [moe_b_grp[layer], moe_b_exp[layer],
             jnp.zeros((ROUTER_ROWS - N_GROUPS - N_EXPERTS,), F32)]).reshape(ROUTER_ROWS, 1)
        return dict(g_moe=row(g_moe[layer]), wr_t=wr_t, br=br,
                    w_up=moe_w_up[layer].astype(BF16), w_down=moe_w_down[layer].astype(BF16))

    w = dict(
        lru=dict(g_mix=row(g_mix[0]), w_in=lru_w_in[0].astype(BF16), conv_w=lru_conv_w[0],
                 conv_b=row(lru_conv_b[0]), w_a=lru_w_a[0].astype(BF16), b_a=row(lru_b_a[0]),
                 w_x=lru_w_x[0].astype(BF16), b_x=row(lru_b_x[0]), lam=row(lru_lambda[0]),
                 w_out=lru_w_out[0].astype(BF16)),
        moe0=moe_weights(0),
        moe1=moe_weights(1),
        post0=dict(kv_g=row(kv_g), w_k=kv_w[:, :hd].astype(BF16),
                   w_v=kv_w[:, hd:2 * hd].astype(BF16), w_f_t=kv_w[:, 2 * hd:].T,
                   w_v_t=kv_w[:, hd:2 * hd].T.astype(BF16), b_f=kv_b_f.reshape(N_HEADS, 1),
                   g_mix1=row(g_mix[1]), w_q=attn_w_q[0].astype(BF16),
                   w_q_t=attn_w_q[0].T.astype(BF16)),
        w_o=attn_w_o[0].astype(BF16),
        g_final=row(g_final),
    )

    n_c = bp + bs
    c_rows = -(-n_c // SUBLANES) * SUBLANES
    c_all = jnp.concatenate([c_prompt, c_sample, jnp.zeros((c_rows - n_c, d), F32)], axis=0)
    mod_l0 = _mm_bias(c_all, w_mod[0], b_mod[0])
    mod_l1 = _mm_bias(c_all, w_mod[1], b_mod[1])
    mod_kv = _mm_bias(c_all, kv_w_mod, kv_b_mod)

    def prompt_mods(m, parts):
        return [a.reshape(bp, 1, d) for a in jnp.split(m[:bp], parts, axis=-1)]

    def sample_mods(m, parts):
        return [jnp.repeat(a, ts, axis=0).reshape(1, bs * ts, d)
                for a in jnp.split(m[bp:n_c], parts, axis=-1)]

    prev_p = jnp.zeros((bp, SUBLANES, D_RNN), F32)
    h0_p = jnp.zeros((bp, 1, D_RNN), F32)
    attend_p = lambda fct, qa, ka, vt: _fox_prompt(qa, ka, vt)
    y_p, h_p, conv_p, k_p, v_p, lft_p = _trunk(
        x_prompt, prompt_mods(mod_l0, 6), prompt_mods(mod_l1, 6), prompt_mods(mod_kv, 2),
        prev_p, h0_p, w, attend_p, tm=TOKEN_TILE, tmx=EXPERT_TILE, seg=TOKEN_TILE, carry=True)

    n_s = bs * ts
    prev_s = jnp.pad(state_conv[0], ((0, 0), (ts - (CONV_WIDTH - 1), 0), (0, 0))).reshape(1, n_s, D_RNN)
    h0_s = jnp.repeat(state_h[0], ts, axis=0).reshape(1, n_s, D_RNN)
    n_pool, page_size = cache_k.shape[0], cache_k.shape[1]
    cache_lf = cache_logf.reshape(n_pool, SUBLANES, page_size * N_HEADS // SUBLANES)

    def attend_s(fct, q, kb, vb):
        c_new = jnp.transpose(fct.reshape(N_HEADS, bs, ts), (1, 0, 2)).reshape(bs, N_HEADS * ts)
        cq = c_new.reshape(bs, N_HEADS * ts, 1)
        ck = jnp.pad(c_new, ((0, 0), (0, LANES - N_HEADS * ts))).reshape(bs, 1, LANES)
        o = _fox_decode(q.reshape(bs, ts, d), kb.reshape(bs, ts, d), vb.reshape(bs, ts, d), cq, ck,
                        cache_k, cache_v, cache_lf, page_table)
        return o.reshape(1, n_s, d)

    y_s, h_s, conv_s, k_s, v_s, lft_s = _trunk(
        x_sample.reshape(1, n_s, d), sample_mods(mod_l0, 6), sample_mods(mod_l1, 6),
        sample_mods(mod_kv, 2), prev_s, h0_s, w, attend_s, tm=n_s, tmx=SAMPLE_EXPERT_TILE, seg=ts,
        carry=False)

    n_conv = CONV_WIDTH - 1
    return (
        y_p,
        y_s.reshape(bs, ts, d),
        h_p.reshape(1, bp, D_RNN),
        conv_p[:, SUBLANES - n_conv:].reshape(1, bp, n_conv, D_RNN),
        k_p.reshape(bp, tp, N_HEADS, HEAD_DIM),
        v_p.reshape(bp, tp, N_HEADS, HEAD_DIM),
        jnp.transpose(lft_p, (0, 2, 1)),
        h_s.reshape(bs, ts, D_RNN)[:, ts - 1].reshape(1, bs, D_RNN),
        conv_s.reshape(bs, ts, D_RNN)[:, ts - n_conv:].reshape(1, bs, n_conv, D_RNN),
        k_s.reshape(bs, ts, N_HEADS, HEAD_DIM),
        v_s.reshape(bs, ts, N_HEADS, HEAD_DIM),
        jnp.transpose(lft_s.reshape(N_HEADS, bs, ts), (1, 2, 0)),
    )
```

```python
import functools
import math

import jax
import jax.numpy as jnp
from jax import lax
from jax.experimental import pallas as pl
from jax.experimental.pallas import tpu as pltpu

F32 = jnp.float32
BF16 = jnp.bfloat16
I32 = jnp.int32

D_MODEL = 1024
D_RNN = D_MODEL
N_LRU_BLOCKS = 8
LRU_BLOCK = D_RNN // N_LRU_BLOCKS
CONV_WIDTH = 4
LRU_C = 8.0
N_HEADS = 8
HEAD_DIM = D_MODEL // N_HEADS
N_GROUPS = 4
EXPERTS_PER_GROUP = 4
N_EXPERTS = N_GROUPS * EXPERTS_PER_GROUP
D_EXPERT = D_MODEL // 2
EPS = 1e-6
N_PAIRS = 6
N_CLASSES = N_GROUPS * N_PAIRS
PAIR_LO = (0, 0, 0, 1, 1, 2)
PAIR_HI = (1, 2, 3, 2, 3, 3)

SUBLANES = 8
LANES = 128
VMEM_LIMIT = 48 * 1024 * 1024
DMA_THREADS = 2

TOKEN_TILE = 256
EXPERT_TILE = 256
SAMPLE_EXPERT_TILE = 32
ROUTER_ROWS = 32
ATTN_TILE = 512
PAGES_PER_STEP = 16
MOD_TILE = 512

HIGHEST = lax.Precision.HIGHEST
NT_DIMS = (((1,), (1,)), ((), ()))


def _params(*semantics):
    return pltpu.CompilerParams(dimension_semantics=semantics, vmem_limit_bytes=VMEM_LIMIT)


def _norm_mod(x, g, shift, scale):
    y = x * lax.rsqrt(jnp.mean(x * x, axis=-1, keepdims=True) + EPS)
    return (y * g) * (1.0 + scale) + shift


def _log_sigmoid(x):
    return jnp.minimum(x, 0.0) - jnp.log1p(jnp.exp(-jnp.abs(x)))


def _sigmoid(x):
    return 1.0 / (1.0 + jnp.exp(-x))


def _gelu_tanh(x):
    c = math.sqrt(2.0 / math.pi)
    return x * (0.5 * (1.0 + jnp.tanh(c * (x + 0.044715 * (x * x * x)))))


def _mod_spec(arr, tm):
    if arr.shape[1] == 1:
        return pl.BlockSpec((1, 1, arr.shape[2]), lambda b, t: (b, 0, 0))
    return pl.BlockSpec((1, tm, arr.shape[2]), lambda b, t: (b, t, 0))


ROW_TILES = D_MODEL // LANES


def _row_spec(nt, tm):
    return pl.BlockSpec((tm * ROW_TILES, LANES), lambda b, t: (b * nt + t, 0))


def _to_token_tiles(ref, lead, x):
    rows = x.shape[0]
    for c in range(ROW_TILES):
        ref[(*lead, pl.ds(c, rows, stride=ROW_TILES), slice(None))] = x[:, c * LANES:(c + 1) * LANES]


def _from_token_tiles(ref, lead, rows):
    return jnp.concatenate(
        [ref[(*lead, pl.ds(c, rows, stride=ROW_TILES), slice(None))] for c in range(ROW_TILES)],
        axis=1)


def _start_token_gather(src_hbm, idx_ref, buf, slot, sem, count):
    for r in range(count):
        rows = pl.ds(pl.multiple_of(idx_ref[0, 0, r], ROW_TILES), ROW_TILES)
        pltpu.make_async_copy(src_hbm.at[rows], buf.at[slot, pl.ds(r * ROW_TILES, ROW_TILES)],
                              sem.at[slot]).start(priority=r % DMA_THREADS)


def _wait_token_gather(src_hbm, buf, slot, sem, count):
    pltpu.make_async_copy(src_hbm.at[pl.ds(0, count * ROW_TILES)], buf.at[slot], sem.at[slot]).wait()


def _gathered_tile(step, is_last, src_hbm, idx0_ref, idxn_ref, buf, sem, count):
    slot = lax.rem(step, 2)

    @pl.when(step == 0)
    def _():
        _start_token_gather(src_hbm, idx0_ref, buf, 0, sem, count)

    _wait_token_gather(src_hbm, buf, slot, sem, count)
    _start_token_gather(src_hbm, idxn_ref, buf, 1 - slot, sem, count)
    x = _from_token_tiles(buf, (slot,), count)

    @pl.when(is_last)
    def _():
        _wait_token_gather(src_hbm, buf, 1 - slot, sem, count)

    return x


def _gather_in_specs(n_steps, count, step_of):
    block = (1, 1, count)
    return [pl.BlockSpec(memory_space=pl.ANY),
            pl.BlockSpec(block, lambda *g: (0, 0, 0), memory_space=pltpu.SMEM),
            pl.BlockSpec(block, lambda *g: (jnp.minimum(step_of(*g) + 1, n_steps - 1), 0, 0),
                         memory_space=pltpu.SMEM)]


def _gather_scratch(count):
    return [pltpu.VMEM((2, count * ROW_TILES, LANES), F32), pltpu.SemaphoreType.DMA((2,))]


def _full_spec(arr):
    zeros = (0,) * arr.ndim
    return pl.BlockSpec(arr.shape, lambda *_: zeros)


def _mm_bias_kernel(x_ref, w_ref, b_ref, o_ref):
    o_ref[...] = jnp.dot(x_ref[...].astype(BF16), w_ref[...].astype(BF16),
                         preferred_element_type=F32) + b_ref[...]


def _mm_bias(x, w, b):
    m, d = x.shape
    n = w.shape[1]
    return pl.pallas_call(
        _mm_bias_kernel,
        grid=(n // MOD_TILE,),
        in_specs=[pl.BlockSpec((m, d), lambda j: (0, 0)),
                  pl.BlockSpec((d, MOD_TILE), lambda j: (0, j)),
                  pl.BlockSpec((1, MOD_TILE), lambda j: (0, j))],
        out_specs=pl.BlockSpec((m, MOD_TILE), lambda j: (0, j)),
        out_shape=jax.ShapeDtypeStruct((m, n), F32),
        compiler_params=_params("arbitrary"),
        name="mod_vectors",
    )(x, w, b.reshape(1, n))


def _lru_kernel(x_ref, sh_ref, sc_ref, gt_ref, gmix_ref, win_ref, cw_ref, cb_ref, wa_ref, ba_ref,
                wx_ref, bx_ref, lam_ref, wout_ref, prev_ref, h0_ref,
                xo_ref, hs_ref, xb_ref, *scratch, tm, seg, carry):
    x = x_ref[0]
    hn = _norm_mod(x, gmix_ref[...], sh_ref[0], sc_ref[0])
    xg = jnp.dot(hn.astype(BF16), win_ref[...], preferred_element_type=F32)
    xb = xg[:, :D_RNN]
    gb = xg[:, D_RNN:]

    if carry:
        prev_scr, h_scr = scratch

        @pl.when(pl.program_id(1) == 0)
        def _():
            prev_scr[...] = prev_ref[0]
            h_scr[...] = h0_ref[0]

        prev = prev_scr[...]
        hprev = h_scr[...]
    else:
        prev = prev_ref[0]
        hprev = h0_ref[0]

    row = lax.broadcasted_iota(I32, (tm, 1), 0)
    rseg = row & (seg - 1)
    nprev = prev.shape[0]

    cw = cw_ref[...]
    xc = cb_ref[...] + cw[CONV_WIDTH - 1:CONV_WIDTH] * xb
    for d in range(1, CONV_WIDTH):
        rolled = pltpu.roll(xb, d, axis=0)
        rp = pltpu.roll(prev, nprev - SUBLANES + d, axis=0)
        if carry:
            head = jnp.where(row[:SUBLANES] < d, rp, rolled[:SUBLANES])
            shifted = head if tm == SUBLANES else jnp.concatenate([head, rolled[SUBLANES:]], axis=0)
        else:
            shifted = jnp.where(rseg < d, rp, rolled)
        xc = xc + cw[CONV_WIDTH - 1 - d:CONV_WIDTH - d] * shifted

    xcb = xc.astype(BF16)
    ra = jnp.concatenate(
        [jnp.dot(xcb[:, n * LRU_BLOCK:(n + 1) * LRU_BLOCK], wa_ref[n], preferred_element_type=F32)
         for n in range(N_LRU_BLOCKS)], axis=1)
    rx = jnp.concatenate(
        [jnp.dot(xcb[:, n * LRU_BLOCK:(n + 1) * LRU_BLOCK], wx_ref[n], preferred_element_type=F32)
         for n in range(N_LRU_BLOCKS)], axis=1)
    r = _sigmoid(ra + ba_ref[...])
    ig = _sigmoid(rx + bx_ref[...])
    log_a = (LRU_C * r) * _log_sigmoid(lam_ref[...])
    a = jnp.exp(log_a)
    bt = jnp.sqrt(-jnp.tanh(log_a) * (a * a + 1.0)) * (ig * xc)

    s = 1
    while s < seg:
        inside = rseg >= s
        a_sh = jnp.where(inside, pltpu.roll(a, s, axis=0), 1.0)
        b_sh = jnp.where(inside, pltpu.roll(bt, s, axis=0), 0.0)
        bt = a * b_sh + bt
        a = a * a_sh
        s *= 2
    hs = bt + a * hprev

    if carry:
        h_scr[...] = hs[tm - 1:tm]
        prev_scr[...] = xb[tm - SUBLANES:]
        hs_ref[0] = hs[tm - 1:tm]
        xb_ref[0] = xb[tm - SUBLANES:]
    else:
        hs_ref[0] = hs
        xb_ref[0] = xb

    y = (hs * _gelu_tanh(gb)).astype(BF16)
    out = jnp.dot(y, wout_ref[...], preferred_element_type=F32)
    xo_ref[0] = x + gt_ref[0] * out


def _lru_layer(x, sh, sc, gt, p, prev, h0, *, tm, seg, carry):
    bsz, t_len, d = x.shape
    r = D_RNN
    nt = t_len // tm
    tok = pl.BlockSpec((1, tm, d), lambda b, t: (b, t, 0))
    if carry:
        state_specs = [pl.BlockSpec((1, SUBLANES, r), lambda b, t: (b, 0, 0)),
                       pl.BlockSpec((1, 1, r), lambda b, t: (b, 0, 0))]
        out_specs = [tok,
                     pl.BlockSpec((1, 1, r), lambda b, t: (b, 0, 0)),
                     pl.BlockSpec((1, SUBLANES, r), lambda b, t: (b, 0, 0))]
        out_shape = [jax.ShapeDtypeStruct((bsz, t_len, d), F32),
                     jax.ShapeDtypeStruct((bsz, 1, r), F32),
                     jax.ShapeDtypeStruct((bsz, SUBLANES, r), F32)]
        scratch = [pltpu.VMEM((SUBLANES, r), F32), pltpu.VMEM((1, r), F32)]
    else:
        rtok = pl.BlockSpec((1, tm, r), lambda b, t: (b, t, 0))
        state_specs = [rtok, rtok]
        out_specs = [tok, rtok, rtok]
        out_shape = [jax.ShapeDtypeStruct((bsz, t_len, d), F32),
                     jax.ShapeDtypeStruct((bsz, t_len, r), F32),
                     jax.ShapeDtypeStruct((bsz, t_len, r), F32)]
        scratch = []
    weights = [p["g_mix"], p["w_in"], p["conv_w"], p["conv_b"], p["w_a"], p["b_a"], p["w_x"],
               p["b_x"], p["lam"], p["w_out"]]
    return pl.pallas_call(
        functools.partial(_lru_kernel, tm=tm, seg=seg, carry=carry),
        grid=(bsz, nt),
        in_specs=[tok, _mod_spec(sh, tm), _mod_spec(sc, tm), _mod_spec(gt, tm)]
                 + [_full_spec(w) for w in weights] + state_specs,
        out_specs=out_specs,
        out_shape=out_shape,
        scratch_shapes=scratch,
        compiler_params=_params("arbitrary", "arbitrary"),
        name="rglru_layer",
    )(x, sh, sc, gt, *weights, prev, h0)


def _router_kernel(x_ref, sh_ref, sc_ref, g_ref, wr_ref, br_ref,
                   hn_ref, cls_ref, rank_ref, glo_ref, ghi_ref, cnt_ref, carry_scr, *, tm):
    @pl.when((pl.program_id(0) == 0) & (pl.program_id(1) == 0))
    def _():
        carry_scr[...] = jnp.zeros_like(carry_scr)

    hn = _norm_mod(x_ref[0], g_ref[...], sh_ref[0], sc_ref[0])
    _to_token_tiles(hn_ref, (), hn)
    lt = lax.dot_general(wr_ref[...], hn, NT_DIMS, precision=HIGHEST,
                         preferred_element_type=F32) + br_ref[...]

    gl = [lt[k:k + 1] for k in range(N_GROUPS)]
    gmax = jnp.maximum(jnp.maximum(gl[0], gl[1]), jnp.maximum(gl[2], gl[3]))
    g_sel = jnp.where(gl[0] >= gmax, 0, jnp.where(gl[1] >= gmax, 1, jnp.where(gl[2] >= gmax, 2, 3)))
    p_grp = 1.0 / (jnp.exp(gl[0] - gmax) + jnp.exp(gl[1] - gmax)
                   + jnp.exp(gl[2] - gmax) + jnp.exp(gl[3] - gmax))

    def expert_logit(e):
        rows = [lt[N_GROUPS + g * EXPERTS_PER_GROUP + e:N_GROUPS + g * EXPERTS_PER_GROUP + e + 1]
                for g in range(N_GROUPS)]
        return jnp.where(g_sel == 0, rows[0],
                         jnp.where(g_sel == 1, rows[1], jnp.where(g_sel == 2, rows[2], rows[3])))

    es = [expert_logit(e) for e in range(EXPERTS_PER_GROUP)]

    def first_argmax(vals):
        vmax = jnp.maximum(jnp.maximum(vals[0], vals[1]), jnp.maximum(vals[2], vals[3]))
        idx = jnp.where(vals[0] >= vmax, 0,
                        jnp.where(vals[1] >= vmax, 1, jnp.where(vals[2] >= vmax, 2, 3)))
        return vmax, idx

    v1, i1 = first_argmax(es)
    rest = [jnp.where(i1 == e, -jnp.inf, es[e]) for e in range(EXPERTS_PER_GROUP)]
    v2, i2 = first_argmax(rest)
    e21 = jnp.exp(v2 - v1)
    w1 = (1.0 / (1.0 + e21)) * p_grp
    w2 = (e21 / (1.0 + e21)) * p_grp
    first_lower = i1 < i2
    lo = jnp.where(first_lower, i1, i2)
    hi = jnp.where(first_lower, i2, i1)
    glo_ref[0] = jnp.where(first_lower, w1, w2)
    ghi_ref[0] = jnp.where(first_lower, w2, w1)
    pair = jnp.where(lo == 0, hi - 1, jnp.where(lo == 1, hi + 1, 5))
    cls = g_sel * N_PAIRS + pair
    cls_ref[0] = cls

    crow = lax.broadcasted_iota(I32, (ROUTER_ROWS, tm), 0)
    onehot = jnp.where(crow == cls, 1.0, 0.0)
    ki = lax.broadcasted_iota(I32, (tm, tm), 0)
    kj = lax.broadcasted_iota(I32, (tm, tm), 1)
    upper = jnp.where(ki <= kj, 1.0, 0.0).astype(BF16)
    cum = jnp.dot(onehot.astype(BF16), upper, preferred_element_type=F32)
    carry = carry_scr[...]
    rank = jnp.sum(onehot * (cum - 1.0 + carry), axis=0, keepdims=True)
    rank_ref[0] = rank.astype(I32)
    carry = carry + cum[:, tm - 1:tm]
    carry_scr[...] = carry
    cnt_ref[...] = carry


def _router(x, sh, sc, g, wr_t, br, *, tm):
    bsz, t_len, d = x.shape
    nt = t_len // tm
    n_tiles = bsz * nt
    tok = pl.BlockSpec((1, tm, d), lambda b, t: (b, t, 0))
    lane_row = pl.BlockSpec((1, 1, tm), lambda b, t: (b * nt + t, 0, 0))
    row_shape = jax.ShapeDtypeStruct((n_tiles, 1, tm), F32)
    row_shape_i = jax.ShapeDtypeStruct((n_tiles, 1, tm), I32)
    return pl.pallas_call(
        functools.partial(_router_kernel, tm=tm),
        grid=(bsz, nt),
        in_specs=[tok, _mod_spec(sh, tm), _mod_spec(sc, tm), _full_spec(g), _full_spec(wr_t),
                  _full_spec(br)],
        out_specs=[_row_spec(nt, tm), lane_row, lane_row, lane_row, lane_row,
                   pl.BlockSpec((ROUTER_ROWS, 1), lambda b, t: (0, 0))],
        out_shape=[jax.ShapeDtypeStruct((bsz * t_len * ROW_TILES, LANES), F32), row_shape_i, row_shape_i,
                   row_shape, row_shape, jax.ShapeDtypeStruct((ROUTER_ROWS, 1), F32)],
        scratch_shapes=[pltpu.VMEM((ROUTER_ROWS, 1), F32)],
        compiler_params=_params("arbitrary", "arbitrary"),
        name="moe_router",
    )(x, sh, sc, g, wr_t, br)


def _expert_kernel(e1_ref, e2_ref, valid_ref, hn_hbm, src0_ref, srcn_ref, glo_ref, ghi_ref,
                   wu1_ref, wd1_ref, wu2_ref, wd2_ref, o_ref, xbuf, sem_g, *, tmx):
    del e1_ref, e2_ref
    j = pl.program_id(0)
    last = pl.num_programs(0) - 1

    @pl.when(valid_ref[j] == 0)
    def _():
        o_ref[...] = jnp.zeros_like(o_ref)

    @pl.when(valid_ref[j] == 1)
    def _():
        is_last = (j == last) | (valid_ref[jnp.minimum(j + 1, last)] == 0)
        x = _gathered_tile(j, is_last, hn_hbm, src0_ref, srcn_ref, xbuf, sem_g, tmx).astype(BF16)
        ri = lax.broadcasted_iota(I32, (tmx, tmx), 0)
        ci = lax.broadcasted_iota(I32, (tmx, tmx), 1)

        def column(row_ref):
            return jnp.sum(jnp.where(ri == ci, row_ref[0], 0.0), axis=1, keepdims=True)

        def ffn(wu_ref, wd_ref, gate):
            gu = jnp.dot(x, wu_ref[0], preferred_element_type=F32)
            g_ = gu[:, :D_EXPERT]
            u_ = gu[:, D_EXPERT:]
            act = ((g_ * _sigmoid(g_)) * u_).astype(BF16)
            return gate * jnp.dot(act, wd_ref[0], preferred_element_type=F32)

        _to_token_tiles(o_ref, (), ffn(wu1_ref, wd1_ref, column(glo_ref))
                        + ffn(wu2_ref, wd2_ref, column(ghi_ref)))


def _experts(hn, src, glo_s, ghi_s, w_up, w_down, e1, e2, valid, n_tiles, tmx):
    d = D_MODEL
    idx_block = (1, 1, tmx)
    at_tile = lambda j, e1, e2, v: (j, 0, 0)
    grid_spec = pltpu.PrefetchScalarGridSpec(
        num_scalar_prefetch=3,
        grid=(n_tiles,),
        in_specs=_gather_in_specs(n_tiles, tmx, lambda j, *_: j) + [
            pl.BlockSpec(idx_block, at_tile),
            pl.BlockSpec(idx_block, at_tile),
            pl.BlockSpec((1, d, 2 * D_EXPERT), lambda j, e1, e2, v: (e1[j], 0, 0)),
            pl.BlockSpec((1, D_EXPERT, d), lambda j, e1, e2, v: (e1[j], 0, 0)),
            pl.BlockSpec((1, d, 2 * D_EXPERT), lambda j, e1, e2, v: (e2[j], 0, 0)),
            pl.BlockSpec((1, D_EXPERT, d), lambda j, e1, e2, v: (e2[j], 0, 0)),
        ],
        out_specs=pl.BlockSpec((tmx * ROW_TILES, LANES), lambda j, e1, e2, v: (j, 0)),
        scratch_shapes=_gather_scratch(tmx),
    )
    return pl.pallas_call(
        functools.partial(_expert_kernel, tmx=tmx),
        grid_spec=grid_spec,
        out_shape=jax.ShapeDtypeStruct((n_tiles * tmx * ROW_TILES, LANES), F32),
        compiler_params=_params("arbitrary"),
        name="moe_experts",
    )(e1, e2, valid, hn, src, src, glo_s, ghi_s, w_up, w_down, w_up, w_down)


def _moe(x, sh, sc, p, *, tm, tmx):
    bsz, t_len, d = x.shape
    n = bsz * t_len
    hn, cls, rank, glo, ghi, cnt = _router(x, sh, sc, p["g_moe"], p["wr_t"], p["br"], tm=tm)
    cls = cls.reshape(n)
    rank = rank.reshape(n)
    counts = cnt[:N_CLASSES, 0].astype(I32)
    padded = ((counts + tmx - 1) // tmx) * tmx
    ends = jnp.cumsum(padded)
    offs = ends - padded
    pos = offs[cls] + rank
    n_tiles = n // tmx + N_CLASSES
    n_rows = n_tiles * tmx
    total = ends[N_CLASSES - 1]
    tile_start = jnp.arange(n_tiles, dtype=I32) * tmx
    valid = (tile_start < total).astype(I32)
    last_tile = jnp.maximum(total // tmx - 1, 0)
    tile_cls = jnp.minimum(
        jnp.sum((ends[None, :] <= jnp.minimum(tile_start, last_tile * tmx)[:, None]).astype(I32),
                axis=1), N_CLASSES - 1)
    grp = tile_cls // N_PAIRS
    pair = tile_cls % N_PAIRS
    e1 = grp * EXPERTS_PER_GROUP + jnp.asarray(PAIR_LO, I32)[pair]
    e2 = grp * EXPERTS_PER_GROUP + jnp.asarray(PAIR_HI, I32)[pair]
    token_info = jnp.stack([jnp.arange(n, dtype=I32).astype(F32), glo.reshape(n), ghi.reshape(n)],
                           axis=1)
    row_info = jnp.zeros((n_rows, 3), F32).at[pos].set(token_info)
    as_tiles = lambda a: a.reshape(n_tiles, 1, tmx)
    src_rows = as_tiles(row_info[:, 0].astype(I32) * ROW_TILES)
    ys = _experts(hn, src_rows, as_tiles(row_info[:, 1]), as_tiles(row_info[:, 2]),
                  p["w_up"], p["w_down"], e1, e2, valid, n_tiles, tmx)
    return ys, (pos * ROW_TILES).reshape(n // tm, 1, tm)


AUG = 2 * HEAD_DIM
N_PIECES = 3
VT_ROWS = HEAD_DIM + 16
LOG2E = 1.0 / math.log(2.0)


def _bf16_pieces(x):
    pieces = []
    rest = x
    for _ in range(N_PIECES):
        piece = rest.astype(BF16).astype(F32)
        pieces.append(piece)
        rest = rest - piece
    return pieces


def _grid_step():
    n_steps = pl.num_programs(0) * pl.num_programs(1)
    return pl.program_id(0) * pl.num_programs(1) + pl.program_id(1), n_steps


def _post0_kernel(x_ref, y_hbm, pos0_ref, posn_ref, g2_ref, shk_ref, sck_ref, kvg_ref, wk_ref, wv_ref,
                  wft_ref, bf_ref, shq_ref, scq_ref, gq_ref, wq_ref, *refs,
                  tm, seg, carry, prompt_layout):
    if prompt_layout:
        wvt_ref, x1_ref, k_ref, v_ref, lft_ref, fct_ref, qa_ref, ka_ref, vt_ref = refs[:9]
        scratch = refs[9:]
    else:
        x1_ref, k_ref, v_ref, lft_ref, fct_ref, q_ref, kb_ref, vb_ref = refs[:8]
        scratch = refs[8:]
    ybuf, sem_y = scratch[:2]
    scratch = scratch[2:]
    scale = HEAD_DIM ** -0.5
    step, n_steps = _grid_step()
    y = _gathered_tile(step, step == n_steps - 1, y_hbm, pos0_ref, posn_ref, ybuf, sem_y, tm)
    x1 = x_ref[0] + g2_ref[0] * y
    x1_ref[0] = x1
    hk = _norm_mod(x1, kvg_ref[...], shk_ref[0], sck_ref[0])
    hkb = hk.astype(BF16)
    k = jnp.dot(hkb, wk_ref[...], preferred_element_type=F32)
    v = jnp.dot(hkb, wv_ref[...], preferred_element_type=F32)
    k_ref[0] = k
    v_ref[0] = v
    f = lax.dot_general(wft_ref[...], hk, NT_DIMS, precision=HIGHEST,
                        preferred_element_type=F32) + bf_ref[...]
    lf = _log_sigmoid(f)
    lft_ref[0] = lf
    col = lax.broadcasted_iota(I32, (1, tm), 1)
    cseg = col & (seg - 1)
    c = lf
    s = 1
    while s < seg:
        c = c + jnp.where(cseg >= s, pltpu.roll(c, s, axis=1), 0.0)
        s *= 2
    if carry:
        (f_scr,) = scratch

        @pl.when(pl.program_id(1) == 0)
        def _():
            f_scr[...] = jnp.zeros_like(f_scr)

        c = c + f_scr[...]
        f_scr[...] = c[:, tm - 1:tm]
    fct_ref[0] = c
    hqb = _norm_mod(x1, gq_ref[...], shq_ref[0], scq_ref[0]).astype(BF16)

    if not prompt_layout:
        q_ref[0] = (jnp.dot(hqb, wq_ref[...], preferred_element_type=F32) * scale).astype(BF16)
        kb_ref[0] = k.astype(BF16)
        vb_ref[0] = v.astype(BF16)
        return

    qt = lax.dot_general(wq_ref[...], hqb, NT_DIMS, preferred_element_type=F32) * (scale * LOG2E)
    vt = lax.dot_general(wvt_ref[...], hkb, NT_DIMS, preferred_element_type=F32)
    ones_rows = jnp.where(lax.broadcasted_iota(I32, (VT_ROWS - HEAD_DIM, tm), 0) == 0, 1.0, 0.0)
    v_rows = []
    for h in range(N_HEADS):
        v_rows += [vt[h * HEAD_DIM:(h + 1) * HEAD_DIM], ones_rows]
    vt_ref[0] = jnp.concatenate(v_rows, axis=0).astype(BF16)
    hi, mid, lo = _bf16_pieces(c * LOG2E)
    sub = lax.broadcasted_iota(I32, (SUBLANES, tm), 0)
    pad = jnp.zeros((HEAD_DIM - SUBLANES, tm), F32)
    q_rows, k_rows = [], []
    for h in range(N_HEADS):
        hs = slice(h, h + 1)
        eq = jnp.where(sub < 3, -1.0, jnp.where(sub == 3, hi[hs], jnp.where(
            sub == 4, mid[hs], jnp.where(sub == 5, lo[hs], 0.0))))
        ek = jnp.where(sub == 0, hi[hs], jnp.where(sub == 1, mid[hs], jnp.where(
            sub == 2, lo[hs], jnp.where(sub < 6, 1.0, 0.0))))
        q_rows += [qt[h * HEAD_DIM:(h + 1) * HEAD_DIM], eq, pad]
        k_rows += [ek, pad]
    qa_ref[0] = jnp.concatenate(q_rows, axis=0).astype(BF16)
    ekt = jnp.concatenate(k_rows, axis=0).astype(BF16)
    ri = lax.broadcasted_iota(I32, (tm, tm), 0)
    ci = lax.broadcasted_iota(I32, (tm, tm), 1)
    eye = jnp.where(ri == ci, 1.0, 0.0).astype(BF16)
    extra = lax.dot_general(eye, ekt, NT_DIMS, preferred_element_type=F32)
    k_cols = []
    for h in range(N_HEADS):
        hs = slice(h * HEAD_DIM, (h + 1) * HEAD_DIM)
        k_cols += [k[:, hs], extra[:, hs]]
    ka_ref[0] = jnp.concatenate(k_cols, axis=1).astype(BF16)


def _post0(x, y, y_rows, g2, shk, sck, shq, scq, p, *, tm, seg, carry, prompt_layout):
    bsz, t_len, d = x.shape
    nt = t_len // tm
    tok = pl.BlockSpec((1, tm, d), lambda b, t: (b, t, 0))
    headrow = pl.BlockSpec((1, N_HEADS, tm), lambda b, t: (b, 0, t))
    weights_kv = [p["kv_g"], p["w_k"], p["w_v"], p["w_f_t"], p["b_f"]]
    tok_f32 = jax.ShapeDtypeStruct((bsz, t_len, d), F32)
    head_f32 = jax.ShapeDtypeStruct((bsz, N_HEADS, t_len), F32)
    if prompt_layout:
        weights_q = [p["g_mix1"], p["w_q_t"], p["w_v_t"]]
        out_specs = [tok, tok, tok, headrow, headrow,
                     pl.BlockSpec((1, N_HEADS * AUG, tm), lambda b, t: (b, 0, t)),
                     pl.BlockSpec((1, tm, N_HEADS * AUG), lambda b, t: (b, t, 0)),
                     pl.BlockSpec((1, N_HEADS * VT_ROWS, tm), lambda b, t: (b, 0, t))]
        out_shape = [tok_f32] * 3 + [head_f32] * 2 + [
            jax.ShapeDtypeStruct((bsz, N_HEADS * AUG, t_len), BF16),
            jax.ShapeDtypeStruct((bsz, t_len, N_HEADS * AUG), BF16),
            jax.ShapeDtypeStruct((bsz, N_HEADS * VT_ROWS, t_len), BF16)]
    else:
        weights_q = [p["g_mix1"], p["w_q"]]
        out_specs = [tok, tok, tok, headrow, headrow, tok, tok, tok]
        out_shape = [tok_f32] * 3 + [head_f32] * 2 + [jax.ShapeDtypeStruct((bsz, t_len, d), BF16)] * 3
    return pl.pallas_call(
        functools.partial(_post0_kernel, tm=tm, seg=seg, carry=carry, prompt_layout=prompt_layout),
        grid=(bsz, t_len // tm),
        in_specs=[tok] + _gather_in_specs(bsz * nt, tm, lambda b, t: b * nt + t)
                 + [_mod_spec(g2, tm), _mod_spec(shk, tm), _mod_spec(sck, tm)]
                 + [_full_spec(w) for w in weights_kv]
                 + [_mod_spec(shq, tm), _mod_spec(scq, tm)] + [_full_spec(w) for w in weights_q],
        out_specs=out_specs,
        out_shape=out_shape,
        scratch_shapes=_gather_scratch(tm) + ([pltpu.VMEM((N_HEADS, 1), F32)] if carry else []),
        compiler_params=_params("arbitrary", "arbitrary"),
        name="shared_kv_and_q",
    )(x, y, y_rows, y_rows, g2, shk, sck, *weights_kv, shq, scq, *weights_q)


def _fox_prompt_kernel(qi_ref, kj_ref, qa_ref, ka_ref, vt_ref, o_ref, m_scr, acc_scr, *, tq):
    p = pl.program_id(1)
    i = qi_ref[p]
    j = kj_ref[p]

    @pl.when(j == 0)
    def _():
        m_scr[...] = jnp.full_like(m_scr, -jnp.inf)
        acc_scr[...] = jnp.zeros_like(acc_scr)

    def step(masked):
        if masked:
            key = lax.broadcasted_iota(I32, (tq, tq), 0)
            qry = lax.broadcasted_iota(I32, (tq, tq), 1)
        for h in range(N_HEADS):
            hv = slice(h * VT_ROWS, (h + 1) * VT_ROWS)
            s = jnp.dot(ka_ref[0, :, h * AUG:(h + 1) * AUG], qa_ref[0, h * AUG:(h + 1) * AUG, :],
                        preferred_element_type=F32)
            if masked:
                s = jnp.where(key <= qry, s, -jnp.inf)
            m_old = m_scr[h]
            m_new = jnp.maximum(m_old, jnp.max(s, axis=0, keepdims=True))
            alpha = jnp.exp2(m_old - m_new)
            pr = jnp.exp2(s - m_new).astype(BF16)
            acc_scr[hv, :] = alpha * acc_scr[hv, :] + jnp.dot(vt_ref[0, hv, :], pr,
                                                              preferred_element_type=F32)
            m_scr[h] = m_new

    @pl.when(j < i)
    def _():
        step(False)

    @pl.when(j == i)
    def _():
        step(True)
        for h in range(N_HEADS):
            base = h * VT_ROWS
            out = acc_scr[base:base + HEAD_DIM, :] / acc_scr[base + HEAD_DIM:base + HEAD_DIM + 1, :]
            o_ref[0, :, h * HEAD_DIM:(h + 1) * HEAD_DIM] = jnp.transpose(out).astype(BF16)


def _fox_prompt(qa, ka, vt):
    bsz, _, t_len = vt.shape
    d = N_HEADS * HEAD_DIM
    tq = ATTN_TILE
    nb = t_len // tq
    pairs = [(i, j) for i in range(nb) for j in range(i + 1)]
    qi = jnp.asarray([a for a, _ in pairs], I32)
    kj = jnp.asarray([b for _, b in pairs], I32)
    grid_spec = pltpu.PrefetchScalarGridSpec(
        num_scalar_prefetch=2,
        grid=(bsz, len(pairs)),
        in_specs=[
            pl.BlockSpec((1, N_HEADS * AUG, tq), lambda b, p, qi, kj: (b, 0, qi[p])),
            pl.BlockSpec((1, tq, N_HEADS * AUG), lambda b, p, qi, kj: (b, kj[p], 0)),
            pl.BlockSpec((1, N_HEADS * VT_ROWS, tq), lambda b, p, qi, kj: (b, 0, kj[p])),
        ],
        out_specs=pl.BlockSpec((1, tq, d), lambda b, p, qi, kj: (b, qi[p], 0)),
        scratch_shapes=[pltpu.VMEM((N_HEADS, 1, tq), F32),
                        pltpu.VMEM((N_HEADS * VT_ROWS, tq), F32)],
    )
    return pl.pallas_call(
        functools.partial(_fox_prompt_kernel, tq=tq),
        grid_spec=grid_spec,
        out_shape=jax.ShapeDtypeStruct((bsz, t_len, d), BF16),
        compiler_params=_params("arbitrary", "arbitrary"),
        name="fox_prompt_attention",
    )(qi, kj, qa, ka, vt)


def _page_suffix(lf):
    lane = lax.broadcasted_iota(I32, lf.shape, 1)
    sub = lax.broadcasted_iota(I32, lf.shape, 0)
    x = lf
    sh = N_HEADS
    while sh < LANES:
        x = x + jnp.where(lane + sh < LANES, pltpu.roll(x, LANES - sh, axis=1), 0.0)
        sh *= 2
    y = jnp.where(lane < N_HEADS, x, 0.0)
    sh = N_HEADS
    while sh < LANES:
        y = y + pltpu.roll(y, sh, axis=1)
        sh *= 2
    z = y
    sh = 1
    while sh < SUBLANES:
        z = z + jnp.where(sub + sh < SUBLANES, pltpu.roll(z, SUBLANES - sh, axis=0), 0.0)
        sh *= 2
    return x + (z - y) - lf, z[0:1]


def _fox_decode_kernel(pt_ref, q_ref, kn_ref, vn_ref, cq_ref, ck_ref, *refs, pages):
    del pt_ref
    k_refs = refs[:pages]
    v_refs = refs[pages:2 * pages]
    lf_refs = refs[2 * pages:3 * pages]
    o_ref = refs[3 * pages]
    qm_scr, m_scr, l_scr, acc_scr, r_scr = refs[3 * pages + 1:]
    step = pl.program_id(1)
    n_tok = q_ref.shape[1]
    rows = n_tok * N_HEADS
    page_size = k_refs[0].shape[1]
    lane = lax.broadcasted_iota(I32, (rows, LANES), 1)
    row = lax.broadcasted_iota(I32, (rows, LANES), 0)
    row_head = row // n_tok
    row_tok = row - row_head * n_tok
    cq = cq_ref[0]

    def by_head(x):
        return jnp.concatenate(
            [x[:, h * HEAD_DIM:(h + 1) * HEAD_DIM] for h in range(N_HEADS)], axis=0)

    def attend(u, v_bf):
        m_old = m_scr[...]
        m_new = jnp.maximum(m_old, jnp.max(u, axis=1, keepdims=True) + cq)
        alpha = jnp.exp(m_old - m_new)
        pr = jnp.exp(u - (m_new - cq))
        l_scr[...] = alpha * l_scr[...] + jnp.sum(pr, axis=1, keepdims=True)
        acc_scr[...] = alpha * acc_scr[...] + jnp.dot(pr.astype(BF16), v_bf,
                                                      preferred_element_type=F32)
        m_scr[...] = m_new

    @pl.when(step == 0)
    def _():
        qm = by_head(q_ref[0].astype(F32)).astype(BF16)
        qm_scr[...] = qm
        m_scr[...] = jnp.full_like(m_scr, -jnp.inf)
        l_scr[...] = jnp.zeros_like(l_scr)
        acc_scr[...] = jnp.zeros_like(acc_scr)
        r_scr[...] = jnp.zeros_like(r_scr)
        pad = jnp.zeros((LANES - rows, HEAD_DIM), F32)
        kn = jnp.concatenate([by_head(kn_ref[0].astype(F32)), pad], axis=0).astype(BF16)
        vn = jnp.concatenate([by_head(vn_ref[0].astype(F32)), pad], axis=0).astype(BF16)
        s = lax.dot_general(qm, kn, NT_DIMS, preferred_element_type=F32)
        key_head = lane // n_tok
        key_tok = lane - key_head * n_tok
        u = jnp.where(key_head == row_head, jnp.where(key_tok <= row_tok, s - ck_ref[0], -jnp.inf),
                      -jnp.inf)
        attend(u, vn)

    own_head = (lane & (N_HEADS - 1)) == row_head
    qm = qm_scr[...]
    after_sum = r_scr[...]
    u_parts, v_parts = [], []
    for pg in range(pages):
        later, total = _page_suffix(lf_refs[pg][0])
        after = after_sum + later
        after_sum = after_sum + total
        k2 = k_refs[pg][0].reshape(page_size * N_HEADS, HEAD_DIM).astype(BF16)
        v_parts.append(v_refs[pg][0].reshape(page_size * N_HEADS, HEAD_DIM).astype(BF16))
        s = lax.dot_general(qm, k2, NT_DIMS, preferred_element_type=F32)
        u_parts += [jnp.where(own_head, s[:, c * LANES:(c + 1) * LANES] + after[c:c + 1], -jnp.inf)
                    for c in range(page_size * N_HEADS // LANES)]
    r_scr[...] = after_sum
    attend(jnp.concatenate(u_parts, axis=1), jnp.concatenate(v_parts, axis=0))

    @pl.when(step == pl.num_programs(1) - 1)
    def _():
        o = acc_scr[...] / l_scr[...]
        o_ref[0] = jnp.concatenate(
            [o[h * n_tok:(h + 1) * n_tok] for h in range(N_HEADS)], axis=1).astype(BF16)


def _fox_decode(q, k_new, v_new, cq, ck, cache_k, cache_v, cache_lf, page_table):
    bsz, n_tok, d = q.shape
    n_pages = page_table.shape[1]
    page_size = cache_k.shape[1]
    pages = PAGES_PER_STEP
    rows = n_tok * N_HEADS

    def page_map(pg, ndim):
        def index_map(b, s, pt):
            return (pt[b, n_pages - 1 - (s * pages + pg)],) + (0,) * (ndim - 1)
        return index_map

    seq = lambda b, s, pt: (b, 0, 0)
    in_specs = [pl.BlockSpec((1, n_tok, d), seq), pl.BlockSpec((1, n_tok, d), seq),
                pl.BlockSpec((1, n_tok, d), seq), pl.BlockSpec((1, rows, 1), seq),
                pl.BlockSpec((1, 1, LANES), seq)]
    kv_block = (1, page_size, N_HEADS, HEAD_DIM)
    in_specs += [pl.BlockSpec(kv_block, page_map(pg, 4)) for pg in range(pages)]
    in_specs += [pl.BlockSpec(kv_block, page_map(pg, 4)) for pg in range(pages)]
    in_specs += [pl.BlockSpec((1, SUBLANES, LANES), page_map(pg, 3)) for pg in range(pages)]
    grid_spec = pltpu.PrefetchScalarGridSpec(
        num_scalar_prefetch=1,
        grid=(bsz, n_pages // pages),
        in_specs=in_specs,
        out_specs=pl.BlockSpec((1, n_tok, d), seq),
        scratch_shapes=[pltpu.VMEM((rows, HEAD_DIM), BF16), pltpu.VMEM((rows, 1), F32),
                        pltpu.VMEM((rows, 1), F32), pltpu.VMEM((rows, HEAD_DIM), F32),
                        pltpu.VMEM((1, LANES), F32)],
    )
    return pl.pallas_call(
        functools.partial(_fox_decode_kernel, pages=pages),
        grid_spec=grid_spec,
        out_shape=jax.ShapeDtypeStruct((bsz, n_tok, d), BF16),
        compiler_params=_params("arbitrary", "arbitrary"),
        name="fox_decode_attention",
    )(page_table, q, k_new, v_new, cq, ck, *([cache_k] * pages), *([cache_v] * pages),
      *([cache_lf] * pages))


def _oproj_kernel(x_ref, o_ref, g1_ref, wo_ref, xo_ref):
    xo_ref[0] = x_ref[0] + g1_ref[0] * jnp.dot(o_ref[0], wo_ref[...], preferred_element_type=F32)


def _oproj(x, o, g1, w_o, *, tm):
    bsz, t_len, d = x.shape
    tok = pl.BlockSpec((1, tm, d), lambda b, t: (b, t, 0))
    return pl.pallas_call(
        _oproj_kernel,
        grid=(bsz, t_len // tm),
        in_specs=[tok, tok, _mod_spec(g1, tm), _full_spec(w_o)],
        out_specs=tok,
        out_shape=jax.ShapeDtypeStruct((bsz, t_len, d), F32),
        compiler_params=_params("arbitrary", "arbitrary"),
        name="attn_out_proj",
    )(x, o, g1, w_o)


def _final_kernel(x_ref, y_hbm, pos0_ref, posn_ref, g2_ref, g_ref, o_ref, ybuf, sem_y):
    step, n_steps = _grid_step()
    y = _gathered_tile(step, step == n_steps - 1, y_hbm, pos0_ref, posn_ref, ybuf, sem_y,
                       x_ref.shape[1])
    x = x_ref[0] + g2_ref[0] * y
    o_ref[0] = (x * lax.rsqrt(jnp.mean(x * x, axis=-1, keepdims=True) + EPS)) * g_ref[...]


def _final(x, y, y_rows, g2, g_final, *, tm):
    bsz, t_len, d = x.shape
    nt = t_len // tm
    tok = pl.BlockSpec((1, tm, d), lambda b, t: (b, t, 0))
    return pl.pallas_call(
        _final_kernel,
        grid=(bsz, nt),
        in_specs=[tok] + _gather_in_specs(bsz * nt, tm, lambda b, t: b * nt + t)
                 + [_mod_spec(g2, tm), _full_spec(g_final)],
        out_specs=tok,
        out_shape=jax.ShapeDtypeStruct((bsz, t_len, d), F32),
        scratch_shapes=_gather_scratch(tm),
        compiler_params=_params("arbitrary", "arbitrary"),
        name="final_norm",
    )(x, y, y_rows, y_rows, g2, g_final)


def _trunk(x, mods0, mods1, kvmods, prev, h0, w, attend, *, tm, tmx, seg, carry):
    sh1, sc1, g1, sh2, sc2, g2 = mods0
    x, h_out, conv_out = _lru_layer(x, sh1, sc1, g1, w["lru"], prev, h0, tm=tm, seg=seg, carry=carry)
    y, y_rows = _moe(x, sh2, sc2, w["moe0"], tm=tm, tmx=tmx)
    sh1b, sc1b, g1b, sh2b, sc2b, g2b = mods1
    shk, sck = kvmods
    x, k, v, lft, fct, *operands = _post0(x, y, y_rows, g2, shk, sck, sh1b, sc1b, w["post0"],
                                          tm=tm, seg=seg, carry=carry, prompt_layout=carry)
    o = attend(fct, *operands)
    x = _oproj(x, o, g1b, w["w_o"], tm=tm)
    y, y_rows = _moe(x, sh2b, sc2b, w["moe1"], tm=tm, tmx=tmx)
    y_out = _final(x, y, y_rows, g2b, w["g_final"], tm=tm)
    return y_out, h_out, conv_out, k, v, lft


def kernel(x_prompt, x_sample, state_h, state_conv, cache_k, cache_v, cache_logf, page_table,
           c_prompt, c_sample, w_mod, b_mod, g_mix, g_moe, lru_w_in, lru_conv_w, lru_conv_b,
           lru_w_a, lru_b_a, lru_w_x, lru_b_x, lru_lambda, lru_w_out, kv_g, kv_w_mod, kv_b_mod,
           kv_w, kv_b_f, attn_w_q, attn_w_o, moe_w_grp, moe_b_grp, moe_w_exp, moe_b_exp,
           moe_w_up, moe_w_down, g_final):
    d = D_MODEL
    bp, tp, _ = x_prompt.shape
    bs, ts, _ = x_sample.shape
    hd = N_HEADS * HEAD_DIM
    row = lambda a: a.reshape(1, -1)

    def moe_weights(layer):
        wr_t = jnp.concatenate(
            [moe_w_grp[layer].T, moe_w_exp[layer].T,
             jnp.zeros((ROUTER_ROWS - N_GROUPS - N_EXPERTS, d), F32)], axis=0)
        br = jnp.concatenate(
            [moe_b_grp[layer], moe_b_exp[layer],
             jnp.zeros((ROUTER_ROWS - N_GROUPS - N_EXPERTS,), F32)]).reshape(ROUTER_ROWS, 1)
        return dict(g_moe=row(g_moe[layer]), wr_t=wr_t, br=br,
                    w_up=moe_w_up[layer].astype(BF16), w_down=moe_w_down[layer].astype(BF16))

    w = dict(
        lru=dict(g_mix=row(g_mix[0]), w_in=lru_w_in[0].astype(BF16), conv_w=lru_conv_w[0],
                 conv_b=row(lru_conv_b[0]), w_a=lru_w_a[0].astype(BF16), b_a=row(lru_b_a[0]),
                 w_x=lru_w_x[0].astype(BF16), b_x=row(lru_b_x[0]), lam=row(lru_lambda[0]),
                 w_out=lru_w_out[0].astype(BF16)),
        moe0=moe_weights(0),
        moe1=moe_weights(1),
        post0=dict(kv_g=row(kv_g), w_k=kv_w[:, :hd].astype(BF16),
                   w_v=kv_w[:, hd:2 * hd].astype(BF16), w_f_t=kv_w[:, 2 * hd:].T,
                   w_v_t=kv_w[:, hd:2 * hd].T.astype(BF16), b_f=kv_b_f.reshape(N_HEADS, 1),
                   g_mix1=row(g_mix[1]), w_q=attn_w_q[0].astype(BF16),
                   w_q_t=attn_w_q[0].T.astype(BF16)),
        w_o=attn_w_o[0].astype(BF16),
        g_final=row(g_final),
    )

    n_c = bp + bs
    c_rows = -(-n_c // SUBLANES) * SUBLANES
    c_all = jnp.concatenate([c_prompt, c_sample, jnp.zeros((c_rows - n_c, d), F32)], axis=0)
    mod_l0 = _mm_bias(c_all, w_mod[0], b_mod[0])
    mod_l1 = _mm_bias(c_all, w_mod[1], b_mod[1])
    mod_kv = _mm_bias(c_all, kv_w_mod, kv_b_mod)

    def prompt_mods(m, parts):
        return [a.reshape(bp, 1, d) for a in jnp.split(m[:bp], parts, axis=-1)]

    def sample_mods(m, parts):
        return [jnp.repeat(a, ts, axis=0).reshape(1, bs * ts, d)
                for a in jnp.split(m[bp:n_c], parts, axis=-1)]

    prev_p = jnp.zeros((bp, SUBLANES, D_RNN), F32)
    h0_p = jnp.zeros((bp, 1, D_RNN), F32)
    attend_p = lambda fct, qa, ka, vt: _fox_prompt(qa, ka, vt)
    y_p, h_p, conv_p, k_p, v_p, lft_p = _trunk(
        x_prompt, prompt_mods(mod_l0, 6), prompt_mods(mod_l1, 6), prompt_mods(mod_kv, 2),
        prev_p, h0_p, w, attend_p, tm=TOKEN_TILE, tmx=EXPERT_TILE, seg=TOKEN_TILE, carry=True)

    n_s = bs * ts
    prev_s = jnp.pad(state_conv[0], ((0, 0), (ts - (CONV_WIDTH - 1), 0), (0, 0))).reshape(1, n_s, D_RNN)
    h0_s = jnp.repeat(state_h[0], ts, axis=0).reshape(1, n_s, D_RNN)
    n_pool, page_size = cache_k.shape[0], cache_k.shape[1]
    cache_lf = cache_logf.reshape(n_pool, SUBLANES, page_size * N_HEADS // SUBLANES)

    def attend_s(fct, q, kb, vb):
        c_new = jnp.transpose(fct.reshape(N_HEADS, bs, ts), (1, 0, 2)).reshape(bs, N_HEADS * ts)
        cq = c_new.reshape(bs, N_HEADS * ts, 1)
        ck = jnp.pad(c_new, ((0, 0), (0, LANES - N_HEADS * ts))).reshape(bs, 1, LANES)
        o = _fox_decode(q.reshape(bs, ts, d), kb.reshape(bs, ts, d), vb.reshape(bs, ts, d), cq, ck,
                        cache_k, cache_v, cache_lf, page_table)
        return o.reshape(1, n_s, d)

    y_s, h_s, conv_s, k_s, v_s, lft_s = _trunk(
        x_sample.reshape(1, n_s, d), sample_mods(mod_l0, 6), sample_mods(mod_l1, 6),
        sample_mods(mod_kv, 2), prev_s, h0_s, w, attend_s, tm=n_s, tmx=SAMPLE_EXPERT_TILE, seg=ts,
        carry=False)

    n_conv = CONV_WIDTH - 1
    return (
        y_p,
        y_s.reshape(bs, ts, d),
        h_p.reshape(1, bp, D_RNN),
        conv_p[:, SUBLANES - n_conv:].reshape(1, bp, n_conv, D_RNN),
        k_p.reshape(bp, tp, N_HEADS, HEAD_DIM),
        v_p.reshape(bp, tp, N_HEADS, HEAD_DIM),
        jnp.transpose(lft_p, (0, 2, 1)),
        h_s.reshape(bs, ts, D_RNN)[:, ts - 1].reshape(1, bs, D_RNN),
        conv_s.reshape(bs, ts, D_RNN)[:, ts - n_conv:].reshape(1, bs, n_conv, D_RNN),
        k_s.reshape(bs, ts, N_HEADS, HEAD_DIM),
        v_s.reshape(bs, ts, N_HEADS, HEAD_DIM),
        jnp.transpose(lft_s.reshape(N_HEADS, bs, ts), (1, 2, 0)),
    )
```

```python
import functools
import math

import jax
import jax.numpy as jnp
from jax import lax
from jax.experimental import pallas as pl
from jax.experimental.pallas import tpu as pltpu

F32 = jnp.float32
BF16 = jnp.bfloat16
I32 = jnp.int32

D_MODEL = 1024
D_RNN = D_MODEL
N_LRU_BLOCKS = 8
LRU_BLOCK = D_RNN // N_LRU_BLOCKS
CONV_WIDTH = 4
LRU_C = 8.0
N_HEADS = 8
HEAD_DIM = D_MODEL // N_HEADS
N_GROUPS = 4
EXPERTS_PER_GROUP = 4
N_EXPERTS = N_GROUPS * EXPERTS_PER_GROUP
D_EXPERT = D_MODEL // 2
EPS = 1e-6
N_PAIRS = 6
N_CLASSES = N_GROUPS * N_PAIRS
PAIR_LO = (0, 0, 0, 1, 1, 2)
PAIR_HI = (1, 2, 3, 2, 3, 3)

SUBLANES = 8
LANES = 128
VMEM_LIMIT = 48 * 1024 * 1024
DMA_THREADS = 2

TOKEN_TILE = 256
EXPERT_TILE = 512
SAMPLE_EXPERT_TILE = 32
ROUTER_ROWS = 32
ATTN_TILE = 512
PAGES_PER_STEP = 16
MOD_TILE = 512

HIGHEST = lax.Precision.HIGHEST
NT_DIMS = (((1,), (1,)), ((), ()))


def _params(*semantics):
    return pltpu.CompilerParams(dimension_semantics=semantics, vmem_limit_bytes=VMEM_LIMIT)


def _norm_mod(x, g, shift, scale):
    y = x * lax.rsqrt(jnp.mean(x * x, axis=-1, keepdims=True) + EPS)
    return (y * g) * (1.0 + scale) + shift


def _log_sigmoid(x):
    return jnp.minimum(x, 0.0) - jnp.log1p(jnp.exp(-jnp.abs(x)))


def _sigmoid(x):
    return 1.0 / (1.0 + jnp.exp(-x))


def _gelu_tanh(x):
    c = math.sqrt(2.0 / math.pi)
    return x * (0.5 * (1.0 + jnp.tanh(c * (x + 0.044715 * (x * x * x)))))


def _mod_spec(arr, tm):
    if arr.shape[1] == 1:
        return pl.BlockSpec((1, 1, arr.shape[2]), lambda b, t: (b, 0, 0))
    return pl.BlockSpec((1, tm, arr.shape[2]), lambda b, t: (b, t, 0))


ROW_TILES = D_MODEL // LANES


def _row_spec(nt, tm):
    return pl.BlockSpec((tm * ROW_TILES, LANES), lambda b, t: (b * nt + t, 0))


def _to_token_tiles(ref, lead, x):
    rows = x.shape[0]
    for c in range(ROW_TILES):
        ref[(*lead, pl.ds(c, rows, stride=ROW_TILES), slice(None))] = x[:, c * LANES:(c + 1) * LANES]


def _from_token_tiles(ref, lead, rows):
    return jnp.concatenate(
        [ref[(*lead, pl.ds(c, rows, stride=ROW_TILES), slice(None))] for c in range(ROW_TILES)],
        axis=1)


def _start_token_gather(src_hbm, idx_ref, buf, slot, sem, count):
    for r in range(count):
        rows = pl.ds(pl.multiple_of(idx_ref[0, 0, r], ROW_TILES), ROW_TILES)
        pltpu.make_async_copy(src_hbm.at[rows], buf.at[slot, pl.ds(r * ROW_TILES, ROW_TILES)],
                              sem.at[slot]).start(priority=r % DMA_THREADS)


def _wait_token_gather(src_hbm, buf, slot, sem, count):
    pltpu.make_async_copy(src_hbm.at[pl.ds(0, count * ROW_TILES)], buf.at[slot], sem.at[slot]).wait()


def _gathered_tile(step, is_last, src_hbm, idx0_ref, idxn_ref, buf, sem, count):
    slot = lax.rem(step, 2)

    @pl.when(step == 0)
    def _():
        _start_token_gather(src_hbm, idx0_ref, buf, 0, sem, count)

    _wait_token_gather(src_hbm, buf, slot, sem, count)
    _start_token_gather(src_hbm, idxn_ref, buf, 1 - slot, sem, count)
    x = _from_token_tiles(buf, (slot,), count)

    @pl.when(is_last)
    def _():
        _wait_token_gather(src_hbm, buf, 1 - slot, sem, count)

    return x


def _gather_in_specs(n_steps, count, step_of):
    block = (1, 1, count)
    return [pl.BlockSpec(memory_space=pl.ANY),
            pl.BlockSpec(block, lambda *g: (0, 0, 0), memory_space=pltpu.SMEM),
            pl.BlockSpec(block, lambda *g: (jnp.minimum(step_of(*g) + 1, n_steps - 1), 0, 0),
                         memory_space=pltpu.SMEM)]


def _gather_scratch(count):
    return [pltpu.VMEM((2, count * ROW_TILES, LANES), F32), pltpu.SemaphoreType.DMA((2,))]


def _full_spec(arr):
    zeros = (0,) * arr.ndim
    return pl.BlockSpec(arr.shape, lambda *_: zeros)


def _mm_bias_kernel(x_ref, w_ref, b_ref, o_ref):
    o_ref[...] = jnp.dot(x_ref[...].astype(BF16), w_ref[...].astype(BF16),
                         preferred_element_type=F32) + b_ref[...]


def _mm_bias(x, w, b):
    m, d = x.shape
    n = w.shape[1]
    return pl.pallas_call(
        _mm_bias_kernel,
        grid=(n // MOD_TILE,),
        in_specs=[pl.BlockSpec((m, d), lambda j: (0, 0)),
                  pl.BlockSpec((d, MOD_TILE), lambda j: (0, j)),
                  pl.BlockSpec((1, MOD_TILE), lambda j: (0, j))],
        out_specs=pl.BlockSpec((m, MOD_TILE), lambda j: (0, j)),
        out_shape=jax.ShapeDtypeStruct((m, n), F32),
        compiler_params=_params("arbitrary"),
        name="mod_vectors",
    )(x, w, b.reshape(1, n))


def _lru_kernel(x_ref, sh_ref, sc_ref, gt_ref, gmix_ref, win_ref, cw_ref, cb_ref, wa_ref, ba_ref,
                wx_ref, bx_ref, lam_ref, wout_ref, prev_ref, h0_ref,
                xo_ref, hs_ref, xb_ref, *scratch, tm, seg, carry):
    x = x_ref[0]
    hn = _norm_mod(x, gmix_ref[...], sh_ref[0], sc_ref[0])
    xg = jnp.dot(hn.astype(BF16), win_ref[...], preferred_element_type=F32)
    xb = xg[:, :D_RNN]
    gb = xg[:, D_RNN:]

    if carry:
        prev_scr, h_scr = scratch

        @pl.when(pl.program_id(1) == 0)
        def _():
            prev_scr[...] = prev_ref[0]
            h_scr[...] = h0_ref[0]

        prev = prev_scr[...]
        hprev = h_scr[...]
    else:
        prev = prev_ref[0]
        hprev = h0_ref[0]

    row = lax.broadcasted_iota(I32, (tm, 1), 0)
    rseg = row & (seg - 1)
    nprev = prev.shape[0]

    cw = cw_ref[...]
    xc = cb_ref[...] + cw[CONV_WIDTH - 1:CONV_WIDTH] * xb
    for d in range(1, CONV_WIDTH):
        rolled = pltpu.roll(xb, d, axis=0)
        rp = pltpu.roll(prev, nprev - SUBLANES + d, axis=0)
        if carry:
            head = jnp.where(row[:SUBLANES] < d, rp, rolled[:SUBLANES])
            shifted = head if tm == SUBLANES else jnp.concatenate([head, rolled[SUBLANES:]], axis=0)
        else:
            shifted = jnp.where(rseg < d, rp, rolled)
        xc = xc + cw[CONV_WIDTH - 1 - d:CONV_WIDTH - d] * shifted

    xcb = xc.astype(BF16)
    ra = jnp.concatenate(
        [jnp.dot(xcb[:, n * LRU_BLOCK:(n + 1) * LRU_BLOCK], wa_ref[n], preferred_element_type=F32)
         for n in range(N_LRU_BLOCKS)], axis=1)
    rx = jnp.concatenate(
        [jnp.dot(xcb[:, n * LRU_BLOCK:(n + 1) * LRU_BLOCK], wx_ref[n], preferred_element_type=F32)
         for n in range(N_LRU_BLOCKS)], axis=1)
    r = _sigmoid(ra + ba_ref[...])
    ig = _sigmoid(rx + bx_ref[...])
    log_a = (LRU_C * r) * _log_sigmoid(lam_ref[...])
    a = jnp.exp(log_a)
    bt = jnp.sqrt(-jnp.tanh(log_a) * (a * a + 1.0)) * (ig * xc)

    s = 1
    while s < seg:
        inside = rseg >= s
        a_sh = jnp.where(inside, pltpu.roll(a, s, axis=0), 1.0)
        b_sh = jnp.where(inside, pltpu.roll(bt, s, axis=0), 0.0)
        bt = a * b_sh + bt
        a = a * a_sh
        s *= 2
    hs = bt + a * hprev

    if carry:
        h_scr[...] = hs[tm - 1:tm]
        prev_scr[...] = xb[tm - SUBLANES:]
        hs_ref[0] = hs[tm - 1:tm]
        xb_ref[0] = xb[tm - SUBLANES:]
    else:
        hs_ref[0] = hs
        xb_ref[0] = xb

    y = (hs * _gelu_tanh(gb)).astype(BF16)
    out = jnp.dot(y, wout_ref[...], preferred_element_type=F32)
    xo_ref[0] = x + gt_ref[0] * out


def _lru_layer(x, sh, sc, gt, p, prev, h0, *, tm, seg, carry):
    bsz, t_len, d = x.shape
    r = D_RNN
    nt = t_len // tm
    tok = pl.BlockSpec((1, tm, d), lambda b, t: (b, t, 0))
    if carry:
        state_specs = [pl.BlockSpec((1, SUBLANES, r), lambda b, t: (b, 0, 0)),
                       pl.BlockSpec((1, 1, r), lambda b, t: (b, 0, 0))]
        out_specs = [tok,
                     pl.BlockSpec((1, 1, r), lambda b, t: (b, 0, 0)),
                     pl.BlockSpec((1, SUBLANES, r), lambda b, t: (b, 0, 0))]
        out_shape = [jax.ShapeDtypeStruct((bsz, t_len, d), F32),
                     jax.ShapeDtypeStruct((bsz, 1, r), F32),
                     jax.ShapeDtypeStruct((bsz, SUBLANES, r), F32)]
        scratch = [pltpu.VMEM((SUBLANES, r), F32), pltpu.VMEM((1, r), F32)]
    else:
        rtok = pl.BlockSpec((1, tm, r), lambda b, t: (b, t, 0))
        state_specs = [rtok, rtok]
        out_specs = [tok, rtok, rtok]
        out_shape = [jax.ShapeDtypeStruct((bsz, t_len, d), F32),
                     jax.ShapeDtypeStruct((bsz, t_len, r), F32),
                     jax.ShapeDtypeStruct((bsz, t_len, r), F32)]
        scratch = []
    weights = [p["g_mix"], p["w_in"], p["conv_w"], p["conv_b"], p["w_a"], p["b_a"], p["w_x"],
               p["b_x"], p["lam"], p["w_out"]]
    return pl.pallas_call(
        functools.partial(_lru_kernel, tm=tm, seg=seg, carry=carry),
        grid=(bsz, nt),
        in_specs=[tok, _mod_spec(sh, tm), _mod_spec(sc, tm), _mod_spec(gt, tm)]
                 + [_full_spec(w) for w in weights] + state_specs,
        out_specs=out_specs,
        out_shape=out_shape,
        scratch_shapes=scratch,
        compiler_params=_params("arbitrary", "arbitrary"),
        name="rglru_layer",
    )(x, sh, sc, gt, *weights, prev, h0)


def _router_kernel(x_ref, sh_ref, sc_ref, g_ref, wr_ref, br_ref,
                   hn_ref, cls_ref, rank_ref, glo_ref, ghi_ref, cnt_ref, carry_scr, *, tm):
    @pl.when((pl.program_id(0) == 0) & (pl.program_id(1) == 0))
    def _():
        carry_scr[...] = jnp.zeros_like(carry_scr)

    hn = _norm_mod(x_ref[0], g_ref[...], sh_ref[0], sc_ref[0])
    _to_token_tiles(hn_ref, (), hn)
    lt = lax.dot_general(wr_ref[...], hn, NT_DIMS, precision=HIGHEST,
                         preferred_element_type=F32) + br_ref[...]

    gl = [lt[k:k + 1] for k in range(N_GROUPS)]
    gmax = jnp.maximum(jnp.maximum(gl[0], gl[1]), jnp.maximum(gl[2], gl[3]))
    g_sel = jnp.where(gl[0] >= gmax, 0, jnp.where(gl[1] >= gmax, 1, jnp.where(gl[2] >= gmax, 2, 3)))
    p_grp = 1.0 / (jnp.exp(gl[0] - gmax) + jnp.exp(gl[1] - gmax)
                   + jnp.exp(gl[2] - gmax) + jnp.exp(gl[3] - gmax))

    def expert_logit(e):
        rows = [lt[N_GROUPS + g * EXPERTS_PER_GROUP + e:N_GROUPS + g * EXPERTS_PER_GROUP + e + 1]
                for g in range(N_GROUPS)]
        return jnp.where(g_sel == 0, rows[0],
                         jnp.where(g_sel == 1, rows[1], jnp.where(g_sel == 2, rows[2], rows[3])))

    es = [expert_logit(e) for e in range(EXPERTS_PER_GROUP)]

    def first_argmax(vals):
        vmax = jnp.maximum(jnp.maximum(vals[0], vals[1]), jnp.maximum(vals[2], vals[3]))
        idx = jnp.where(vals[0] >= vmax, 0,
                        jnp.where(vals[1] >= vmax, 1, jnp.where(vals[2] >= vmax, 2, 3)))
        return vmax, idx

    v1, i1 = first_argmax(es)
    rest = [jnp.where(i1 == e, -jnp.inf, es[e]) for e in range(EXPERTS_PER_GROUP)]
    v2, i2 = first_argmax(rest)
    e21 = jnp.exp(v2 - v1)
    w1 = (1.0 / (1.0 + e21)) * p_grp
    w2 = (e21 / (1.0 + e21)) * p_grp
    first_lower = i1 < i2
    lo = jnp.where(first_lower, i1, i2)
    hi = jnp.where(first_lower, i2, i1)
    glo_ref[0] = jnp.where(first_lower, w1, w2)
    ghi_ref[0] = jnp.where(first_lower, w2, w1)
    pair = jnp.where(lo == 0, hi - 1, jnp.where(lo == 1, hi + 1, 5))
    cls = g_sel * N_PAIRS + pair
    cls_ref[0] = cls

    crow = lax.broadcasted_iota(I32, (ROUTER_ROWS, tm), 0)
    onehot = jnp.where(crow == cls, 1.0, 0.0)
    ki = lax.broadcasted_iota(I32, (tm, tm), 0)
    kj = lax.broadcasted_iota(I32, (tm, tm), 1)
    upper = jnp.where(ki <= kj, 1.0, 0.0).astype(BF16)
    cum = jnp.dot(onehot.astype(BF16), upper, preferred_element_type=F32)
    carry = carry_scr[...]
    rank = jnp.sum(onehot * (cum - 1.0 + carry), axis=0, keepdims=True)
    rank_ref[0] = rank.astype(I32)
    carry = carry + cum[:, tm - 1:tm]
    carry_scr[...] = carry
    cnt_ref[...] = carry


def _router(x, sh, sc, g, wr_t, br, *, tm):
    bsz, t_len, d = x.shape
    nt = t_len // tm
    n_tiles = bsz * nt
    tok = pl.BlockSpec((1, tm, d), lambda b, t: (b, t, 0))
    lane_row = pl.BlockSpec((1, 1, tm), lambda b, t: (b * nt + t, 0, 0))
    row_shape = jax.ShapeDtypeStruct((n_tiles, 1, tm), F32)
    row_shape_i = jax.ShapeDtypeStruct((n_tiles, 1, tm), I32)
    return pl.pallas_call(
        functools.partial(_router_kernel, tm=tm),
        grid=(bsz, nt),
        in_specs=[tok, _mod_spec(sh, tm), _mod_spec(sc, tm), _full_spec(g), _full_spec(wr_t),
                  _full_spec(br)],
        out_specs=[_row_spec(nt, tm), lane_row, lane_row, lane_row, lane_row,
                   pl.BlockSpec((ROUTER_ROWS, 1), lambda b, t: (0, 0))],
        out_shape=[jax.ShapeDtypeStruct((bsz * t_len * ROW_TILES, LANES), F32), row_shape_i, row_shape_i,
                   row_shape, row_shape, jax.ShapeDtypeStruct((ROUTER_ROWS, 1), F32)],
        scratch_shapes=[pltpu.VMEM((ROUTER_ROWS, 1), F32)],
        compiler_params=_params("arbitrary", "arbitrary"),
        name="moe_router",
    )(x, sh, sc, g, wr_t, br)


def _expert_kernel(e1_ref, e2_ref, valid_ref, hn_hbm, src0_ref, srcn_ref, glo_ref, ghi_ref,
                   wu1_ref, wd1_ref, wu2_ref, wd2_ref, o_ref, xbuf, sem_g, *, tmx):
    del e1_ref, e2_ref
    j = pl.program_id(0)
    last = pl.num_programs(0) - 1

    @pl.when(valid_ref[j] == 0)
    def _():
        o_ref[...] = jnp.zeros_like(o_ref)

    @pl.when(valid_ref[j] == 1)
    def _():
        is_last = (j == last) | (valid_ref[jnp.minimum(j + 1, last)] == 0)
        x = _gathered_tile(j, is_last, hn_hbm, src0_ref, srcn_ref, xbuf, sem_g, tmx).astype(BF16)
        ri = lax.broadcasted_iota(I32, (tmx, tmx), 0)
        ci = lax.broadcasted_iota(I32, (tmx, tmx), 1)

        def column(row_ref):
            return jnp.sum(jnp.where(ri == ci, row_ref[0], 0.0), axis=1, keepdims=True)

        def ffn(wu_ref, wd_ref, gate):
            gu = jnp.dot(x, wu_ref[0], preferred_element_type=F32)
            g_ = gu[:, :D_EXPERT]
            u_ = gu[:, D_EXPERT:]
            act = ((g_ * _sigmoid(g_)) * u_).astype(BF16)
            return gate * jnp.dot(act, wd_ref[0], preferred_element_type=F32)

        _to_token_tiles(o_ref, (), ffn(wu1_ref, wd1_ref, column(glo_ref))
                        + ffn(wu2_ref, wd2_ref, column(ghi_ref)))


def _experts(hn, src, glo_s, ghi_s, w_up, w_down, e1, e2, valid, n_tiles, tmx):
    d = D_MODEL
    idx_block = (1, 1, tmx)
    at_tile = lambda j, e1, e2, v: (j, 0, 0)
    grid_spec = pltpu.PrefetchScalarGridSpec(
        num_scalar_prefetch=3,
        grid=(n_tiles,),
        in_specs=_gather_in_specs(n_tiles, tmx, lambda j, *_: j) + [
            pl.BlockSpec(idx_block, at_tile),
            pl.BlockSpec(idx_block, at_tile),
            pl.BlockSpec((1, d, 2 * D_EXPERT), lambda j, e1, e2, v: (e1[j], 0, 0)),
            pl.BlockSpec((1, D_EXPERT, d), lambda j, e1, e2, v: (e1[j], 0, 0)),
            pl.BlockSpec((1, d, 2 * D_EXPERT), lambda j, e1, e2, v: (e2[j], 0, 0)),
            pl.BlockSpec((1, D_EXPERT, d), lambda j, e1, e2, v: (e2[j], 0, 0)),
        ],
        out_specs=pl.BlockSpec((tmx * ROW_TILES, LANES), lambda j, e1, e2, v: (j, 0)),
        scratch_shapes=_gather_scratch(tmx),
    )
    return pl.pallas_call(
        functools.partial(_expert_kernel, tmx=tmx),
        grid_spec=grid_spec,
        out_shape=jax.ShapeDtypeStruct((n_tiles * tmx * ROW_TILES, LANES), F32),
        compiler_params=_params("arbitrary"),
        name="moe_experts",
    )(e1, e2, valid, hn, src, src, glo_s, ghi_s, w_up, w_down, w_up, w_down)


def _moe(x, sh, sc, p, *, tm, tmx):
    bsz, t_len, d = x.shape
    n = bsz * t_len
    hn, cls, rank, glo, ghi, cnt = _router(x, sh, sc, p["g_moe"], p["wr_t"], p["br"], tm=tm)
    cls = cls.reshape(n)
    rank = rank.reshape(n)
    counts = cnt[:N_CLASSES, 0].astype(I32)
    padded = ((counts + tmx - 1) // tmx) * tmx
    ends = jnp.cumsum(padded)
    offs = ends - padded
    pos = offs[cls] + rank
    n_tiles = n // tmx + N_CLASSES
    n_rows = n_tiles * tmx
    total = ends[N_CLASSES - 1]
    tile_start = jnp.arange(n_tiles, dtype=I32) * tmx
    valid = (tile_start < total).astype(I32)
    last_tile = jnp.maximum(total // tmx - 1, 0)
    tile_cls = jnp.minimum(
        jnp.sum((ends[None, :] <= jnp.minimum(tile_start, last_tile * tmx)[:, None]).astype(I32),
                axis=1), N_CLASSES - 1)
    grp = tile_cls // N_PAIRS
    pair = tile_cls % N_PAIRS
    e1 = grp * EXPERTS_PER_GROUP + jnp.asarray(PAIR_LO, I32)[pair]
    e2 = grp * EXPERTS_PER_GROUP + jnp.asarray(PAIR_HI, I32)[pair]
    token_info = jnp.stack([jnp.arange(n, dtype=I32).astype(F32), glo.reshape(n), ghi.reshape(n)],
                           axis=1)
    row_info = jnp.zeros((n_rows, 3), F32).at[pos].set(token_info)
    as_tiles = lambda a: a.reshape(n_tiles, 1, tmx)
    src_rows = as_tiles(row_info[:, 0].astype(I32) * ROW_TILES)
    ys = _experts(hn, src_rows, as_tiles(row_info[:, 1]), as_tiles(row_info[:, 2]),
                  p["w_up"], p["w_down"], e1, e2, valid, n_tiles, tmx)
    return ys, (pos * ROW_TILES).reshape(n // tm, 1, tm)


AUG = 2 * HEAD_DIM
N_PIECES = 3
VT_ROWS = HEAD_DIM + 16
LOG2E = 1.0 / math.log(2.0)


def _bf16_pieces(x):
    pieces = []
    rest = x
    for _ in range(N_PIECES):
        piece = rest.astype(BF16).astype(F32)
        pieces.append(piece)
        rest = rest - piece
    return pieces


def _grid_step():
    n_steps = pl.num_programs(0) * pl.num_programs(1)
    return pl.program_id(0) * pl.num_programs(1) + pl.program_id(1), n_steps


def _post0_kernel(x_ref, y_hbm, pos0_ref, posn_ref, g2_ref, shk_ref, sck_ref, kvg_ref, wk_ref, wv_ref,
                  wft_ref, bf_ref, shq_ref, scq_ref, gq_ref, wq_ref, *refs,
                  tm, seg, carry, prompt_layout):
    if prompt_layout:
        wvt_ref, x1_ref, k_ref, v_ref, lft_ref, fct_ref, qa_ref, ka_ref, vt_ref = refs[:9]
        scratch = refs[9:]
    else:
        x1_ref, k_ref, v_ref, lft_ref, fct_ref, q_ref, kb_ref, vb_ref = refs[:8]
        scratch = refs[8:]
    ybuf, sem_y = scratch[:2]
    scratch = scratch[2:]
    scale = HEAD_DIM ** -0.5
    step, n_steps = _grid_step()
    y = _gathered_tile(step, step == n_steps - 1, y_hbm, pos0_ref, posn_ref, ybuf, sem_y, tm)
    x1 = x_ref[0] + g2_ref[0] * y
    x1_ref[0] = x1
    hk = _norm_mod(x1, kvg_ref[...], shk_ref[0], sck_ref[0])
    hkb = hk.astype(BF16)
    k = jnp.dot(hkb, wk_ref[...], preferred_element_type=F32)
    v = jnp.dot(hkb, wv_ref[...], preferred_element_type=F32)
    k_ref[0] = k
    v_ref[0] = v
    f = lax.dot_general(wft_ref[...], hk, NT_DIMS, precision=HIGHEST,
                        preferred_element_type=F32) + bf_ref[...]
    lf = _log_sigmoid(f)
    lft_ref[0] = lf
    col = lax.broadcasted_iota(I32, (1, tm), 1)
    cseg = col & (seg - 1)
    c = lf
    s = 1
    while s < seg:
        c = c + jnp.where(cseg >= s, pltpu.roll(c, s, axis=1), 0.0)
        s *= 2
    if carry:
        (f_scr,) = scratch

        @pl.when(pl.program_id(1) == 0)
        def _():
            f_scr[...] = jnp.zeros_like(f_scr)

        c = c + f_scr[...]
        f_scr[...] = c[:, tm - 1:tm]
    fct_ref[0] = c
    hqb = _norm_mod(x1, gq_ref[...], shq_ref[0], scq_ref[0]).astype(BF16)

    if not prompt_layout:
        q_ref[0] = (jnp.dot(hqb, wq_ref[...], preferred_element_type=F32) * scale).astype(BF16)
        kb_ref[0] = k.astype(BF16)
        vb_ref[0] = v.astype(BF16)
        return

    qt = lax.dot_general(wq_ref[...], hqb, NT_DIMS, preferred_element_type=F32) * (scale * LOG2E)
    vt = lax.dot_general(wvt_ref[...], hkb, NT_DIMS, preferred_element_type=F32)
    ones_rows = jnp.where(lax.broadcasted_iota(I32, (VT_ROWS - HEAD_DIM, tm), 0) == 0, 1.0, 0.0)
    v_rows = []
    for h in range(N_HEADS):
        v_rows += [vt[h * HEAD_DIM:(h + 1) * HEAD_DIM], ones_rows]
    vt_ref[0] = jnp.concatenate(v_rows, axis=0).astype(BF16)
    hi, mid, lo = _bf16_pieces(c * LOG2E)
    sub = lax.broadcasted_iota(I32, (SUBLANES, tm), 0)
    pad = jnp.zeros((HEAD_DIM - SUBLANES, tm), F32)
    q_rows, k_rows = [], []
    for h in range(N_HEADS):
        hs = slice(h, h + 1)
        eq = jnp.where(sub < 3, -1.0, jnp.where(sub == 3, hi[hs], jnp.where(
            sub == 4, mid[hs], jnp.where(sub == 5, lo[hs], 0.0))))
        ek = jnp.where(sub == 0, hi[hs], jnp.where(sub == 1, mid[hs], jnp.where(
            sub == 2, lo[hs], jnp.where(sub < 6, 1.0, 0.0))))
        q_rows += [qt[h * HEAD_DIM:(h + 1) * HEAD_DIM], eq, pad]
        k_rows += [ek, pad]
    qa_ref[0] = jnp.concatenate(q_rows, axis=0).astype(BF16)
    ekt = jnp.concatenate(k_rows, axis=0).astype(BF16)
    ri = lax.broadcasted_iota(I32, (tm, tm), 0)
    ci = lax.broadcasted_iota(I32, (tm, tm), 1)
    eye = jnp.where(ri == ci, 1.0, 0.0).astype(BF16)
    extra = lax.dot_general(eye, ekt, NT_DIMS, preferred_element_type=F32)
    k_cols = []
    for h in range(N_HEADS):
        hs = slice(h * HEAD_DIM, (h + 1) * HEAD_DIM)
        k_cols += [k[:, hs], extra[:, hs]]
    ka_ref[0] = jnp.concatenate(k_cols, axis=1).astype(BF16)


def _post0(x, y, y_rows, g2, shk, sck, shq, scq, p, *, tm, seg, carry, prompt_layout):
    bsz, t_len, d = x.shape
    nt = t_len // tm
    tok = pl.BlockSpec((1, tm, d), lambda b, t: (b, t, 0))
    headrow = pl.BlockSpec((1, N_HEADS, tm), lambda b, t: (b, 0, t))
    weights_kv = [p["kv_g"], p["w_k"], p["w_v"], p["w_f_t"], p["b_f"]]
    tok_f32 = jax.ShapeDtypeStruct((bsz, t_len, d), F32)
    head_f32 = jax.ShapeDtypeStruct((bsz, N_HEADS, t_len), F32)
    if prompt_layout:
        weights_q = [p["g_mix1"], p["w_q_t"], p["w_v_t"]]
        out_specs = [tok, tok, tok, headrow, headrow,
                     pl.BlockSpec((1, N_HEADS * AUG, tm), lambda b, t: (b, 0, t)),
                     pl.BlockSpec((1, tm, N_HEADS * AUG), lambda b, t: (b, t, 0)),
                     pl.BlockSpec((1, N_HEADS * VT_ROWS, tm), lambda b, t: (b, 0, t))]
        out_shape = [tok_f32] * 3 + [head_f32] * 2 + [
            jax.ShapeDtypeStruct((bsz, N_HEADS * AUG, t_len), BF16),
            jax.ShapeDtypeStruct((bsz, t_len, N_HEADS * AUG), BF16),
            jax.ShapeDtypeStruct((bsz, N_HEADS * VT_ROWS, t_len), BF16)]
    else:
        weights_q = [p["g_mix1"], p["w_q"]]
        out_specs = [tok, tok, tok, headrow, headrow, tok, tok, tok]
        out_shape = [tok_f32] * 3 + [head_f32] * 2 + [jax.ShapeDtypeStruct((bsz, t_len, d), BF16)] * 3
    return pl.pallas_call(
        functools.partial(_post0_kernel, tm=tm, seg=seg, carry=carry, prompt_layout=prompt_layout),
        grid=(bsz, t_len // tm),
        in_specs=[tok] + _gather_in_specs(bsz * nt, tm, lambda b, t: b * nt + t)
                 + [_mod_spec(g2, tm), _mod_spec(shk, tm), _mod_spec(sck, tm)]
                 + [_full_spec(w) for w in weights_kv]
                 + [_mod_spec(shq, tm), _mod_spec(scq, tm)] + [_full_spec(w) for w in weights_q],
        out_specs=out_specs,
        out_shape=out_shape,
        scratch_shapes=_gather_scratch(tm) + ([pltpu.VMEM((N_HEADS, 1), F32)] if carry else []),
        compiler_params=_params("arbitrary", "arbitrary"),
        name="shared_kv_and_q",
    )(x, y, y_rows, y_rows, g2, shk, sck, *weights_kv, shq, scq, *weights_q)


def _fox_prompt_kernel(qi_ref, kj_ref, qa_ref, ka_ref, vt_ref, o_ref, m_scr, acc_scr, *, tq):
    p = pl.program_id(1)
    i = qi_ref[p]
    j = kj_ref[p]

    @pl.when(j == 0)
    def _():
        m_scr[...] = jnp.full_like(m_scr, -jnp.inf)
        acc_scr[...] = jnp.zeros_like(acc_scr)

    def step(masked):
        if masked:
            key = lax.broadcasted_iota(I32, (tq, tq), 0)
            qry = lax.broadcasted_iota(I32, (tq, tq), 1)
        for h in range(N_HEADS):
            hv = slice(h * VT_ROWS, (h + 1) * VT_ROWS)
            s = jnp.dot(ka_ref[0, :, h * AUG:(h + 1) * AUG], qa_ref[0, h * AUG:(h + 1) * AUG, :],
                        preferred_element_type=F32)
            if masked:
                s = jnp.where(key <= qry, s, -jnp.inf)
            m_old = m_scr[h]
            m_new = jnp.maximum(m_old, jnp.max(s, axis=0, keepdims=True))
            alpha = jnp.exp2(m_old - m_new)
            pr = jnp.exp2(s - m_new).astype(BF16)
            acc_scr[hv, :] = alpha * acc_scr[hv, :] + jnp.dot(vt_ref[0, hv, :], pr,
                                                              preferred_element_type=F32)
            m_scr[h] = m_new

    @pl.when(j < i)
    def _():
        step(False)

    @pl.when(j == i)
    def _():
        step(True)
        for h in range(N_HEADS):
            base = h * VT_ROWS
            out = acc_scr[base:base + HEAD_DIM, :] / acc_scr[base + HEAD_DIM:base + HEAD_DIM + 1, :]
            o_ref[0, :, h * HEAD_DIM:(h + 1) * HEAD_DIM] = jnp.transpose(out).astype(BF16)


def _fox_prompt(qa, ka, vt):
    bsz, _, t_len = vt.shape
    d = N_HEADS * HEAD_DIM
    tq = ATTN_TILE
    nb = t_len // tq
    pairs = [(i, j) for i in range(nb) for j in range(i + 1)]
    qi = jnp.asarray([a for a, _ in pairs], I32)
    kj = jnp.asarray([b for _, b in pairs], I32)
    grid_spec = pltpu.PrefetchScalarGridSpec(
        num_scalar_prefetch=2,
        grid=(bsz, len(pairs)),
        in_specs=[
            pl.BlockSpec((1, N_HEADS * AUG, tq), lambda b, p, qi, kj: (b, 0, qi[p])),
            pl.BlockSpec((1, tq, N_HEADS * AUG), lambda b, p, qi, kj: (b, kj[p], 0)),
            pl.BlockSpec((1, N_HEADS * VT_ROWS, tq), lambda b, p, qi, kj: (b, 0, kj[p])),
        ],
        out_specs=pl.BlockSpec((1, tq, d), lambda b, p, qi, kj: (b, qi[p], 0)),
        scratch_shapes=[pltpu.VMEM((N_HEADS, 1, tq), F32),
                        pltpu.VMEM((N_HEADS * VT_ROWS, tq), F32)],
    )
    return pl.pallas_call(
        functools.partial(_fox_prompt_kernel, tq=tq),
        grid_spec=grid_spec,
        out_shape=jax.ShapeDtypeStruct((bsz, t_len, d), BF16),
        compiler_params=_params("arbitrary", "arbitrary"),
        name="fox_prompt_attention",
    )(qi, kj, qa, ka, vt)


def _page_suffix(lf):
    lane = lax.broadcasted_iota(I32, lf.shape, 1)
    sub = lax.broadcasted_iota(I32, lf.shape, 0)
    x = lf
    sh = N_HEADS
    while sh < LANES:
        x = x + jnp.where(lane + sh < LANES, pltpu.roll(x, LANES - sh, axis=1), 0.0)
        sh *= 2
    y = jnp.where(lane < N_HEADS, x, 0.0)
    sh = N_HEADS
    while sh < LANES:
        y = y + pltpu.roll(y, sh, axis=1)
        sh *= 2
    z = y
    sh = 1
    while sh < SUBLANES:
        z = z + jnp.where(sub + sh < SUBLANES, pltpu.roll(z, SUBLANES - sh, axis=0), 0.0)
        sh *= 2
    return x + (z - y) - lf, z[0:1]


def _fox_decode_kernel(pt_ref, q_ref, kn_ref, vn_ref, cq_ref, ck_ref, *refs, pages):
    del pt_ref
    k_refs = refs[:pages]
    v_refs = refs[pages:2 * pages]
    lf_refs = refs[2 * pages:3 * pages]
    o_ref = refs[3 * pages]
    qm_scr, m_scr, l_scr, acc_scr, r_scr = refs[3 * pages + 1:]
    step = pl.program_id(1)
    n_tok = q_ref.shape[1]
    rows = n_tok * N_HEADS
    page_size = k_refs[0].shape[1]
    lane = lax.broadcasted_iota(I32, (rows, LANES), 1)
    row = lax.broadcasted_iota(I32, (rows, LANES), 0)
    row_head = row // n_tok
    row_tok = row - row_head * n_tok
    cq = cq_ref[0]

    def by_head(x):
        return jnp.concatenate(
            [x[:, h * HEAD_DIM:(h + 1) * HEAD_DIM] for h in range(N_HEADS)], axis=0)

    def attend(u, v_bf):
        m_old = m_scr[...]
        m_new = jnp.maximum(m_old, jnp.max(u, axis=1, keepdims=True) + cq)
        alpha = jnp.exp(m_old - m_new)
        pr = jnp.exp(u - (m_new - cq))
        l_scr[...] = alpha * l_scr[...] + jnp.sum(pr, axis=1, keepdims=True)
        acc_scr[...] = alpha * acc_scr[...] + jnp.dot(pr.astype(BF16), v_bf,
                                                      preferred_element_type=F32)
        m_scr[...] = m_new

    @pl.when(step == 0)
    def _():
        qm = by_head(q_ref[0].astype(F32)).astype(BF16)
        qm_scr[...] = qm
        m_scr[...] = jnp.full_like(m_scr, -jnp.inf)
        l_scr[...] = jnp.zeros_like(l_scr)
        acc_scr[...] = jnp.zeros_like(acc_scr)
        r_scr[...] = jnp.zeros_like(r_scr)
        pad = jnp.zeros((LANES - rows, HEAD_DIM), F32)
        kn = jnp.concatenate([by_head(kn_ref[0].astype(F32)), pad], axis=0).astype(BF16)
        vn = jnp.concatenate([by_head(vn_ref[0].astype(F32)), pad], axis=0).astype(BF16)
        s = lax.dot_general(qm, kn, NT_DIMS, preferred_element_type=F32)
        key_head = lane // n_tok
        key_tok = lane - key_head * n_tok
        u = jnp.where(key_head == row_head, jnp.where(key_tok <= row_tok, s - ck_ref[0], -jnp.inf),
                      -jnp.inf)
        attend(u, vn)

    own_head = (lane & (N_HEADS - 1)) == row_head
    qm = qm_scr[...]
    after_sum = r_scr[...]
    u_parts, v_parts = [], []
    for pg in range(pages):
        later, total = _page_suffix(lf_refs[pg][0])
        after = after_sum + later
        after_sum = after_sum + total
        k2 = k_refs[pg][0].reshape(page_size * N_HEADS, HEAD_DIM).astype(BF16)
        v_parts.append(v_refs[pg][0].reshape(page_size * N_HEADS, HEAD_DIM).astype(BF16))
        s = lax.dot_general(qm, k2, NT_DIMS, preferred_element_type=F32)
        u_parts += [jnp.where(own_head, s[:, c * LANES:(c + 1) * LANES] + after[c:c + 1], -jnp.inf)
                    for c in range(page_size * N_HEADS // LANES)]
    r_scr[...] = after_sum
    attend(jnp.concatenate(u_parts, axis=1), jnp.concatenate(v_parts, axis=0))

    @pl.when(step == pl.num_programs(1) - 1)
    def _():
        o = acc_scr[...] / l_scr[...]
        o_ref[0] = jnp.concatenate(
            [o[h * n_tok:(h + 1) * n_tok] for h in range(N_HEADS)], axis=1).astype(BF16)


def _fox_decode(q, k_new, v_new, cq, ck, cache_k, cache_v, cache_lf, page_table):
    bsz, n_tok, d = q.shape
    n_pages = page_table.shape[1]
    page_size = cache_k.shape[1]
    pages = PAGES_PER_STEP
    rows = n_tok * N_HEADS

    def page_map(pg, ndim):
        def index_map(b, s, pt):
            return (pt[b, n_pages - 1 - (s * pages + pg)],) + (0,) * (ndim - 1)
        return index_map

    seq = lambda b, s, pt: (b, 0, 0)
    in_specs = [pl.BlockSpec((1, n_tok, d), seq), pl.BlockSpec((1, n_tok, d), seq),
                pl.BlockSpec((1, n_tok, d), seq), pl.BlockSpec((1, rows, 1), seq),
                pl.BlockSpec((1, 1, LANES), seq)]
    kv_block = (1, page_size, N_HEADS, HEAD_DIM)
    in_specs += [pl.BlockSpec(kv_block, page_map(pg, 4)) for pg in range(pages)]
    in_specs += [pl.BlockSpec(kv_block, page_map(pg, 4)) for pg in range(pages)]
    in_specs += [pl.BlockSpec((1, SUBLANES, LANES), page_map(pg, 3)) for pg in range(pages)]
    grid_spec = pltpu.PrefetchScalarGridSpec(
        num_scalar_prefetch=1,
        grid=(bsz, n_pages // pages),
        in_specs=in_specs,
        out_specs=pl.BlockSpec((1, n_tok, d), seq),
        scratch_shapes=[pltpu.VMEM((rows, HEAD_DIM), BF16), pltpu.VMEM((rows, 1), F32),
                        pltpu.VMEM((rows, 1), F32), pltpu.VMEM((rows, HEAD_DIM), F32),
                        pltpu.VMEM((1, LANES), F32)],
    )
    return pl.pallas_call(
        functools.partial(_fox_decode_kernel, pages=pages),
        grid_spec=grid_spec,
        out_shape=jax.ShapeDtypeStruct((bsz, n_tok, d), BF16),
        compiler_params=_params("arbitrary", "arbitrary"),
        name="fox_decode_attention",
    )(page_table, q, k_new, v_new, cq, ck, *([cache_k] * pages), *([cache_v] * pages),
      *([cache_lf] * pages))


def _oproj_kernel(x_ref, o_ref, g1_ref, wo_ref, xo_ref):
    xo_ref[0] = x_ref[0] + g1_ref[0] * jnp.dot(o_ref[0], wo_ref[...], preferred_element_type=F32)


def _oproj(x, o, g1, w_o, *, tm):
    bsz, t_len, d = x.shape
    tok = pl.BlockSpec((1, tm, d), lambda b, t: (b, t, 0))
    return pl.pallas_call(
        _oproj_kernel,
        grid=(bsz, t_len // tm),
        in_specs=[tok, tok, _mod_spec(g1, tm), _full_spec(w_o)],
        out_specs=tok,
        out_shape=jax.ShapeDtypeStruct((bsz, t_len, d), F32),
        compiler_params=_params("arbitrary", "arbitrary"),
        name="attn_out_proj",
    )(x, o, g1, w_o)


def _final_kernel(x_ref, y_hbm, pos0_ref, posn_ref, g2_ref, g_ref, o_ref, ybuf, sem_y):
    step, n_steps = _grid_step()
    y = _gathered_tile(step, step == n_steps - 1, y_hbm, pos0_ref, posn_ref, ybuf, sem_y,
                       x_ref.shape[1])
    x = x_ref[0] + g2_ref[0] * y
    o_ref[0] = (x * lax.rsqrt(jnp.mean(x * x, axis=-1, keepdims=True) + EPS)) * g_ref[...]


def _final(x, y, y_rows, g2, g_final, *, tm):
    bsz, t_len, d = x.shape
    nt = t_len // tm
    tok = pl.BlockSpec((1, tm, d), lambda b, t: (b, t, 0))
    return pl.pallas_call(
        _final_kernel,
        grid=(bsz, nt),
        in_specs=[tok] + _gather_in_specs(bsz * nt, tm, lambda b, t: b * nt + t)
                 + [_mod_spec(g2, tm), _full_spec(g_final)],
        out_specs=tok,
        out_shape=jax.ShapeDtypeStruct((bsz, t_len, d), F32),
        scratch_shapes=_gather_scratch(tm),
        compiler_params=_params("arbitrary", "arbitrary"),
        name="final_norm",
    )(x, y, y_rows, y_rows, g2, g_final)


def _trunk(x, mods0, mods1, kvmods, prev, h0, w, attend, *, tm, tmx, seg, carry):
    sh1, sc1, g1, sh2, sc2, g2 = mods0
    x, h_out, conv_out = _lru_layer(x, sh1, sc1, g1, w["lru"], prev, h0, tm=tm, seg=seg, carry=carry)
    y, y_rows = _moe(x, sh2, sc2, w["moe0"], tm=tm, tmx=tmx)
    sh1b, sc1b, g1b, sh2b, sc2b, g2b = mods1
    shk, sck = kvmods
    x, k, v, lft, fct, *operands = _post0(x, y, y_rows, g2, shk, sck, sh1b, sc1b, w["post0"],
                                          tm=tm, seg=seg, carry=carry, prompt_layout=carry)
    o = attend(fct, *operands)
    x = _oproj(x, o, g1b, w["w_o"], tm=tm)
    y, y_rows = _moe(x, sh2b, sc2b, w["moe1"], tm=tm, tmx=tmx)
    y_out = _final(x, y, y_rows, g2b, w["g_final"], tm=tm)
    return y_out, h_out, conv_out, k, v, lft


def kernel(x_prompt, x_sample, state_h, state_conv, cache_k, cache_v, cache_logf, page_table,
           c_prompt, c_sample, w_mod, b_mod, g_mix, g_moe, lru_w_in, lru_conv_w, lru_conv_b,
           lru_w_a, lru_b_a, lru_w_x, lru_b_x, lru_lambda, lru_w_out, kv_g, kv_w_mod, kv_b_mod,
           kv_w, kv_b_f, attn_w_q, attn_w_o, moe_w_grp, moe_b_grp, moe_w_exp, moe_b_exp,
           moe_w_up, moe_w_down, g_final):
    d = D_MODEL
    bp, tp, _ = x_prompt.shape
    bs, ts, _ = x_sample.shape
    hd = N_HEADS * HEAD_DIM
    row = lambda a: a.reshape(1, -1)

    def moe_weights(layer):
        wr_t = jnp.concatenate(
            [moe_w_grp[layer].T, moe_w_exp[layer].T,
             jnp.zeros((ROUTER_ROWS - N_GROUPS - N_EXPERTS, d), F32)], axis=0)
        br = jnp.concatenate(
            [moe_b_grp[layer], moe_b_exp[layer],
             jnp.zeros((ROUTER_ROWS - N_GROUPS - N_EXPERTS,), F32)]).reshape(ROUTER_ROWS, 1)
        return dict(g_moe=row(g_moe[layer]), wr_t=wr_t, br=br,
                    w_up=moe_w_up[layer].astype(BF16), w_down=moe_w_down[layer].astype(BF16))

    w = dict(
        lru=dict(g_mix=row(g_mix[0]), w_in=lru_w_in[0].astype(BF16), conv_w=lru_conv_w[0],
                 conv_b=row(lru_conv_b[0]), w_a=lru_w_a[0].astype(BF16), b_a=row(lru_b_a[0]),
                 w_x=lru_w_x[0].astype(BF16), b_x=row(lru_b_x[0]), lam=row(lru_lambda[0]),
                 w_out=lru_w_out[0].astype(BF16)),
        moe0=moe_weights(0),
        moe1=moe_weights(1),
        post0=dict(kv_g=row(kv_g), w_k=kv_w[:, :hd].astype(BF16),
                   w_v=kv_w[:, hd:2 * hd].astype(BF16), w_f_t=kv_w[:, 2 * hd:].T,
                   w_v_t=kv_w[:, hd:2 * hd].T.astype(BF16), b_f=kv_b_f.reshape(N_HEADS, 1),
                   g_mix1=row(g_mix[1]), w_q=attn_w_q[0].astype(BF16),
                   w_q_t=attn_w_q[0].T.astype(BF16)),
        w_o=attn_w_o[0].astype(BF16),
        g_final=row(g_final),
    )

    n_c = bp + bs
    c_rows = -(-n_c // SUBLANES) * SUBLANES
    c_all = jnp.concatenate([c_prompt, c_sample, jnp.zeros((c_rows - n_c, d), F32)], axis=0)
    mod_l0 = _mm_bias(c_all, w_mod[0], b_mod[0])
    mod_l1 = _mm_bias(c_all, w_mod[1], b_mod[1])
    mod_kv = _mm_bias(c_all, kv_w_mod, kv_b_mod)

    def prompt_mods(m, parts):
        return [a.reshape(bp, 1, d) for a in jnp.split(m[:bp], parts, axis=-1)]

    def sample_mods(m, parts):
        return [jnp.repeat(a, ts, axis=0).reshape(1, bs * ts, d)
                for a in jnp.split(m[bp:n_c], parts, axis=-1)]

    prev_p = jnp.zeros((bp, SUBLANES, D_RNN), F32)
    h0_p = jnp.zeros((bp, 1, D_RNN), F32)
    attend_p = lambda fct, qa, ka, vt: _fox_prompt(qa, ka, vt)
    y_p, h_p, conv_p, k_p, v_p, lft_p = _trunk(
        x_prompt, prompt_mods(mod_l0, 6), prompt_mods(mod_l1, 6), prompt_mods(mod_kv, 2),
        prev_p, h0_p, w, attend_p, tm=TOKEN_TILE, tmx=EXPERT_TILE, seg=TOKEN_TILE, carry=True)

    n_s = bs * ts
    prev_s = jnp.pad(state_conv[0], ((0, 0), (ts - (CONV_WIDTH - 1), 0), (0, 0))).reshape(1, n_s, D_RNN)
    h0_s = jnp.repeat(state_h[0], ts, axis=0).reshape(1, n_s, D_RNN)
    n_pool, page_size = cache_k.shape[0], cache_k.shape[1]
    cache_lf = cache_logf.reshape(n_pool, SUBLANES, page_size * N_HEADS // SUBLANES)

    def attend_s(fct, q, kb, vb):
        c_new = jnp.transpose(fct.reshape(N_HEADS, bs, ts), (1, 0, 2)).reshape(bs, N_HEADS * ts)
        cq = c_new.reshape(bs, N_HEADS * ts, 1)
        ck = jnp.pad(c_new, ((0, 0), (0, LANES - N_HEADS * ts))).reshape(bs, 1, LANES)
        o = _fox_decode(q.reshape(bs, ts, d), kb.reshape(bs, ts, d), vb.reshape(bs, ts, d), cq, ck,
                        cache_k, cache_v, cache_lf, page_table)
        return o.reshape(1, n_s, d)

    y_s, h_s, conv_s, k_s, v_s, lft_s = _trunk(
        x_sample.reshape(1, n_s, d), sample_mods(mod_l0, 6), sample_mods(mod_l1, 6),
        sample_mods(mod_kv, 2), prev_s, h0_s, w, attend_s, tm=n_s, tmx=SAMPLE_EXPERT_TILE, seg=ts,
        carry=False)

    n_conv = CONV_WIDTH - 1
    return (
        y_p,
        y_s.reshape(bs, ts, d),
        h_p.reshape(1, bp, D_RNN),
        conv_p[:, SUBLANES - n_conv:].reshape(1, bp, n_conv, D_RNN),
        k_p.reshape(bp, tp, N_HEADS, HEAD_DIM),
        v_p.reshape(bp, tp, N_HEADS, HEAD_DIM),
        jnp.transpose(lft_p, (0, 2, 1)),
        h_s.reshape(bs, ts, D_RNN)[:, ts - 1].reshape(1, bs, D_RNN),
        conv_s.reshape(bs, ts, D_RNN)[:, ts - n_conv:].reshape(1, bs, n_conv, D_RNN),
        k_s.reshape(bs, ts, N_HEADS, HEAD_DIM),
        v_s.reshape(bs, ts, N_HEADS, HEAD_DIM),
        jnp.transpose(lft_s.reshape(N_HEADS, bs, ts), (1, 2, 0)),
    )
```

```python
import functools
import math

import jax
import jax.numpy as jnp
from jax import lax
from jax.experimental import pallas as pl
from jax.experimental.pallas import tpu as pltpu

F32 = jnp.float32
BF16 = jnp.bfloat16
I32 = jnp.int32

D_MODEL = 1024
D_RNN = D_MODEL
N_LRU_BLOCKS = 8
LRU_BLOCK = D_RNN // N_LRU_BLOCKS
CONV_WIDTH = 4
LRU_C = 8.0
N_HEADS = 8
HEAD_DIM = D_MODEL // N_HEADS
N_GROUPS = 4
EXPERTS_PER_GROUP = 4
N_EXPERTS = N_GROUPS * EXPERTS_PER_GROUP
D_EXPERT = D_MODEL // 2
EPS = 1e-6
N_PAIRS = 6
N_CLASSES = N_GROUPS * N_PAIRS
PAIR_LO = (0, 0, 0, 1, 1, 2)
PAIR_HI = (1, 2, 3, 2, 3, 3)

SUBLANES = 8
LANES = 128
VMEM_LIMIT = 48 * 1024 * 1024
DMA_THREADS = 2

TOKEN_TILE = 256
EXPERT_TILE = 256
SAMPLE_EXPERT_TILE = 32
ROUTER_ROWS = 32
ATTN_TILE = 512
PAGES_PER_STEP = 16
MOD_TILE = 512

HIGHEST = lax.Precision.HIGHEST
NT_DIMS = (((1,), (1,)), ((), ()))


def _params(*semantics):
    return pltpu.CompilerParams(dimension_semantics=semantics, vmem_limit_bytes=VMEM_LIMIT)


def _norm_mod(x, g, shift, scale):
    y = x * lax.rsqrt(jnp.mean(x * x, axis=-1, keepdims=True) + EPS)
    return (y * g) * (1.0 + scale) + shift


def _log_sigmoid(x):
    return jnp.minimum(x, 0.0) - jnp.log1p(jnp.exp(-jnp.abs(x)))


def _sigmoid(x):
    return 1.0 / (1.0 + jnp.exp(-x))


def _gelu_tanh(x):
    c = math.sqrt(2.0 / math.pi)
    return x * (0.5 * (1.0 + jnp.tanh(c * (x + 0.044715 * (x * x * x)))))


def _mod_spec(arr, tm):
    if arr.shape[1] == 1:
        return pl.BlockSpec((1, 1, arr.shape[2]), lambda b, t: (b, 0, 0))
    return pl.BlockSpec((1, tm, arr.shape[2]), lambda b, t: (b, t, 0))


ROW_TILES = D_MODEL // LANES


def _row_spec(nt, tm):
    return pl.BlockSpec((tm * ROW_TILES, LANES), lambda b, t: (b * nt + t, 0))


def _to_token_tiles(ref, lead, x):
    rows = x.shape[0]
    for c in range(ROW_TILES):
        ref[(*lead, pl.ds(c, rows, stride=ROW_TILES), slice(None))] = x[:, c * LANES:(c + 1) * LANES]


def _from_token_tiles(ref, lead, rows):
    return jnp.concatenate(
        [ref[(*lead, pl.ds(c, rows, stride=ROW_TILES), slice(None))] for c in range(ROW_TILES)],
        axis=1)


def _start_token_gather(src_hbm, idx_ref, buf, slot, sem, count):
    for r in range(count):
        rows = pl.ds(pl.multiple_of(idx_ref[0, 0, r], ROW_TILES), ROW_TILES)
        pltpu.make_async_copy(src_hbm.at[rows], buf.at[slot, pl.ds(r * ROW_TILES, ROW_TILES)],
                              sem.at[slot]).start(priority=r % DMA_THREADS)


def _wait_token_gather(src_hbm, buf, slot, sem, count):
    pltpu.make_async_copy(src_hbm.at[pl.ds(0, count * ROW_TILES)], buf.at[slot], sem.at[slot]).wait()


GATHER_AHEAD = 2
GATHER_SLOTS = GATHER_AHEAD + 1


def _gathered_tile(step, is_last, src_hbm, idx0_ref, idx1_ref, idxa_ref, buf, sem, count):
    slot = lax.rem(step, GATHER_SLOTS)

    @pl.when(step == 0)
    def _():
        _start_token_gather(src_hbm, idx0_ref, buf, 0, sem, count)
        _start_token_gather(src_hbm, idx1_ref, buf, 1, sem, count)

    _wait_token_gather(src_hbm, buf, slot, sem, count)
    _start_token_gather(src_hbm, idxa_ref, buf, lax.rem(step + GATHER_AHEAD, GATHER_SLOTS), sem, count)
    x = _from_token_tiles(buf, (slot,), count)

    @pl.when(is_last)
    def _():
        for ahead in range(1, GATHER_SLOTS):
            _wait_token_gather(src_hbm, buf, lax.rem(step + ahead, GATHER_SLOTS), sem, count)

    return x


def _gather_in_specs(n_steps, count, step_of):
    block = (1, 1, count)
    at_step = lambda s: (jnp.minimum(s, n_steps - 1), 0, 0)
    return [pl.BlockSpec(memory_space=pl.ANY),
            pl.BlockSpec(block, lambda *g: at_step(0), memory_space=pltpu.SMEM),
            pl.BlockSpec(block, lambda *g: at_step(1), memory_space=pltpu.SMEM),
            pl.BlockSpec(block, lambda *g: at_step(step_of(*g) + GATHER_AHEAD),
                         memory_space=pltpu.SMEM)]


def _gather_scratch(count):
    return [pltpu.VMEM((GATHER_SLOTS, count * ROW_TILES, LANES), F32),
            pltpu.SemaphoreType.DMA((GATHER_SLOTS,))]


def _full_spec(arr):
    zeros = (0,) * arr.ndim
    return pl.BlockSpec(arr.shape, lambda *_: zeros)


def _mm_bias_kernel(x_ref, w_ref, b_ref, o_ref):
    o_ref[...] = jnp.dot(x_ref[...].astype(BF16), w_ref[...].astype(BF16),
                         preferred_element_type=F32) + b_ref[...]


def _mm_bias(x, w, b):
    m, d = x.shape
    n = w.shape[1]
    return pl.pallas_call(
        _mm_bias_kernel,
        grid=(n // MOD_TILE,),
        in_specs=[pl.BlockSpec((m, d), lambda j: (0, 0)),
                  pl.BlockSpec((d, MOD_TILE), lambda j: (0, j)),
                  pl.BlockSpec((1, MOD_TILE), lambda j: (0, j))],
        out_specs=pl.BlockSpec((m, MOD_TILE), lambda j: (0, j)),
        out_shape=jax.ShapeDtypeStruct((m, n), F32),
        compiler_params=_params("arbitrary"),
        name="mod_vectors",
    )(x, w, b.reshape(1, n))


def _lru_kernel(x_ref, sh_ref, sc_ref, gt_ref, gmix_ref, win_ref, cw_ref, cb_ref, wa_ref, ba_ref,
                wx_ref, bx_ref, lam_ref, wout_ref, prev_ref, h0_ref,
                xo_ref, hs_ref, xb_ref, *scratch, tm, seg, carry):
    x = x_ref[0]
    hn = _norm_mod(x, gmix_ref[...], sh_ref[0], sc_ref[0])
    xg = jnp.dot(hn.astype(BF16), win_ref[...], preferred_element_type=F32)
    xb = xg[:, :D_RNN]
    gb = xg[:, D_RNN:]

    if carry:
        prev_scr, h_scr = scratch

        @pl.when(pl.program_id(1) == 0)
        def _():
            prev_scr[...] = prev_ref[0]
            h_scr[...] = h0_ref[0]

        prev = prev_scr[...]
        hprev = h_scr[...]
    else:
        prev = prev_ref[0]
        hprev = h0_ref[0]

    row = lax.broadcasted_iota(I32, (tm, 1), 0)
    rseg = row & (seg - 1)
    nprev = prev.shape[0]

    cw = cw_ref[...]
    xc = cb_ref[...] + cw[CONV_WIDTH - 1:CONV_WIDTH] * xb
    for d in range(1, CONV_WIDTH):
        rolled = pltpu.roll(xb, d, axis=0)
        rp = pltpu.roll(prev, nprev - SUBLANES + d, axis=0)
        if carry:
            head = jnp.where(row[:SUBLANES] < d, rp, rolled[:SUBLANES])
            shifted = head if tm == SUBLANES else jnp.concatenate([head, rolled[SUBLANES:]], axis=0)
        else:
            shifted = jnp.where(rseg < d, rp, rolled)
        xc = xc + cw[CONV_WIDTH - 1 - d:CONV_WIDTH - d] * shifted

    xcb = xc.astype(BF16)
    ra = jnp.concatenate(
        [jnp.dot(xcb[:, n * LRU_BLOCK:(n + 1) * LRU_BLOCK], wa_ref[n], preferred_element_type=F32)
         for n in range(N_LRU_BLOCKS)], axis=1)
    rx = jnp.concatenate(
        [jnp.dot(xcb[:, n * LRU_BLOCK:(n + 1) * LRU_BLOCK], wx_ref[n], preferred_element_type=F32)
         for n in range(N_LRU_BLOCKS)], axis=1)
    r = _sigmoid(ra + ba_ref[...])
    ig = _sigmoid(rx + bx_ref[...])
    log_a = (LRU_C * r) * _log_sigmoid(lam_ref[...])
    a = jnp.exp(log_a)
    bt = jnp.sqrt(-jnp.tanh(log_a) * (a * a + 1.0)) * (ig * xc)

    s = 1
    while s < seg:
        inside = rseg >= s
        a_sh = jnp.where(inside, pltpu.roll(a, s, axis=0), 1.0)
        b_sh = jnp.where(inside, pltpu.roll(bt, s, axis=0), 0.0)
        bt = a * b_sh + bt
        a = a * a_sh
        s *= 2
    hs = bt + a * hprev

    if carry:
        h_scr[...] = hs[tm - 1:tm]
        prev_scr[...] = xb[tm - SUBLANES:]
        hs_ref[0] = hs[tm - 1:tm]
        xb_ref[0] = xb[tm - SUBLANES:]
    else:
        hs_ref[0] = hs
        xb_ref[0] = xb

    y = (hs * _gelu_tanh(gb)).astype(BF16)
    out = jnp.dot(y, wout_ref[...], preferred_element_type=F32)
    xo_ref[0] = x + gt_ref[0] * out


def _lru_layer(x, sh, sc, gt, p, prev, h0, *, tm, seg, carry):
    bsz, t_len, d = x.shape
    r = D_RNN
    nt = t_len // tm
    tok = pl.BlockSpec((1, tm, d), lambda b, t: (b, t, 0))
    if carry:
        state_specs = [pl.BlockSpec((1, SUBLANES, r), lambda b, t: (b, 0, 0)),
                       pl.BlockSpec((1, 1, r), lambda b, t: (b, 0, 0))]
        out_specs = [tok,
                     pl.BlockSpec((1, 1, r), lambda b, t: (b, 0, 0)),
                     pl.BlockSpec((1, SUBLANES, r), lambda b, t: (b, 0, 0))]
        out_shape = [jax.ShapeDtypeStruct((bsz, t_len, d), F32),
                     jax.ShapeDtypeStruct((bsz, 1, r), F32),
                     jax.ShapeDtypeStruct((bsz, SUBLANES, r), F32)]
        scratch = [pltpu.VMEM((SUBLANES, r), F32), pltpu.VMEM((1, r), F32)]
    else:
        rtok = pl.BlockSpec((1, tm, r), lambda b, t: (b, t, 0))
        state_specs = [rtok, rtok]
        out_specs = [tok, rtok, rtok]
        out_shape = [jax.ShapeDtypeStruct((bsz, t_len, d), F32),
                     jax.ShapeDtypeStruct((bsz, t_len, r), F32),
                     jax.ShapeDtypeStruct((bsz, t_len, r), F32)]
        scratch = []
    weights = [p["g_mix"], p["w_in"], p["conv_w"], p["conv_b"], p["w_a"], p["b_a"], p["w_x"],
               p["b_x"], p["lam"], p["w_out"]]
    return pl.pallas_call(
        functools.partial(_lru_kernel, tm=tm, seg=seg, carry=carry),
        grid=(bsz, nt),
        in_specs=[tok, _mod_spec(sh, tm), _mod_spec(sc, tm), _mod_spec(gt, tm)]
                 + [_full_spec(w) for w in weights] + state_specs,
        out_specs=out_specs,
        out_shape=out_shape,
        scratch_shapes=scratch,
        compiler_params=_params("arbitrary", "arbitrary"),
        name="rglru_layer",
    )(x, sh, sc, gt, *weights, prev, h0)


def _router_kernel(x_ref, sh_ref, sc_ref, g_ref, wr_ref, br_ref,
                   hn_ref, cls_ref, rank_ref, glo_ref, ghi_ref, cnt_ref, carry_scr, *, tm):
    @pl.when((pl.program_id(0) == 0) & (pl.program_id(1) == 0))
    def _():
        carry_scr[...] = jnp.zeros_like(carry_scr)

    hn = _norm_mod(x_ref[0], g_ref[...], sh_ref[0], sc_ref[0])
    _to_token_tiles(hn_ref, (), hn)
    lt = lax.dot_general(wr_ref[...], hn, NT_DIMS, precision=HIGHEST,
                         preferred_element_type=F32) + br_ref[...]

    gl = [lt[k:k + 1] for k in range(N_GROUPS)]
    gmax = jnp.maximum(jnp.maximum(gl[0], gl[1]), jnp.maximum(gl[2], gl[3]))
    g_sel = jnp.where(gl[0] >= gmax, 0, jnp.where(gl[1] >= gmax, 1, jnp.where(gl[2] >= gmax, 2, 3)))
    p_grp = 1.0 / (jnp.exp(gl[0] - gmax) + jnp.exp(gl[1] - gmax)
                   + jnp.exp(gl[2] - gmax) + jnp.exp(gl[3] - gmax))

    def expert_logit(e):
        rows = [lt[N_GROUPS + g * EXPERTS_PER_GROUP + e:N_GROUPS + g * EXPERTS_PER_GROUP + e + 1]
                for g in range(N_GROUPS)]
        return jnp.where(g_sel == 0, rows[0],
                         jnp.where(g_sel == 1, rows[1], jnp.where(g_sel == 2, rows[2], rows[3])))

    es = [expert_logit(e) for e in range(EXPERTS_PER_GROUP)]

    def first_argmax(vals):
        vmax = jnp.maximum(jnp.maximum(vals[0], vals[1]), jnp.maximum(vals[2], vals[3]))
        idx = jnp.where(vals[0] >= vmax, 0,
                        jnp.where(vals[1] >= vmax, 1, jnp.where(vals[2] >= vmax, 2, 3)))
        return vmax, idx

    v1, i1 = first_argmax(es)
    rest = [jnp.where(i1 == e, -jnp.inf, es[e]) for e in range(EXPERTS_PER_GROUP)]
    v2, i2 = first_argmax(rest)
    e21 = jnp.exp(v2 - v1)
    w1 = (1.0 / (1.0 + e21)) * p_grp
    w2 = (e21 / (1.0 + e21)) * p_grp
    first_lower = i1 < i2
    lo = jnp.where(first_lower, i1, i2)
    hi = jnp.where(first_lower, i2, i1)
    glo_ref[0] = jnp.where(first_lower, w1, w2)
    ghi_ref[0] = jnp.where(first_lower, w2, w1)
    pair = jnp.where(lo == 0, hi - 1, jnp.where(lo == 1, hi + 1, 5))
    cls = g_sel * N_PAIRS + pair
    cls_ref[0] = cls

    crow = lax.broadcasted_iota(I32, (ROUTER_ROWS, tm), 0)
    onehot = jnp.where(crow == cls, 1.0, 0.0)
    ki = lax.broadcasted_iota(I32, (tm, tm), 0)
    kj = lax.broadcasted_iota(I32, (tm, tm), 1)
    upper = jnp.where(ki <= kj, 1.0, 0.0).astype(BF16)
    cum = jnp.dot(onehot.astype(BF16), upper, preferred_element_type=F32)
    carry = carry_scr[...]
    rank = jnp.sum(onehot * (cum - 1.0 + carry), axis=0, keepdims=True)
    rank_ref[0] = rank.astype(I32)
    carry = carry + cum[:, tm - 1:tm]
    carry_scr[...] = carry
    cnt_ref[...] = carry


def _router(x, sh, sc, g, wr_t, br, *, tm):
    bsz, t_len, d = x.shape
    nt = t_len // tm
    n_tiles = bsz * nt
    tok = pl.BlockSpec((1, tm, d), lambda b, t: (b, t, 0))
    lane_row = pl.BlockSpec((1, 1, tm), lambda b, t: (b * nt + t, 0, 0))
    row_shape = jax.ShapeDtypeStruct((n_tiles, 1, tm), F32)
    row_shape_i = jax.ShapeDtypeStruct((n_tiles, 1, tm), I32)
    return pl.pallas_call(
        functools.partial(_router_kernel, tm=tm),
        grid=(bsz, nt),
        in_specs=[tok, _mod_spec(sh, tm), _mod_spec(sc, tm), _full_spec(g), _full_spec(wr_t),
                  _full_spec(br)],
        out_specs=[_row_spec(nt, tm), lane_row, lane_row, lane_row, lane_row,
                   pl.BlockSpec((ROUTER_ROWS, 1), lambda b, t: (0, 0))],
        out_shape=[jax.ShapeDtypeStruct((bsz * t_len * ROW_TILES, LANES), F32), row_shape_i, row_shape_i,
                   row_shape, row_shape, jax.ShapeDtypeStruct((ROUTER_ROWS, 1), F32)],
        scratch_shapes=[pltpu.VMEM((ROUTER_ROWS, 1), F32)],
        compiler_params=_params("arbitrary", "arbitrary"),
        name="moe_router",
    )(x, sh, sc, g, wr_t, br)


def _expert_kernel(e1_ref, e2_ref, valid_ref, hn_hbm, src0_ref, src1_ref, srca_ref, glo_ref, ghi_ref,
                   wu1_ref, wd1_ref, wu2_ref, wd2_ref, o_ref, xbuf, sem_g, *, tmx):
    del e1_ref, e2_ref
    j = pl.program_id(0)
    last = pl.num_programs(0) - 1

    @pl.when(valid_ref[j] == 0)
    def _():
        o_ref[...] = jnp.zeros_like(o_ref)

    @pl.when(valid_ref[j] == 1)
    def _():
        is_last = (j == last) | (valid_ref[jnp.minimum(j + 1, last)] == 0)
        x = _gathered_tile(j, is_last, hn_hbm, src0_ref, src1_ref, srca_ref, xbuf, sem_g,
                           tmx).astype(BF16)
        ri = lax.broadcasted_iota(I32, (tmx, tmx), 0)
        ci = lax.broadcasted_iota(I32, (tmx, tmx), 1)

        def column(row_ref):
            return jnp.sum(jnp.where(ri == ci, row_ref[0], 0.0), axis=1, keepdims=True)

        def ffn(wu_ref, wd_ref, gate):
            gu = jnp.dot(x, wu_ref[0], preferred_element_type=F32)
            g_ = gu[:, :D_EXPERT]
            u_ = gu[:, D_EXPERT:]
            act = ((g_ * _sigmoid(g_)) * u_).astype(BF16)
            return gate * jnp.dot(act, wd_ref[0], preferred_element_type=F32)

        _to_token_tiles(o_ref, (), ffn(wu1_ref, wd1_ref, column(glo_ref))
                        + ffn(wu2_ref, wd2_ref, column(ghi_ref)))


def _experts(hn, src, glo_s, ghi_s, w_up, w_down, e1, e2, valid, n_tiles, tmx):
    d = D_MODEL
    idx_block = (1, 1, tmx)
    at_tile = lambda j, e1, e2, v: (j, 0, 0)
    grid_spec = pltpu.PrefetchScalarGridSpec(
        num_scalar_prefetch=3,
        grid=(n_tiles,),
        in_specs=_gather_in_specs(n_tiles, tmx, lambda j, *_: j) + [
            pl.BlockSpec(idx_block, at_tile),
            pl.BlockSpec(idx_block, at_tile),
            pl.BlockSpec((1, d, 2 * D_EXPERT), lambda j, e1, e2, v: (e1[j], 0, 0)),
            pl.BlockSpec((1, D_EXPERT, d), lambda j, e1, e2, v: (e1[j], 0, 0)),
            pl.BlockSpec((1, d, 2 * D_EXPERT), lambda j, e1, e2, v: (e2[j], 0, 0)),
            pl.BlockSpec((1, D_EXPERT, d), lambda j, e1, e2, v: (e2[j], 0, 0)),
        ],
        out_specs=pl.BlockSpec((tmx * ROW_TILES, LANES), lambda j, e1, e2, v: (j, 0)),
        scratch_shapes=_gather_scratch(tmx),
    )
    return pl.pallas_call(
        functools.partial(_expert_kernel, tmx=tmx),
        grid_spec=grid_spec,
        out_shape=jax.ShapeDtypeStruct((n_tiles * tmx * ROW_TILES, LANES), F32),
        compiler_params=_params("arbitrary"),
        name="moe_experts",
    )(e1, e2, valid, hn, src, src, src, glo_s, ghi_s, w_up, w_down, w_up, w_down)


def _moe(x, sh, sc, p, *, tm, tmx):
    bsz, t_len, d = x.shape
    n = bsz * t_len
    hn, cls, rank, glo, ghi, cnt = _router(x, sh, sc, p["g_moe"], p["wr_t"], p["br"], tm=tm)
    cls = cls.reshape(n)
    rank = rank.reshape(n)
    counts = cnt[:N_CLASSES, 0].astype(I32)
    padded = ((counts + tmx - 1) // tmx) * tmx
    ends = jnp.cumsum(padded)
    offs = ends - padded
    pos = offs[cls] + rank
    n_tiles = n // tmx + N_CLASSES
    n_rows = n_tiles * tmx
    total = ends[N_CLASSES - 1]
    tile_start = jnp.arange(n_tiles, dtype=I32) * tmx
    valid = (tile_start < total).astype(I32)
    last_tile = jnp.maximum(total // tmx - 1, 0)
    tile_cls = jnp.minimum(
        jnp.sum((ends[None, :] <= jnp.minimum(tile_start, last_tile * tmx)[:, None]).astype(I32),
                axis=1), N_CLASSES - 1)
    grp = tile_cls // N_PAIRS
    pair = tile_cls % N_PAIRS
    e1 = grp * EXPERTS_PER_GROUP + jnp.asarray(PAIR_LO, I32)[pair]
    e2 = grp * EXPERTS_PER_GROUP + jnp.asarray(PAIR_HI, I32)[pair]
    token_info = jnp.stack([jnp.arange(n, dtype=I32).astype(F32), glo.reshape(n), ghi.reshape(n)],
                           axis=1)
    row_info = jnp.zeros((n_rows, 3), F32).at[pos].set(token_info)
    as_tiles = lambda a: a.reshape(n_tiles, 1, tmx)
    src_rows = as_tiles(row_info[:, 0].astype(I32) * ROW_TILES)
    ys = _experts(hn, src_rows, as_tiles(row_info[:, 1]), as_tiles(row_info[:, 2]),
                  p["w_up"], p["w_down"], e1, e2, valid, n_tiles, tmx)
    return ys, (pos * ROW_TILES).reshape(n // tm, 1, tm)


AUG = 2 * HEAD_DIM
N_PIECES = 3
VT_ROWS = HEAD_DIM + 16
LOG2E = 1.0 / math.log(2.0)


def _bf16_pieces(x):
    pieces = []
    rest = x
    for _ in range(N_PIECES):
        piece = rest.astype(BF16).astype(F32)
        pieces.append(piece)
        rest = rest - piece
    return pieces


def _grid_step():
    n_steps = pl.num_programs(0) * pl.num_programs(1)
    return pl.program_id(0) * pl.num_programs(1) + pl.program_id(1), n_steps


def _post0_kernel(x_ref, y_hbm, pos0_ref, pos1_ref, posa_ref, g2_ref, shk_ref, sck_ref, kvg_ref,
                  wk_ref, wv_ref, wft_ref, bf_ref, shq_ref, scq_ref, gq_ref, wq_ref, *refs,
                  tm, seg, carry, prompt_layout):
    if prompt_layout:
        wvt_ref, x1_ref, k_ref, v_ref, lft_ref, fct_ref, qa_ref, ka_ref, vt_ref = refs[:9]
        scratch = refs[9:]
    else:
        x1_ref, k_ref, v_ref, lft_ref, fct_ref, q_ref, kb_ref, vb_ref = refs[:8]
        scratch = refs[8:]
    ybuf, sem_y = scratch[:2]
    scratch = scratch[2:]
    scale = HEAD_DIM ** -0.5
    step, n_steps = _grid_step()
    y = _gathered_tile(step, step == n_steps - 1, y_hbm, pos0_ref, pos1_ref, posa_ref, ybuf, sem_y,
                       tm)
    x1 = x_ref[0] + g2_ref[0] * y
    x1_ref[0] = x1
    hk = _norm_mod(x1, kvg_ref[...], shk_ref[0], sck_ref[0])
    hkb = hk.astype(BF16)
    k = jnp.dot(hkb, wk_ref[...], preferred_element_type=F32)
    v = jnp.dot(hkb, wv_ref[...], preferred_element_type=F32)
    k_ref[0] = k
    v_ref[0] = v
    f = lax.dot_general(wft_ref[...], hk, NT_DIMS, precision=HIGHEST,
                        preferred_element_type=F32) + bf_ref[...]
    lf = _log_sigmoid(f)
    lft_ref[0] = lf
    col = lax.broadcasted_iota(I32, (1, tm), 1)
    cseg = col & (seg - 1)
    c = lf
    s = 1
    while s < seg:
        c = c + jnp.where(cseg >= s, pltpu.roll(c, s, axis=1), 0.0)
        s *= 2
    if carry:
        (f_scr,) = scratch

        @pl.when(pl.program_id(1) == 0)
        def _():
            f_scr[...] = jnp.zeros_like(f_scr)

        c = c + f_scr[...]
        f_scr[...] = c[:, tm - 1:tm]
    fct_ref[0] = c
    hqb = _norm_mod(x1, gq_ref[...], shq_ref[0], scq_ref[0]).astype(BF16)

    if not prompt_layout:
        q_ref[0] = (jnp.dot(hqb, wq_ref[...], preferred_element_type=F32) * scale).astype(BF16)
        kb_ref[0] = k.astype(BF16)
        vb_ref[0] = v.astype(BF16)
        return

    qt = lax.dot_general(wq_ref[...], hqb, NT_DIMS, preferred_element_type=F32) * (scale * LOG2E)
    vt = lax.dot_general(wvt_ref[...], hkb, NT_DIMS, preferred_element_type=F32)
    ones_rows = jnp.where(lax.broadcasted_iota(I32, (VT_ROWS - HEAD_DIM, tm), 0) == 0, 1.0, 0.0)
    v_rows = []
    for h in range(N_HEADS):
        v_rows += [vt[h * HEAD_DIM:(h + 1) * HEAD_DIM], ones_rows]
    vt_ref[0] = jnp.concatenate(v_rows, axis=0).astype(BF16)
    hi, mid, lo = _bf16_pieces(c * LOG2E)
    sub = lax.broadcasted_iota(I32, (SUBLANES, tm), 0)
    pad = jnp.zeros((HEAD_DIM - SUBLANES, tm), F32)
    q_rows, k_rows = [], []
    for h in range(N_HEADS):
        hs = slice(h, h + 1)
        eq = jnp.where(sub < 3, -1.0, jnp.where(sub == 3, hi[hs], jnp.where(
            sub == 4, mid[hs], jnp.where(sub == 5, lo[hs], 0.0))))
        ek = jnp.where(sub == 0, hi[hs], jnp.where(sub == 1, mid[hs], jnp.where(
            sub == 2, lo[hs], jnp.where(sub < 6, 1.0, 0.0))))
        q_rows += [qt[h * HEAD_DIM:(h + 1) * HEAD_DIM], eq, pad]
        k_rows += [ek, pad]
    qa_ref[0] = jnp.concatenate(q_rows, axis=0).astype(BF16)
    ekt = jnp.concatenate(k_rows, axis=0).astype(BF16)
    ri = lax.broadcasted_iota(I32, (tm, tm), 0)
    ci = lax.broadcasted_iota(I32, (tm, tm), 1)
    eye = jnp.where(ri == ci, 1.0, 0.0).astype(BF16)
    extra = lax.dot_general(eye, ekt, NT_DIMS, preferred_element_type=F32)
    k_cols = []
    for h in range(N_HEADS):
        hs = slice(h * HEAD_DIM, (h + 1) * HEAD_DIM)
        k_cols += [k[:, hs], extra[:, hs]]
    ka_ref[0] = jnp.concatenate(k_cols, axis=1).astype(BF16)


def _post0(x, y, y_rows, g2, shk, sck, shq, scq, p, *, tm, seg, carry, prompt_layout):
    bsz, t_len, d = x.shape
    nt = t_len // tm
    tok = pl.BlockSpec((1, tm, d), lambda b, t: (b, t, 0))
    headrow = pl.BlockSpec((1, N_HEADS, tm), lambda b, t: (b, 0, t))
    weights_kv = [p["kv_g"], p["w_k"], p["w_v"], p["w_f_t"], p["b_f"]]
    tok_f32 = jax.ShapeDtypeStruct((bsz, t_len, d), F32)
    head_f32 = jax.ShapeDtypeStruct((bsz, N_HEADS, t_len), F32)
    if prompt_layout:
        weights_q = [p["g_mix1"], p["w_q_t"], p["w_v_t"]]
        out_specs = [tok, tok, tok, headrow, headrow,
                     pl.BlockSpec((1, N_HEADS * AUG, tm), lambda b, t: (b, 0, t)),
                     pl.BlockSpec((1, tm, N_HEADS * AUG), lambda b, t: (b, t, 0)),
                     pl.BlockSpec((1, N_HEADS * VT_ROWS, tm), lambda b, t: (b, 0, t))]
        out_shape = [tok_f32] * 3 + [head_f32] * 2 + [
            jax.ShapeDtypeStruct((bsz, N_HEADS * AUG, t_len), BF16),
            jax.ShapeDtypeStruct((bsz, t_len, N_HEADS * AUG), BF16),
            jax.ShapeDtypeStruct((bsz, N_HEADS * VT_ROWS, t_len), BF16)]
    else:
        weights_q = [p["g_mix1"], p["w_q"]]
        out_specs = [tok, tok, tok, headrow, headrow, tok, tok, tok]
        out_shape = [tok_f32] * 3 + [head_f32] * 2 + [jax.ShapeDtypeStruct((bsz, t_len, d), BF16)] * 3
    return pl.pallas_call(
        functools.partial(_post0_kernel, tm=tm, seg=seg, carry=carry, prompt_layout=prompt_layout),
        grid=(bsz, t_len // tm),
        in_specs=[tok] + _gather_in_specs(bsz * nt, tm, lambda b, t: b * nt + t)
                 + [_mod_spec(g2, tm), _mod_spec(shk, tm), _mod_spec(sck, tm)]
                 + [_full_spec(w) for w in weights_kv]
                 + [_mod_spec(shq, tm), _mod_spec(scq, tm)] + [_full_spec(w) for w in weights_q],
        out_specs=out_specs,
        out_shape=out_shape,
        scratch_shapes=_gather_scratch(tm) + ([pltpu.VMEM((N_HEADS, 1), F32)] if carry else []),
        compiler_params=_params("arbitrary", "arbitrary"),
        name="shared_kv_and_q",
    )(x, y, y_rows, y_rows, y_rows, g2, shk, sck, *weights_kv, shq, scq, *weights_q)


def _fox_prompt_kernel(qi_ref, kj_ref, qa_ref, ka_ref, vt_ref, o_ref, m_scr, acc_scr, *, tq):
    p = pl.program_id(1)
    i = qi_ref[p]
    j = kj_ref[p]

    @pl.when(j == 0)
    def _():
        m_scr[...] = jnp.full_like(m_scr, -jnp.inf)
        acc_scr[...] = jnp.zeros_like(acc_scr)

    def step(masked):
        if masked:
            key = lax.broadcasted_iota(I32, (tq, tq), 0)
            qry = lax.broadcasted_iota(I32, (tq, tq), 1)
        for h in range(N_HEADS):
            hv = slice(h * VT_ROWS, (h + 1) * VT_ROWS)
            s = jnp.dot(ka_ref[0, :, h * AUG:(h + 1) * AUG], qa_ref[0, h * AUG:(h + 1) * AUG, :],
                        preferred_element_type=F32)
            if masked:
                s = jnp.where(key <= qry, s, -jnp.inf)
            m_old = m_scr[h]
            m_new = jnp.maximum(m_old, jnp.max(s, axis=0, keepdims=True))
            alpha = jnp.exp2(m_old - m_new)
            pr = jnp.exp2(s - m_new).astype(BF16)
            acc_scr[hv, :] = alpha * acc_scr[hv, :] + jnp.dot(vt_ref[0, hv, :], pr,
                                                              preferred_element_type=F32)
            m_scr[h] = m_new

    @pl.when(j < i)
    def _():
        step(False)

    @pl.when(j == i)
    def _():
        step(True)
        for h in range(N_HEADS):
            base = h * VT_ROWS
            out = acc_scr[base:base + HEAD_DIM, :] / acc_scr[base + HEAD_DIM:base + HEAD_DIM + 1, :]
            o_ref[0, :, h * HEAD_DIM:(h + 1) * HEAD_DIM] = jnp.transpose(out).astype(BF16)


def _fox_prompt(qa, ka, vt):
    bsz, _, t_len = vt.shape
    d = N_HEADS * HEAD_DIM
    tq = ATTN_TILE
    nb = t_len // tq
    pairs = [(i, j) for i in range(nb) for j in range(i + 1)]
    qi = jnp.asarray([a for a, _ in pairs], I32)
    kj = jnp.asarray([b for _, b in pairs], I32)
    grid_spec = pltpu.PrefetchScalarGridSpec(
        num_scalar_prefetch=2,
        grid=(bsz, len(pairs)),
        in_specs=[
            pl.BlockSpec((1, N_HEADS * AUG, tq), lambda b, p, qi, kj: (b, 0, qi[p])),
            pl.BlockSpec((1, tq, N_HEADS * AUG), lambda b, p, qi, kj: (b, kj[p], 0)),
            pl.BlockSpec((1, N_HEADS * VT_ROWS, tq), lambda b, p, qi, kj: (b, 0, kj[p])),
        ],
        out_specs=pl.BlockSpec((1, tq, d), lambda b, p, qi, kj: (b, qi[p], 0)),
        scratch_shapes=[pltpu.VMEM((N_HEADS, 1, tq), F32),
                        pltpu.VMEM((N_HEADS * VT_ROWS, tq), F32)],
    )
    return pl.pallas_call(
        functools.partial(_fox_prompt_kernel, tq=tq),
        grid_spec=grid_spec,
        out_shape=jax.ShapeDtypeStruct((bsz, t_len, d), BF16),
        compiler_params=_params("arbitrary", "arbitrary"),
        name="fox_prompt_attention",
    )(qi, kj, qa, ka, vt)


def _page_suffix(lf):
    lane = lax.broadcasted_iota(I32, lf.shape, 1)
    sub = lax.broadcasted_iota(I32, lf.shape, 0)
    x = lf
    sh = N_HEADS
    while sh < LANES:
        x = x + jnp.where(lane + sh < LANES, pltpu.roll(x, LANES - sh, axis=1), 0.0)
        sh *= 2
    y = jnp.where(lane < N_HEADS, x, 0.0)
    sh = N_HEADS
    while sh < LANES:
        y = y + pltpu.roll(y, sh, axis=1)
        sh *= 2
    z = y
    sh = 1
    while sh < SUBLANES:
        z = z + jnp.where(sub + sh < SUBLANES, pltpu.roll(z, SUBLANES - sh, axis=0), 0.0)
        sh *= 2
    return x + (z - y) - lf, z[0:1]


def _fox_decode_kernel(pt_ref, q_ref, kn_ref, vn_ref, cq_ref, ck_ref, *refs, pages):
    del pt_ref
    k_refs = refs[:pages]
    v_refs = refs[pages:2 * pages]
    lf_refs = refs[2 * pages:3 * pages]
    o_ref = refs[3 * pages]
    qm_scr, m_scr, l_scr, acc_scr, r_scr = refs[3 * pages + 1:]
    step = pl.program_id(1)
    n_tok = q_ref.shape[1]
    rows = n_tok * N_HEADS
    page_size = k_refs[0].shape[1]
    lane = lax.broadcasted_iota(I32, (rows, LANES), 1)
    row = lax.broadcasted_iota(I32, (rows, LANES), 0)
    row_head = row // n_tok
    row_tok = row - row_head * n_tok
    cq = cq_ref[0]

    def by_head(x):
        return jnp.concatenate(
            [x[:, h * HEAD_DIM:(h + 1) * HEAD_DIM] for h in range(N_HEADS)], axis=0)

    def attend(u, v_bf):
        m_old = m_scr[...]
        m_new = jnp.maximum(m_old, jnp.max(u, axis=1, keepdims=True) + cq)
        alpha = jnp.exp(m_old - m_new)
        pr = jnp.exp(u - (m_new - cq))
        l_scr[...] = alpha * l_scr[...] + jnp.sum(pr, axis=1, keepdims=True)
        acc_scr[...] = alpha * acc_scr[...] + jnp.dot(pr.astype(BF16), v_bf,
                                                      preferred_element_type=F32)
        m_scr[...] = m_new

    @pl.when(step == 0)
    def _():
        qm = by_head(q_ref[0].astype(F32)).astype(BF16)
        qm_scr[...] = qm
        m_scr[...] = jnp.full_like(m_scr, -jnp.inf)
        l_scr[...] = jnp.zeros_like(l_scr)
        acc_scr[...] = jnp.zeros_like(acc_scr)
        r_scr[...] = jnp.zeros_like(r_scr)
        pad = jnp.zeros((LANES - rows, HEAD_DIM), F32)
        kn = jnp.concatenate([by_head(kn_ref[0].astype(F32)), pad], axis=0).astype(BF16)
        vn = jnp.concatenate([by_head(vn_ref[0].astype(F32)), pad], axis=0).astype(BF16)
        s = lax.dot_general(qm, kn, NT_DIMS, preferred_element_type=F32)
        key_head = lane // n_tok
        key_tok = lane - key_head * n_tok
        u = jnp.where(key_head == row_head, jnp.where(key_tok <= row_tok, s - ck_ref[0], -jnp.inf),
                      -jnp.inf)
        attend(u, vn)

    own_head = (lane & (N_HEADS - 1)) == row_head
    qm = qm_scr[...]
    after_sum = r_scr[...]
    u_parts, v_parts = [], []
    for pg in range(pages):
        later, total = _page_suffix(lf_refs[pg][0])
        after = after_sum + later
        after_sum = after_sum + total
        k2 = k_refs[pg][0].reshape(page_size * N_HEADS, HEAD_DIM).astype(BF16)
        v_parts.append(v_refs[pg][0].reshape(page_size * N_HEADS, HEAD_DIM).astype(BF16))
        s = lax.dot_general(qm, k2, NT_DIMS, preferred_element_type=F32)
        u_parts += [jnp.where(own_head, s[:, c * LANES:(c + 1) * LANES] + after[c:c + 1], -jnp.inf)
                    for c in range(page_size * N_HEADS // LANES)]
    r_scr[...] = after_sum
    attend(jnp.concatenate(u_parts, axis=1), jnp.concatenate(v_parts, axis=0))

    @pl.when(step == pl.num_programs(1) - 1)
    def _():
        o = acc_scr[...] / l_scr[...]
        o_ref[0] = jnp.concatenate(
            [o[h * n_tok:(h + 1) * n_tok] for h in range(N_HEADS)], axis=1).astype(BF16)


def _fox_decode(q, k_new, v_new, cq, ck, cache_k, cache_v, cache_lf, page_table):
    bsz, n_tok, d = q.shape
    n_pages = page_table.shape[1]
    page_size = cache_k.shape[1]
    pages = PAGES_PER_STEP
    rows = n_tok * N_HEADS

    def page_map(pg, ndim):
        def index_map(b, s, pt):
            return (pt[b, n_pages - 1 - (s * pages + pg)],) + (0,) * (ndim - 1)
        return index_map

    seq = lambda b, s, pt: (b, 0, 0)
    in_specs = [pl.BlockSpec((1, n_tok, d), seq), pl.BlockSpec((1, n_tok, d), seq),
                pl.BlockSpec((1, n_tok, d), seq), pl.BlockSpec((1, rows, 1), seq),
                pl.BlockSpec((1, 1, LANES), seq)]
    kv_block = (1, page_size, N_HEADS, HEAD_DIM)
    in_specs += [pl.BlockSpec(kv_block, page_map(pg, 4)) for pg in range(pages)]
    in_specs += [pl.BlockSpec(kv_block, page_map(pg, 4)) for pg in range(pages)]
    in_specs += [pl.BlockSpec((1, SUBLANES, LANES), page_map(pg, 3)) for pg in range(pages)]
    grid_spec = pltpu.PrefetchScalarGridSpec(
        num_scalar_prefetch=1,
        grid=(bsz, n_pages // pages),
        in_specs=in_specs,
        out_specs=pl.BlockSpec((1, n_tok, d), seq),
        scratch_shapes=[pltpu.VMEM((rows, HEAD_DIM), BF16), pltpu.VMEM((rows, 1), F32),
                        pltpu.VMEM((rows, 1), F32), pltpu.VMEM((rows, HEAD_DIM), F32),
                        pltpu.VMEM((1, LANES), F32)],
    )
    return pl.pallas_call(
        functools.partial(_fox_decode_kernel, pages=pages),
        grid_spec=grid_spec,
        out_shape=jax.ShapeDtypeStruct((bsz, n_tok, d), BF16),
        compiler_params=_params("arbitrary", "arbitrary"),
        name="fox_decode_attention",
    )(page_table, q, k_new, v_new, cq, ck, *([cache_k] * pages), *([cache_v] * pages),
      *([cache_lf] * pages))


def _oproj_kernel(x_ref, o_ref, g1_ref, wo_ref, xo_ref):
    xo_ref[0] = x_ref[0] + g1_ref[0] * jnp.dot(o_ref[0], wo_ref[...], preferred_element_type=F32)


def _oproj(x, o, g1, w_o, *, tm):
    bsz, t_len, d = x.shape
    tok = pl.BlockSpec((1, tm, d), lambda b, t: (b, t, 0))
    return pl.pallas_call(
        _oproj_kernel,
        grid=(bsz, t_len // tm),
        in_specs=[tok, tok, _mod_spec(g1, tm), _full_spec(w_o)],
        out_specs=tok,
        out_shape=jax.ShapeDtypeStruct((bsz, t_len, d), F32),
        compiler_params=_params("arbitrary", "arbitrary"),
        name="attn_out_proj",
    )(x, o, g1, w_o)


def _final_kernel(x_ref, y_hbm, pos0_ref, pos1_ref, posa_ref, g2_ref, g_ref, o_ref, ybuf, sem_y):
    step, n_steps = _grid_step()
    y = _gathered_tile(step, step == n_steps - 1, y_hbm, pos0_ref, pos1_ref, posa_ref, ybuf, sem_y,
                       x_ref.shape[1])
    x = x_ref[0] + g2_ref[0] * y
    o_ref[0] = (x * lax.rsqrt(jnp.mean(x * x, axis=-1, keepdims=True) + EPS)) * g_ref[...]


def _final(x, y, y_rows, g2, g_final, *, tm):
    bsz, t_len, d = x.shape
    nt = t_len // tm
    tok = pl.BlockSpec((1, tm, d), lambda b, t: (b, t, 0))
    return pl.pallas_call(
        _final_kernel,
        grid=(bsz, nt),
        in_specs=[tok] + _gather_in_specs(bsz * nt, tm, lambda b, t: b * nt + t)
                 + [_mod_spec(g2, tm), _full_spec(g_final)],
        out_specs=tok,
        out_shape=jax.ShapeDtypeStruct((bsz, t_len, d), F32),
        scratch_shapes=_gather_scratch(tm),
        compiler_params=_params("arbitrary", "arbitrary"),
        name="final_norm",
    )(x, y, y_rows, y_rows, y_rows, g2, g_final)


def _trunk(x, mods0, mods1, kvmods, prev, h0, w, attend, *, tm, tmx, seg, carry):
    sh1, sc1, g1, sh2, sc2, g2 = mods0
    x, h_out, conv_out = _lru_layer(x, sh1, sc1, g1, w["lru"], prev, h0, tm=tm, seg=seg, carry=carry)
    y, y_rows = _moe(x, sh2, sc2, w["moe0"], tm=tm, tmx=tmx)
    sh1b, sc1b, g1b, sh2b, sc2b, g2b = mods1
    shk, sck = kvmods
    x, k, v, lft, fct, *operands = _post0(x, y, y_rows, g2, shk, sck, sh1b, sc1b, w["post0"],
                                          tm=tm, seg=seg, carry=carry, prompt_layout=carry)
    o = attend(fct, *operands)
    x = _oproj(x, o, g1b, w["w_o"], tm=tm)
    y, y_rows = _moe(x, sh2b, sc2b, w["moe1"], tm=tm, tmx=tmx)
    y_out = _final(x, y, y_rows, g2b, w["g_final"], tm=tm)
    return y_out, h_out, conv_out, k, v, lft


def kernel(x_prompt, x_sample, state_h, state_conv, cache_k, cache_v, cache_logf, page_table,
           c_prompt, c_sample, w_mod, b_mod, g_mix, g_moe, lru_w_in, lru_conv_w, lru_conv_b,
           lru_w_a, lru_b_a, lru_w_x, lru_b_x, lru_lambda, lru_w_out, kv_g, kv_w_mod, kv_b_mod,
           kv_w, kv_b_f, attn_w_q, attn_w_o, moe_w_grp, moe_b_grp, moe_w_exp, moe_b_exp,
           moe_w_up, moe_w_down, g_final):
    d = D_MODEL
    bp, tp, _ = x_prompt.shape
    bs, ts, _ = x_sample.shape
    hd = N_HEADS * HEAD_DIM
    row = lambda a: a.reshape(1, -1)

    def moe_weights(layer):
        wr_t = jnp.concatenate(
            [moe_w_grp[layer].T, moe_w_exp[layer].T,
             jnp.zeros((ROUTER_ROWS - N_GROUPS - N_EXPERTS, d), F32)], axis=0)
        br = jnp.concatenate(
            [moe_b_grp[layer], moe_b_exp[layer],
             jnp.zeros((ROUTER_ROWS - N_GROUPS - N_EXPERTS,), F32)]).reshape(ROUTER_ROWS, 1)
        return dict(g_moe=row(g_moe[layer]), wr_t=wr_t, br=br,
                    w_up=moe_w_up[layer].astype(BF16), w_down=moe_w_down[layer].astype(BF16))

    w = dict(
        lru=dict(g_mix=row(g_mix[0]), w_in=lru_w_in[0].astype(BF16), conv_w=lru_conv_w[0],
                 conv_b=row(lru_conv_b[0]), w_a=lru_w_a[0].astype(BF16), b_a=row(lru_b_a[0]),
                 w_x=lru_w_x[0].astype(BF16), b_x=row(lru_b_x[0]), lam=row(lru_lambda[0]),
                 w_out=lru_w_out[0].astype(BF16)),
        moe0=moe_weights(0),
        moe1=moe_weights(1),
        post0=dict(kv_g=row(kv_g), w_k=kv_w[:, :hd].astype(BF16),
                   w_v=kv_w[:, hd:2 * hd].astype(BF16), w_f_t=kv_w[:, 2 * hd:].T,
                   w_v_t=kv_w[:, hd:2 * hd].T.astype(BF16), b_f=kv_b_f.reshape(N_HEADS, 1),
                   g_mix1=row(g_mix[1]), w_q=attn_w_q[0].astype(BF16),
                   w_q_t=attn_w_q[0].T.astype(BF16)),
        w_o=attn_w_o[0].astype(BF16),
        g_final=row(g_final),
    )

    n_c = bp + bs
    c_rows = -(-n_c // SUBLANES) * SUBLANES
    c_all = jnp.concatenate([c_prompt, c_sample, jnp.zeros((c_rows - n_c, d), F32)], axis=0)
    mod_l0 = _mm_bias(c_all, w_mod[0], b_mod[0])
    mod_l1 = _mm_bias(c_all, w_mod[1], b_mod[1])
    mod_kv = _mm_bias(c_all, kv_w_mod, kv_b_mod)

    def prompt_mods(m, parts):
        return [a.reshape(bp, 1, d) for a in jnp.split(m[:bp], parts, axis=-1)]

    def sample_mods(m, parts):
        return [jnp.repeat(a, ts, axis=0).reshape(1, bs * ts, d)
                for a in jnp.split(m[bp:n_c], parts, axis=-1)]

    prev_p = jnp.zeros((bp, SUBLANES, D_RNN), F32)
    h0_p = jnp.zeros((bp, 1, D_RNN), F32)
    attend_p = lambda fct, qa, ka, vt: _fox_prompt(qa, ka, vt)
    y_p, h_p, conv_p, k_p, v_p, lft_p = _trunk(
        x_prompt, prompt_mods(mod_l0, 6), prompt_mods(mod_l1, 6), prompt_mods(mod_kv, 2),
        prev_p, h0_p, w, attend_p, tm=TOKEN_TILE, tmx=EXPERT_TILE, seg=TOKEN_TILE, carry=True)

    n_s = bs * ts
    prev_s = jnp.pad(state_conv[0], ((0, 0), (ts - (CONV_WIDTH - 1), 0), (0, 0))).reshape(1, n_s, D_RNN)
    h0_s = jnp.repeat(state_h[0], ts, axis=0).reshape(1, n_s, D_RNN)
    n_pool, page_size = cache_k.shape[0], cache_k.shape[1]
    cache_lf = cache_logf.reshape(n_pool, SUBLANES, page_size * N_HEADS // SUBLANES)

    def attend_s(fct, q, kb, vb):
        c_new = jnp.transpose(fct.reshape(N_HEADS, bs, ts), (1, 0, 2)).reshape(bs, N_HEADS * ts)
        cq = c_new.reshape(bs, N_HEADS * ts, 1)
        ck = jnp.pad(c_new, ((0, 0), (0, LANES - N_HEADS * ts))).reshape(bs, 1, LANES)
        o = _fox_decode(q.reshape(bs, ts, d), kb.reshape(bs, ts, d), vb.reshape(bs, ts, d), cq, ck,
                        cache_k, cache_v, cache_lf, page_table)
        return o.reshape(1, n_s, d)

    y_s, h_s, conv_s, k_s, v_s, lft_s = _trunk(
        x_sample.reshape(1, n_s, d), sample_mods(mod_l0, 6), sample_mods(mod_l1, 6),
        sample_mods(mod_kv, 2), prev_s, h0_s, w, attend_s, tm=n_s, tmx=SAMPLE_EXPERT_TILE, seg=ts,
        carry=False)

    n_conv = CONV_WIDTH - 1
    return (
        y_p,
        y_s.reshape(bs, ts, d),
        h_p.reshape(1, bp, D_RNN),
        conv_p[:, SUBLANES - n_conv:].reshape(1, bp, n_conv, D_RNN),
        k_p.reshape(bp, tp, N_HEADS, HEAD_DIM),
        v_p.reshape(bp, tp, N_HEADS, HEAD_DIM),
        jnp.transpose(lft_p, (0, 2, 1)),
        h_s.reshape(bs, ts, D_RNN)[:, ts - 1].reshape(1, bs, D_RNN),
        conv_s.reshape(bs, ts, D_RNN)[:, ts - n_conv:].reshape(1, bs, n_conv, D_RNN),
        k_s.reshape(bs, ts, N_HEADS, HEAD_DIM),
        v_s.reshape(bs, ts, N_HEADS, HEAD_DIM),
        jnp.transpose(lft_s.reshape(N_HEADS, bs, ts), (1, 2, 0)),
    )
```

```python
import functools
import math

import jax
import jax.numpy as jnp
from jax import lax
from jax.experimental import pallas as pl
from jax.experimental.pallas import tpu as pltpu

F32 = jnp.float32
BF16 = jnp.bfloat16
I32 = jnp.int32

D_MODEL = 1024
D_RNN = D_MODEL
N_LRU_BLOCKS = 8
LRU_BLOCK = D_RNN // N_LRU_BLOCKS
CONV_WIDTH = 4
LRU_C = 8.0
N_HEADS = 8
HEAD_DIM = D_MODEL // N_HEADS
N_GROUPS = 4
EXPERTS_PER_GROUP = 4
N_EXPERTS = N_GROUPS * EXPERTS_PER_GROUP
D_EXPERT = D_MODEL // 2
EPS = 1e-6
N_PAIRS = 6
N_CLASSES = N_GROUPS * N_PAIRS
PAIR_LO = (0, 0, 0, 1, 1, 2)
PAIR_HI = (1, 2, 3, 2, 3, 3)

SUBLANES = 8
LANES = 128
VMEM_LIMIT = 48 * 1024 * 1024
DMA_THREADS = 2

TOKEN_TILE = 256
EXPERT_TILE = 256
SAMPLE_EXPERT_TILE = 32
ROUTER_ROWS = 32
ATTN_TILE = 512
PAGES_PER_STEP = 16
MOD_TILE = 512

HIGHEST = lax.Precision.HIGHEST
NT_DIMS = (((1,), (1,)), ((), ()))


def _params(*semantics):
    return pltpu.CompilerParams(dimension_semantics=semantics, vmem_limit_bytes=VMEM_LIMIT)


def _norm_mod(x, g, shift, scale):
    y = x * lax.rsqrt(jnp.mean(x * x, axis=-1, keepdims=True) + EPS)
    return (y * g) * (1.0 + scale) + shift


def _log_sigmoid(x):
    return jnp.minimum(x, 0.0) - jnp.log1p(jnp.exp(-jnp.abs(x)))


def _sigmoid(x):
    return 1.0 / (1.0 + jnp.exp(-x))


def _gelu_tanh(x):
    c = math.sqrt(2.0 / math.pi)
    return x * (0.5 * (1.0 + jnp.tanh(c * (x + 0.044715 * (x * x * x)))))


def _mod_spec(arr, tm):
    if arr.shape[1] == 1:
        return pl.BlockSpec((1, 1, arr.shape[2]), lambda b, t: (b, 0, 0))
    return pl.BlockSpec((1, tm, arr.shape[2]), lambda b, t: (b, t, 0))


ROW_TILES = D_MODEL // LANES


def _row_spec(nt, tm):
    return pl.BlockSpec((tm * ROW_TILES, LANES), lambda b, t: (b * nt + t, 0))


def _to_token_tiles(ref, lead, x):
    rows = x.shape[0]
    for c in range(ROW_TILES):
        ref[(*lead, pl.ds(c, rows, stride=ROW_TILES), slice(None))] = x[:, c * LANES:(c + 1) * LANES]


def _from_token_tiles(ref, lead, rows):
    return jnp.concatenate(
        [ref[(*lead, pl.ds(c, rows, stride=ROW_TILES), slice(None))] for c in range(ROW_TILES)],
        axis=1)


def _start_token_gather(src_hbm, idx_ref, k, buf, slot, sem, count):
    for r in range(count):
        rows = pl.ds(pl.multiple_of(idx_ref[k, 0, r], ROW_TILES), ROW_TILES)
        pltpu.make_async_copy(src_hbm.at[rows], buf.at[slot, pl.ds(r * ROW_TILES, ROW_TILES)],
                              sem.at[slot]).start(priority=r % DMA_THREADS)


def _wait_token_gather(src_hbm, buf, slot, sem, count):
    pltpu.make_async_copy(src_hbm.at[pl.ds(0, count * ROW_TILES)], buf.at[slot], sem.at[slot]).wait()


GATHER_AHEAD = 3
GATHER_SLOTS = GATHER_AHEAD + 1


def _gathered_tile(step, is_last, src_hbm, idxp_ref, idxa_ref, buf, sem, count):
    slot = lax.rem(step, GATHER_SLOTS)

    @pl.when(step == 0)
    def _():
        for k in range(GATHER_AHEAD):
            _start_token_gather(src_hbm, idxp_ref, k, buf, k, sem, count)

    _wait_token_gather(src_hbm, buf, slot, sem, count)
    _start_token_gather(src_hbm, idxa_ref, 0, buf, lax.rem(step + GATHER_AHEAD, GATHER_SLOTS), sem,
                        count)
    x = _from_token_tiles(buf, (slot,), count)

    @pl.when(is_last)
    def _():
        for ahead in range(1, GATHER_SLOTS):
            _wait_token_gather(src_hbm, buf, lax.rem(step + ahead, GATHER_SLOTS), sem, count)

    return x


def _pad_gather_steps(idx):
    return jnp.concatenate([idx, jnp.broadcast_to(idx[-1:], (GATHER_AHEAD,) + idx.shape[1:])], axis=0)


def _gather_in_specs(count, step_of):
    return [pl.BlockSpec(memory_space=pl.ANY),
            pl.BlockSpec((GATHER_AHEAD, 1, count), lambda *g: (0, 0, 0), memory_space=pltpu.SMEM),
            pl.BlockSpec((1, 1, count), lambda *g: (step_of(*g) + GATHER_AHEAD, 0, 0),
                         memory_space=pltpu.SMEM)]


def _gather_scratch(count):
    return [pltpu.VMEM((GATHER_SLOTS, count * ROW_TILES, LANES), F32),
            pltpu.SemaphoreType.DMA((GATHER_SLOTS,))]


def _full_spec(arr):
    zeros = (0,) * arr.ndim
    return pl.BlockSpec(arr.shape, lambda *_: zeros)


def _mm_bias_kernel(x_ref, w_ref, b_ref, o_ref):
    o_ref[...] = jnp.dot(x_ref[...].astype(BF16), w_ref[...].astype(BF16),
                         preferred_element_type=F32) + b_ref[...]


def _mm_bias(x, w, b):
    m, d = x.shape
    n = w.shape[1]
    return pl.pallas_call(
        _mm_bias_kernel,
        grid=(n // MOD_TILE,),
        in_specs=[pl.BlockSpec((m, d), lambda j: (0, 0)),
                  pl.BlockSpec((d, MOD_TILE), lambda j: (0, j)),
                  pl.BlockSpec((1, MOD_TILE), lambda j: (0, j))],
        out_specs=pl.BlockSpec((m, MOD_TILE), lambda j: (0, j)),
        out_shape=jax.ShapeDtypeStruct((m, n), F32),
        compiler_params=_params("arbitrary"),
        name="mod_vectors",
    )(x, w, b.reshape(1, n))


def _lru_kernel(x_ref, sh_ref, sc_ref, gt_ref, gmix_ref, win_ref, cw_ref, cb_ref, wa_ref, ba_ref,
                wx_ref, bx_ref, lam_ref, wout_ref, prev_ref, h0_ref,
                xo_ref, hs_ref, xb_ref, *scratch, tm, seg, carry):
    x = x_ref[0]
    hn = _norm_mod(x, gmix_ref[...], sh_ref[0], sc_ref[0])
    xg = jnp.dot(hn.astype(BF16), win_ref[...], preferred_element_type=F32)
    xb = xg[:, :D_RNN]
    gb = xg[:, D_RNN:]

    if carry:
        prev_scr, h_scr = scratch

        @pl.when(pl.program_id(1) == 0)
        def _():
            prev_scr[...] = prev_ref[0]
            h_scr[...] = h0_ref[0]

        prev = prev_scr[...]
        hprev = h_scr[...]
    else:
        prev = prev_ref[0]
        hprev = h0_ref[0]

    row = lax.broadcasted_iota(I32, (tm, 1), 0)
    rseg = row & (seg - 1)
    nprev = prev.shape[0]

    cw = cw_ref[...]
    xc = cb_ref[...] + cw[CONV_WIDTH - 1:CONV_WIDTH] * xb
    for d in range(1, CONV_WIDTH):
        rolled = pltpu.roll(xb, d, axis=0)
        rp = pltpu.roll(prev, nprev - SUBLANES + d, axis=0)
        if carry:
            head = jnp.where(row[:SUBLANES] < d, rp, rolled[:SUBLANES])
            shifted = head if tm == SUBLANES else jnp.concatenate([head, rolled[SUBLANES:]], axis=0)
        else:
            shifted = jnp.where(rseg < d, rp, rolled)
        xc = xc + cw[CONV_WIDTH - 1 - d:CONV_WIDTH - d] * shifted

    xcb = xc.astype(BF16)
    ra = jnp.concatenate(
        [jnp.dot(xcb[:, n * LRU_BLOCK:(n + 1) * LRU_BLOCK], wa_ref[n], preferred_element_type=F32)
         for n in range(N_LRU_BLOCKS)], axis=1)
    rx = jnp.concatenate(
        [jnp.dot(xcb[:, n * LRU_BLOCK:(n + 1) * LRU_BLOCK], wx_ref[n], preferred_element_type=F32)
         for n in range(N_LRU_BLOCKS)], axis=1)
    r = _sigmoid(ra + ba_ref[...])
    ig = _sigmoid(rx + bx_ref[...])
    log_a = (LRU_C * r) * _log_sigmoid(lam_ref[...])
    a = jnp.exp(log_a)
    bt = jnp.sqrt(-jnp.tanh(log_a) * (a * a + 1.0)) * (ig * xc)

    s = 1
    while s < seg:
        inside = rseg >= s
        a_sh = jnp.where(inside, pltpu.roll(a, s, axis=0), 1.0)
        b_sh = jnp.where(inside, pltpu.roll(bt, s, axis=0), 0.0)
        bt = a * b_sh + bt
        a = a * a_sh
        s *= 2
    hs = bt + a * hprev

    if carry:
        h_scr[...] = hs[tm - 1:tm]
        prev_scr[...] = xb[tm - SUBLANES:]
        hs_ref[0] = hs[tm - 1:tm]
        xb_ref[0] = xb[tm - SUBLANES:]
    else:
        hs_ref[0] = hs
        xb_ref[0] = xb

    y = (hs * _gelu_tanh(gb)).astype(BF16)
    out = jnp.dot(y, wout_ref[...], preferred_element_type=F32)
    xo_ref[0] = x + gt_ref[0] * out


def _lru_layer(x, sh, sc, gt, p, prev, h0, *, tm, seg, carry):
    bsz, t_len, d = x.shape
    r = D_RNN
    nt = t_len // tm
    tok = pl.BlockSpec((1, tm, d), lambda b, t: (b, t, 0))
    if carry:
        state_specs = [pl.BlockSpec((1, SUBLANES, r), lambda b, t: (b, 0, 0)),
                       pl.BlockSpec((1, 1, r), lambda b, t: (b, 0, 0))]
        out_specs = [tok,
                     pl.BlockSpec((1, 1, r), lambda b, t: (b, 0, 0)),
                     pl.BlockSpec((1, SUBLANES, r), lambda b, t: (b, 0, 0))]
        out_shape = [jax.ShapeDtypeStruct((bsz, t_len, d), F32),
                     jax.ShapeDtypeStruct((bsz, 1, r), F32),
                     jax.ShapeDtypeStruct((bsz, SUBLANES, r), F32)]
        scratch = [pltpu.VMEM((SUBLANES, r), F32), pltpu.VMEM((1, r), F32)]
    else:
        rtok = pl.BlockSpec((1, tm, r), lambda b, t: (b, t, 0))
        state_specs = [rtok, rtok]
        out_specs = [tok, rtok, rtok]
        out_shape = [jax.ShapeDtypeStruct((bsz, t_len, d), F32),
                     jax.ShapeDtypeStruct((bsz, t_len, r), F32),
                     jax.ShapeDtypeStruct((bsz, t_len, r), F32)]
        scratch = []
    weights = [p["g_mix"], p["w_in"], p["conv_w"], p["conv_b"], p["w_a"], p["b_a"], p["w_x"],
               p["b_x"], p["lam"], p["w_out"]]
    return pl.pallas_call(
        functools.partial(_lru_kernel, tm=tm, seg=seg, carry=carry),
        grid=(bsz, nt),
        in_specs=[tok, _mod_spec(sh, tm), _mod_spec(sc, tm), _mod_spec(gt, tm)]
                 + [_full_spec(w) for w in weights] + state_specs,
        out_specs=out_specs,
        out_shape=out_shape,
        scratch_shapes=scratch,
        compiler_params=_params("arbitrary", "arbitrary"),
        name="rglru_layer",
    )(x, sh, sc, gt, *weights, prev, h0)


def _router_kernel(x_ref, sh_ref, sc_ref, g_ref, wr_ref, br_ref,
                   hn_ref, cls_ref, rank_ref, glo_ref, ghi_ref, cnt_ref, carry_scr, *, tm):
    @pl.when((pl.program_id(0) == 0) & (pl.program_id(1) == 0))
    def _():
        carry_scr[...] = jnp.zeros_like(carry_scr)

    hn = _norm_mod(x_ref[0], g_ref[...], sh_ref[0], sc_ref[0])
    _to_token_tiles(hn_ref, (), hn)
    lt = lax.dot_general(wr_ref[...], hn, NT_DIMS, precision=HIGHEST,
                         preferred_element_type=F32) + br_ref[...]

    gl = [lt[k:k + 1] for k in range(N_GROUPS)]
    gmax = jnp.maximum(jnp.maximum(gl[0], gl[1]), jnp.maximum(gl[2], gl[3]))
    g_sel = jnp.where(gl[0] >= gmax, 0, jnp.where(gl[1] >= gmax, 1, jnp.where(gl[2] >= gmax, 2, 3)))
    p_grp = 1.0 / (jnp.exp(gl[0] - gmax) + jnp.exp(gl[1] - gmax)
                   + jnp.exp(gl[2] - gmax) + jnp.exp(gl[3] - gmax))

    def expert_logit(e):
        rows = [lt[N_GROUPS + g * EXPERTS_PER_GROUP + e:N_GROUPS + g * EXPERTS_PER_GROUP + e + 1]
                for g in range(N_GROUPS)]
        return jnp.where(g_sel == 0, rows[0],
                         jnp.where(g_sel == 1, rows[1], jnp.where(g_sel == 2, rows[2], rows[3])))

    es = [expert_logit(e) for e in range(EXPERTS_PER_GROUP)]

    def first_argmax(vals):
        vmax = jnp.maximum(jnp.maximum(vals[0], vals[1]), jnp.maximum(vals[2], vals[3]))
        idx = jnp.where(vals[0] >= vmax, 0,
                        jnp.where(vals[1] >= vmax, 1, jnp.where(vals[2] >= vmax, 2, 3)))
        return vmax, idx

    v1, i1 = first_argmax(es)
    rest = [jnp.where(i1 == e, -jnp.inf, es[e]) for e in range(EXPERTS_PER_GROUP)]
    v2, i2 = first_argmax(rest)
    e21 = jnp.exp(v2 - v1)
    w1 = (1.0 / (1.0 + e21)) * p_grp
    w2 = (e21 / (1.0 + e21)) * p_grp
    first_lower = i1 < i2
    lo = jnp.where(first_lower, i1, i2)
    hi = jnp.where(first_lower, i2, i1)
    glo_ref[0] = jnp.where(first_lower, w1, w2)
    ghi_ref[0] = jnp.where(first_lower, w2, w1)
    pair = jnp.where(lo == 0, hi - 1, jnp.where(lo == 1, hi + 1, 5))
    cls = g_sel * N_PAIRS + pair
    cls_ref[0] = cls

    crow = lax.broadcasted_iota(I32, (ROUTER_ROWS, tm), 0)
    onehot = jnp.where(crow == cls, 1.0, 0.0)
    ki = lax.broadcasted_iota(I32, (tm, tm), 0)
    kj = lax.broadcasted_iota(I32, (tm, tm), 1)
    upper = jnp.where(ki <= kj, 1.0, 0.0).astype(BF16)
    cum = jnp.dot(onehot.astype(BF16), upper, preferred_element_type=F32)
    carry = carry_scr[...]
    rank = jnp.sum(onehot * (cum - 1.0 + carry), axis=0, keepdims=True)
    rank_ref[0] = rank.astype(I32)
    carry = carry + cum[:, tm - 1:tm]
    carry_scr[...] = carry
    cnt_ref[...] = carry


def _router(x, sh, sc, g, wr_t, br, *, tm):
    bsz, t_len, d = x.shape
    nt = t_len // tm
    n_tiles = bsz * nt
    tok = pl.BlockSpec((1, tm, d), lambda b, t: (b, t, 0))
    lane_row = pl.BlockSpec((1, 1, tm), lambda b, t: (b * nt + t, 0, 0))
    row_shape = jax.ShapeDtypeStruct((n_tiles, 1, tm), F32)
    row_shape_i = jax.ShapeDtypeStruct((n_tiles, 1, tm), I32)
    return pl.pallas_call(
        functools.partial(_router_kernel, tm=tm),
        grid=(bsz, nt),
        in_specs=[tok, _mod_spec(sh, tm), _mod_spec(sc, tm), _full_spec(g), _full_spec(wr_t),
                  _full_spec(br)],
        out_specs=[_row_spec(nt, tm), lane_row, lane_row, lane_row, lane_row,
                   pl.BlockSpec((ROUTER_ROWS, 1), lambda b, t: (0, 0))],
        out_shape=[jax.ShapeDtypeStruct((bsz * t_len * ROW_TILES, LANES), F32), row_shape_i, row_shape_i,
                   row_shape, row_shape, jax.ShapeDtypeStruct((ROUTER_ROWS, 1), F32)],
        scratch_shapes=[pltpu.VMEM((ROUTER_ROWS, 1), F32)],
        compiler_params=_params("arbitrary", "arbitrary"),
        name="moe_router",
    )(x, sh, sc, g, wr_t, br)


def _expert_kernel(e1_ref, e2_ref, valid_ref, hn_hbm, srcp_ref, srca_ref, glo_ref, ghi_ref,
                   wu1_ref, wd1_ref, wu2_ref, wd2_ref, o_ref, xbuf, sem_g, *, tmx):
    del e1_ref, e2_ref
    j = pl.program_id(0)
    last = pl.num_programs(0) - 1

    @pl.when(valid_ref[j] == 0)
    def _():
        o_ref[...] = jnp.zeros_like(o_ref)

    @pl.when(valid_ref[j] == 1)
    def _():
        is_last = (j == last) | (valid_ref[jnp.minimum(j + 1, last)] == 0)
        x = _gathered_tile(j, is_last, hn_hbm, srcp_ref, srca_ref, xbuf, sem_g, tmx).astype(BF16)
        ri = lax.broadcasted_iota(I32, (tmx, tmx), 0)
        ci = lax.broadcasted_iota(I32, (tmx, tmx), 1)

        def column(row_ref):
            return jnp.sum(jnp.where(ri == ci, row_ref[0], 0.0), axis=1, keepdims=True)

        def ffn(wu_ref, wd_ref, gate):
            gu = jnp.dot(x, wu_ref[0], preferred_element_type=F32)
            g_ = gu[:, :D_EXPERT]
            u_ = gu[:, D_EXPERT:]
            act = ((g_ * _sigmoid(g_)) * u_).astype(BF16)
            return gate * jnp.dot(act, wd_ref[0], preferred_element_type=F32)

        _to_token_tiles(o_ref, (), ffn(wu1_ref, wd1_ref, column(glo_ref))
                        + ffn(wu2_ref, wd2_ref, column(ghi_ref)))


def _experts(hn, src, glo_s, ghi_s, w_up, w_down, e1, e2, valid, n_tiles, tmx):
    d = D_MODEL
    idx_block = (1, 1, tmx)
    at_tile = lambda j, e1, e2, v: (j, 0, 0)
    grid_spec = pltpu.PrefetchScalarGridSpec(
        num_scalar_prefetch=3,
        grid=(n_tiles,),
        in_specs=_gather_in_specs(tmx, lambda j, *_: j) + [
            pl.BlockSpec(idx_block, at_tile),
            pl.BlockSpec(idx_block, at_tile),
            pl.BlockSpec((1, d, 2 * D_EXPERT), lambda j, e1, e2, v: (e1[j], 0, 0)),
            pl.BlockSpec((1, D_EXPERT, d), lambda j, e1, e2, v: (e1[j], 0, 0)),
            pl.BlockSpec((1, d, 2 * D_EXPERT), lambda j, e1, e2, v: (e2[j], 0, 0)),
            pl.BlockSpec((1, D_EXPERT, d), lambda j, e1, e2, v: (e2[j], 0, 0)),
        ],
        out_specs=pl.BlockSpec((tmx * ROW_TILES, LANES), lambda j, e1, e2, v: (j, 0)),
        scratch_shapes=_gather_scratch(tmx),
    )
    return pl.pallas_call(
        functools.partial(_expert_kernel, tmx=tmx),
        grid_spec=grid_spec,
        out_shape=jax.ShapeDtypeStruct((n_tiles * tmx * ROW_TILES, LANES), F32),
        compiler_params=_params("arbitrary"),
        name="moe_experts",
    )(e1, e2, valid, hn, src, src, glo_s, ghi_s, w_up, w_down, w_up, w_down)


def _moe(x, sh, sc, p, *, tm, tmx):
    bsz, t_len, d = x.shape
    n = bsz * t_len
    hn, cls, rank, glo, ghi, cnt = _router(x, sh, sc, p["g_moe"], p["wr_t"], p["br"], tm=tm)
    cls = cls.reshape(n)
    rank = rank.reshape(n)
    counts = cnt[:N_CLASSES, 0].astype(I32)
    padded = ((counts + tmx - 1) // tmx) * tmx
    ends = jnp.cumsum(padded)
    offs = ends - padded
    pos = offs[cls] + rank
    n_tiles = n // tmx + N_CLASSES
    n_rows = n_tiles * tmx
    total = ends[N_CLASSES - 1]
    tile_start = jnp.arange(n_tiles, dtype=I32) * tmx
    valid = (tile_start < total).astype(I32)
    last_tile = jnp.maximum(total // tmx - 1, 0)
    tile_cls = jnp.minimum(
        jnp.sum((ends[None, :] <= jnp.minimum(tile_start, last_tile * tmx)[:, None]).astype(I32),
                axis=1), N_CLASSES - 1)
    grp = tile_cls // N_PAIRS
    pair = tile_cls % N_PAIRS
    e1 = grp * EXPERTS_PER_GROUP + jnp.asarray(PAIR_LO, I32)[pair]
    e2 = grp * EXPERTS_PER_GROUP + jnp.asarray(PAIR_HI, I32)[pair]
    token_info = jnp.stack([jnp.arange(n, dtype=I32).astype(F32), glo.reshape(n), ghi.reshape(n)],
                           axis=1)
    row_info = jnp.zeros((n_rows, 3), F32).at[pos].set(token_info)
    as_tiles = lambda a: a.reshape(n_tiles, 1, tmx)
    src_rows = _pad_gather_steps(as_tiles(row_info[:, 0].astype(I32) * ROW_TILES))
    ys = _experts(hn, src_rows, as_tiles(row_info[:, 1]), as_tiles(row_info[:, 2]),
                  p["w_up"], p["w_down"], e1, e2, valid, n_tiles, tmx)
    return ys, _pad_gather_steps((pos * ROW_TILES).reshape(n // tm, 1, tm))


AUG = 2 * HEAD_DIM
N_PIECES = 3
VT_ROWS = HEAD_DIM + 16
LOG2E = 1.0 / math.log(2.0)


def _bf16_pieces(x):
    pieces = []
    rest = x
    for _ in range(N_PIECES):
        piece = rest.astype(BF16).astype(F32)
        pieces.append(piece)
        rest = rest - piece
    return pieces


def _grid_step():
    n_steps = pl.num_programs(0) * pl.num_programs(1)
    return pl.program_id(0) * pl.num_programs(1) + pl.program_id(1), n_steps


def _post0_kernel(x_ref, y_hbm, posp_ref, posa_ref, g2_ref, shk_ref, sck_ref, kvg_ref,
                  wk_ref, wv_ref, wft_ref, bf_ref, shq_ref, scq_ref, gq_ref, wq_ref, *refs,
                  tm, seg, carry, prompt_layout):
    if prompt_layout:
        wvt_ref, x1_ref, k_ref, v_ref, lft_ref, fct_ref, qa_ref, ka_ref, vt_ref = refs[:9]
        scratch = refs[9:]
    else:
        x1_ref, k_ref, v_ref, lft_ref, fct_ref, q_ref, kb_ref, vb_ref = refs[:8]
        scratch = refs[8:]
    ybuf, sem_y = scratch[:2]
    scratch = scratch[2:]
    scale = HEAD_DIM ** -0.5
    step, n_steps = _grid_step()
    y = _gathered_tile(step, step == n_steps - 1, y_hbm, posp_ref, posa_ref, ybuf, sem_y, tm)
    x1 = x_ref[0] + g2_ref[0] * y
    x1_ref[0] = x1
    hk = _norm_mod(x1, kvg_ref[...], shk_ref[0], sck_ref[0])
    hkb = hk.astype(BF16)
    k = jnp.dot(hkb, wk_ref[...], preferred_element_type=F32)
    v = jnp.dot(hkb, wv_ref[...], preferred_element_type=F32)
    k_ref[0] = k
    v_ref[0] = v
    f = lax.dot_general(wft_ref[...], hk, NT_DIMS, precision=HIGHEST,
                        preferred_element_type=F32) + bf_ref[...]
    lf = _log_sigmoid(f)
    lft_ref[0] = lf
    col = lax.broadcasted_iota(I32, (1, tm), 1)
    cseg = col & (seg - 1)
    c = lf
    s = 1
    while s < seg:
        c = c + jnp.where(cseg >= s, pltpu.roll(c, s, axis=1), 0.0)
        s *= 2
    if carry:
        (f_scr,) = scratch

        @pl.when(pl.program_id(1) == 0)
        def _():
            f_scr[...] = jnp.zeros_like(f_scr)

        c = c + f_scr[...]
        f_scr[...] = c[:, tm - 1:tm]
    fct_ref[0] = c
    hqb = _norm_mod(x1, gq_ref[...], shq_ref[0], scq_ref[0]).astype(BF16)

    if not prompt_layout:
        q_ref[0] = (jnp.dot(hqb, wq_ref[...], preferred_element_type=F32) * scale).astype(BF16)
        kb_ref[0] = k.astype(BF16)
        vb_ref[0] = v.astype(BF16)
        return

    qt = lax.dot_general(wq_ref[...], hqb, NT_DIMS, preferred_element_type=F32) * (scale * LOG2E)
    vt = lax.dot_general(wvt_ref[...], hkb, NT_DIMS, preferred_element_type=F32)
    ones_rows = jnp.where(lax.broadcasted_iota(I32, (VT_ROWS - HEAD_DIM, tm), 0) == 0, 1.0, 0.0)
    v_rows = []
    for h in range(N_HEADS):
        v_rows += [vt[h * HEAD_DIM:(h + 1) * HEAD_DIM], ones_rows]
    vt_ref[0] = jnp.concatenate(v_rows, axis=0).astype(BF16)
    hi, mid, lo = _bf16_pieces(c * LOG2E)
    sub = lax.broadcasted_iota(I32, (SUBLANES, tm), 0)
    pad = jnp.zeros((HEAD_DIM - SUBLANES, tm), F32)
    q_rows, k_rows = [], []
    for h in range(N_HEADS):
        hs = slice(h, h + 1)
        eq = jnp.where(sub < 3, -1.0, jnp.where(sub == 3, hi[hs], jnp.where(
            sub == 4, mid[hs], jnp.where(sub == 5, lo[hs], 0.0))))
        ek = jnp.where(sub == 0, hi[hs], jnp.where(sub == 1, mid[hs], jnp.where(
            sub == 2, lo[hs], jnp.where(sub < 6, 1.0, 0.0))))
        q_rows += [qt[h * HEAD_DIM:(h + 1) * HEAD_DIM], eq, pad]
        k_rows += [ek, pad]
    qa_ref[0] = jnp.concatenate(q_rows, axis=0).astype(BF16)
    ekt = jnp.concatenate(k_rows, axis=0).astype(BF16)
    ri = lax.broadcasted_iota(I32, (tm, tm), 0)
    ci = lax.broadcasted_iota(I32, (tm, tm), 1)
    eye = jnp.where(ri == ci, 1.0, 0.0).astype(BF16)
    extra = lax.dot_general(eye, ekt, NT_DIMS, preferred_element_type=F32)
    k_cols = []
    for h in range(N_HEADS):
        hs = slice(h * HEAD_DIM, (h + 1) * HEAD_DIM)
        k_cols += [k[:, hs], extra[:, hs]]
    ka_ref[0] = jnp.concatenate(k_cols, axis=1).astype(BF16)


def _post0(x, y, y_rows, g2, shk, sck, shq, scq, p, *, tm, seg, carry, prompt_layout):
    bsz, t_len, d = x.shape
    nt = t_len // tm
    tok = pl.BlockSpec((1, tm, d), lambda b, t: (b, t, 0))
    headrow = pl.BlockSpec((1, N_HEADS, tm), lambda b, t: (b, 0, t))
    weights_kv = [p["kv_g"], p["w_k"], p["w_v"], p["w_f_t"], p["b_f"]]
    tok_f32 = jax.ShapeDtypeStruct((bsz, t_len, d), F32)
    head_f32 = jax.ShapeDtypeStruct((bsz, N_HEADS, t_len), F32)
    if prompt_layout:
        weights_q = [p["g_mix1"], p["w_q_t"], p["w_v_t"]]
        out_specs = [tok, tok, tok, headrow, headrow,
                     pl.BlockSpec((1, N_HEADS * AUG, tm), lambda b, t: (b, 0, t)),
                     pl.BlockSpec((1, tm, N_HEADS * AUG), lambda b, t: (b, t, 0)),
                     pl.BlockSpec((1, N_HEADS * VT_ROWS, tm), lambda b, t: (b, 0, t))]
        out_shape = [tok_f32] * 3 + [head_f32] * 2 + [
            jax.ShapeDtypeStruct((bsz, N_HEADS * AUG, t_len), BF16),
            jax.ShapeDtypeStruct((bsz, t_len, N_HEADS * AUG), BF16),
            jax.ShapeDtypeStruct((bsz, N_HEADS * VT_ROWS, t_len), BF16)]
    else:
        weights_q = [p["g_mix1"], p["w_q"]]
        out_specs = [tok, tok, tok, headrow, headrow, tok, tok, tok]
        out_shape = [tok_f32] * 3 + [head_f32] * 2 + [jax.ShapeDtypeStruct((bsz, t_len, d), BF16)] * 3
    return pl.pallas_call(
        functools.partial(_post0_kernel, tm=tm, seg=seg, carry=carry, prompt_layout=prompt_layout),
        grid=(bsz, t_len // tm),
        in_specs=[tok] + _gather_in_specs(tm, lambda b, t: b * nt + t)
                 + [_mod_spec(g2, tm), _mod_spec(shk, tm), _mod_spec(sck, tm)]
                 + [_full_spec(w) for w in weights_kv]
                 + [_mod_spec(shq, tm), _mod_spec(scq, tm)] + [_full_spec(w) for w in weights_q],
        out_specs=out_specs,
        out_shape=out_shape,
        scratch_shapes=_gather_scratch(tm) + ([pltpu.VMEM((N_HEADS, 1), F32)] if carry else []),
        compiler_params=_params("arbitrary", "arbitrary"),
        name="shared_kv_and_q",
    )(x, y, y_rows, y_rows, g2, shk, sck, *weights_kv, shq, scq, *weights_q)


def _fox_prompt_kernel(qi_ref, kj_ref, qa_ref, ka_ref, vt_ref, o_ref, m_scr, acc_scr, *, tq):
    p = pl.program_id(1)
    i = qi_ref[p]
    j = kj_ref[p]

    @pl.when(j == 0)
    def _():
        m_scr[...] = jnp.full_like(m_scr, -jnp.inf)
        acc_scr[...] = jnp.zeros_like(acc_scr)

    def step(masked):
        if masked:
            key = lax.broadcasted_iota(I32, (tq, tq), 0)
            qry = lax.broadcasted_iota(I32, (tq, tq), 1)
        for h in range(N_HEADS):
            hv = slice(h * VT_ROWS, (h + 1) * VT_ROWS)
            s = jnp.dot(ka_ref[0, :, h * AUG:(h + 1) * AUG], qa_ref[0, h * AUG:(h + 1) * AUG, :],
                        preferred_element_type=F32)
            if masked:
                s = jnp.where(key <= qry, s, -jnp.inf)
            m_old = m_scr[h]
            m_new = jnp.maximum(m_old, jnp.max(s, axis=0, keepdims=True))
            alpha = jnp.exp2(m_old - m_new)
            pr = jnp.exp2(s - m_new).astype(BF16)
            acc_scr[hv, :] = alpha * acc_scr[hv, :] + jnp.dot(vt_ref[0, hv, :], pr,
                                                              preferred_element_type=F32)
            m_scr[h] = m_new

    @pl.when(j < i)
    def _():
        step(False)

    @pl.when(j == i)
    def _():
        step(True)
        for h in range(N_HEADS):
            base = h * VT_ROWS
            out = acc_scr[base:base + HEAD_DIM, :] / acc_scr[base + HEAD_DIM:base + HEAD_DIM + 1, :]
            o_ref[0, :, h * HEAD_DIM:(h + 1) * HEAD_DIM] = jnp.transpose(out).astype(BF16)


def _fox_prompt(qa, ka, vt):
    bsz, _, t_len = vt.shape
    d = N_HEADS * HEAD_DIM
    tq = ATTN_TILE
    nb = t_len // tq
    pairs = [(i, j) for i in range(nb) for j in range(i + 1)]
    qi = jnp.asarray([a for a, _ in pairs], I32)
    kj = jnp.asarray([b for _, b in pairs], I32)
    grid_spec = pltpu.PrefetchScalarGridSpec(
        num_scalar_prefetch=2,
        grid=(bsz, len(pairs)),
        in_specs=[
            pl.BlockSpec((1, N_HEADS * AUG, tq), lambda b, p, qi, kj: (b, 0, qi[p])),
            pl.BlockSpec((1, tq, N_HEADS * AUG), lambda b, p, qi, kj: (b, kj[p], 0)),
            pl.BlockSpec((1, N_HEADS * VT_ROWS, tq), lambda b, p, qi, kj: (b, 0, kj[p])),
        ],
        out_specs=pl.BlockSpec((1, tq, d), lambda b, p, qi, kj: (b, qi[p], 0)),
        scratch_shapes=[pltpu.VMEM((N_HEADS, 1, tq), F32),
                        pltpu.VMEM((N_HEADS * VT_ROWS, tq), F32)],
    )
    return pl.pallas_call(
        functools.partial(_fox_prompt_kernel, tq=tq),
        grid_spec=grid_spec,
        out_shape=jax.ShapeDtypeStruct((bsz, t_len, d), BF16),
        compiler_params=_params("arbitrary", "arbitrary"),
        name="fox_prompt_attention",
    )(qi, kj, qa, ka, vt)


def _page_suffix(lf):
    lane = lax.broadcasted_iota(I32, lf.shape, 1)
    sub = lax.broadcasted_iota(I32, lf.shape, 0)
    x = lf
    sh = N_HEADS
    while sh < LANES:
        x = x + jnp.where(lane + sh < LANES, pltpu.roll(x, LANES - sh, axis=1), 0.0)
        sh *= 2
    y = jnp.where(lane < N_HEADS, x, 0.0)
    sh = N_HEADS
    while sh < LANES:
        y = y + pltpu.roll(y, sh, axis=1)
        sh *= 2
    z = y
    sh = 1
    while sh < SUBLANES:
        z = z + jnp.where(sub + sh < SUBLANES, pltpu.roll(z, SUBLANES - sh, axis=0), 0.0)
        sh *= 2
    return x + (z - y) - lf, z[0:1]


def _fox_decode_kernel(pt_ref, q_ref, kn_ref, vn_ref, cq_ref, ck_ref, *refs, pages):
    del pt_ref
    k_refs = refs[:pages]
    v_refs = refs[pages:2 * pages]
    lf_refs = refs[2 * pages:3 * pages]
    o_ref = refs[3 * pages]
    qm_scr, m_scr, l_scr, acc_scr, r_scr = refs[3 * pages + 1:]
    step = pl.program_id(1)
    n_tok = q_ref.shape[1]
    rows = n_tok * N_HEADS
    page_size = k_refs[0].shape[1]
    lane = lax.broadcasted_iota(I32, (rows, LANES), 1)
    row = lax.broadcasted_iota(I32, (rows, LANES), 0)
    row_head = row // n_tok
    row_tok = row - row_head * n_tok
    cq = cq_ref[0]

    def by_head(x):
        return jnp.concatenate(
            [x[:, h * HEAD_DIM:(h + 1) * HEAD_DIM] for h in range(N_HEADS)], axis=0)

    def attend(u, v_bf):
        m_old = m_scr[...]
        m_new = jnp.maximum(m_old, jnp.max(u, axis=1, keepdims=True) + cq)
        alpha = jnp.exp(m_old - m_new)
        pr = jnp.exp(u - (m_new - cq))
        l_scr[...] = alpha * l_scr[...] + jnp.sum(pr, axis=1, keepdims=True)
        acc_scr[...] = alpha * acc_scr[...] + jnp.dot(pr.astype(BF16), v_bf,
                                                      preferred_element_type=F32)
        m_scr[...] = m_new

    @pl.when(step == 0)
    def _():
        qm = by_head(q_ref[0].astype(F32)).astype(BF16)
        qm_scr[...] = qm
        m_scr[...] = jnp.full_like(m_scr, -jnp.inf)
        l_scr[...] = jnp.zeros_like(l_scr)
        acc_scr[...] = jnp.zeros_like(acc_scr)
        r_scr[...] = jnp.zeros_like(r_scr)
        pad = jnp.zeros((LANES - rows, HEAD_DIM), F32)
        kn = jnp.concatenate([by_head(kn_ref[0].astype(F32)), pad], axis=0).astype(BF16)
        vn = jnp.concatenate([by_head(vn_ref[0].astype(F32)), pad], axis=0).astype(BF16)
        s = lax.dot_general(qm, kn, NT_DIMS, preferred_element_type=F32)
        key_head = lane // n_tok
        key_tok = lane - key_head * n_tok
        u = jnp.where(key_head == row_head, jnp.where(key_tok <= row_tok, s - ck_ref[0], -jnp.inf),
                      -jnp.inf)
        attend(u, vn)

    own_head = (lane & (N_HEADS - 1)) == row_head
    qm = qm_scr[...]
    after_sum = r_scr[...]
    u_parts, v_parts = [], []
    for pg in range(pages):
        later, total = _page_suffix(lf_refs[pg][0])
        after = after_sum + later
        after_sum = after_sum + total
        k2 = k_refs[pg][0].reshape(page_size * N_HEADS, HEAD_DIM).astype(BF16)
        v_parts.append(v_refs[pg][0].reshape(page_size * N_HEADS, HEAD_DIM).astype(BF16))
        s = lax.dot_general(qm, k2, NT_DIMS, preferred_element_type=F32)
        u_parts += [jnp.where(own_head, s[:, c * LANES:(c + 1) * LANES] + after[c:c + 1], -jnp.inf)
                    for c in range(page_size * N_HEADS // LANES)]
    r_scr[...] = after_sum
    attend(jnp.concatenate(u_parts, axis=1), jnp.concatenate(v_parts, axis=0))

    @pl.when(step == pl.num_programs(1) - 1)
    def _():
        o = acc_scr[...] / l_scr[...]
        o_ref[0] = jnp.concatenate(
            [o[h * n_tok:(h + 1) * n_tok] for h in range(N_HEADS)], axis=1).astype(BF16)


def _fox_decode(q, k_new, v_new, cq, ck, cache_k, cache_v, cache_lf, page_table):
    bsz, n_tok, d = q.shape
    n_pages = page_table.shape[1]
    page_size = cache_k.shape[1]
    pages = PAGES_PER_STEP
    rows = n_tok * N_HEADS

    def page_map(pg, ndim):
        def index_map(b, s, pt):
            return (pt[b, n_pages - 1 - (s * pages + pg)],) + (0,) * (ndim - 1)
        return index_map

    seq = lambda b, s, pt: (b, 0, 0)
    in_specs = [pl.BlockSpec((1, n_tok, d), seq), pl.BlockSpec((1, n_tok, d), seq),
                pl.BlockSpec((1, n_tok, d), seq), pl.BlockSpec((1, rows, 1), seq),
                pl.BlockSpec((1, 1, LANES), seq)]
    kv_block = (1, page_size, N_HEADS, HEAD_DIM)
    in_specs += [pl.BlockSpec(kv_block, page_map(pg, 4)) for pg in range(pages)]
    in_specs += [pl.BlockSpec(kv_block, page_map(pg, 4)) for pg in range(pages)]
    in_specs += [pl.BlockSpec((1, SUBLANES, LANES), page_map(pg, 3)) for pg in range(pages)]
    grid_spec = pltpu.PrefetchScalarGridSpec(
        num_scalar_prefetch=1,
        grid=(bsz, n_pages // pages),
        in_specs=in_specs,
        out_specs=pl.BlockSpec((1, n_tok, d), seq),
        scratch_shapes=[pltpu.VMEM((rows, HEAD_DIM), BF16), pltpu.VMEM((rows, 1), F32),
                        pltpu.VMEM((rows, 1), F32), pltpu.VMEM((rows, HEAD_DIM), F32),
                        pltpu.VMEM((1, LANES), F32)],
    )
    return pl.pallas_call(
        functools.partial(_fox_decode_kernel, pages=pages),
        grid_spec=grid_spec,
        out_shape=jax.ShapeDtypeStruct((bsz, n_tok, d), BF16),
        compiler_params=_params("arbitrary", "arbitrary"),
        name="fox_decode_attention",
    )(page_table, q, k_new, v_new, cq, ck, *([cache_k] * pages), *([cache_v] * pages),
      *([cache_lf] * pages))


def _oproj_kernel(x_ref, o_ref, g1_ref, wo_ref, xo_ref):
    xo_ref[0] = x_ref[0] + g1_ref[0] * jnp.dot(o_ref[0], wo_ref[...], preferred_element_type=F32)


def _oproj(x, o, g1, w_o, *, tm):
    bsz, t_len, d = x.shape
    tok = pl.BlockSpec((1, tm, d), lambda b, t: (b, t, 0))
    return pl.pallas_call(
        _oproj_kernel,
        grid=(bsz, t_len // tm),
        in_specs=[tok, tok, _mod_spec(g1, tm), _full_spec(w_o)],
        out_specs=tok,
        out_shape=jax.ShapeDtypeStruct((bsz, t_len, d), F32),
        compiler_params=_params("arbitrary", "arbitrary"),
        name="attn_out_proj",
    )(x, o, g1, w_o)


def _final_kernel(x_ref, y_hbm, posp_ref, posa_ref, g2_ref, g_ref, o_ref, ybuf, sem_y):
    step, n_steps = _grid_step()
    y = _gathered_tile(step, step == n_steps - 1, y_hbm, posp_ref, posa_ref, ybuf, sem_y,
                       x_ref.shape[1])
    x = x_ref[0] + g2_ref[0] * y
    o_ref[0] = (x * lax.rsqrt(jnp.mean(x * x, axis=-1, keepdims=True) + EPS)) * g_ref[...]


def _final(x, y, y_rows, g2, g_final, *, tm):
    bsz, t_len, d = x.shape
    nt = t_len // tm
    tok = pl.BlockSpec((1, tm, d), lambda b, t: (b, t, 0))
    return pl.pallas_call(
        _final_kernel,
        grid=(bsz, nt),
        in_specs=[tok] + _gather_in_specs(tm, lambda b, t: b * nt + t)
                 + [_mod_spec(g2, tm), _full_spec(g_final)],
        out_specs=tok,
        out_shape=jax.ShapeDtypeStruct((bsz, t_len, d), F32),
        scratch_shapes=_gather_scratch(tm),
        compiler_params=_params("arbitrary", "arbitrary"),
        name="final_norm",
    )(x, y, y_rows, y_rows, g2, g_final)


def _trunk(x, mods0, mods1, kvmods, prev, h0, w, attend, *, tm, tmx, seg, carry):
    sh1, sc1, g1, sh2, sc2, g2 = mods0
    x, h_out, conv_out = _lru_layer(x, sh1, sc1, g1, w["lru"], prev, h0, tm=tm, seg=seg, carry=carry)
    y, y_rows = _moe(x, sh2, sc2, w["moe0"], tm=tm, tmx=tmx)
    sh1b, sc1b, g1b, sh2b, sc2b, g2b = mods1
    shk, sck = kvmods
    x, k, v, lft, fct, *operands = _post0(x, y, y_rows, g2, shk, sck, sh1b, sc1b, w["post0"],
                                          tm=tm, seg=seg, carry=carry, prompt_layout=carry)
    o = attend(fct, *operands)
    x = _oproj(x, o, g1b, w["w_o"], tm=tm)
    y, y_rows = _moe(x, sh2b, sc2b, w["moe1"], tm=tm, tmx=tmx)
    y_out = _final(x, y, y_rows, g2b, w["g_final"], tm=tm)
    return y_out, h_out, conv_out, k, v, lft


def kernel(x_prompt, x_sample, state_h, state_conv, cache_k, cache_v, cache_logf, page_table,
           c_prompt, c_sample, w_mod, b_mod, g_mix, g_moe, lru_w_in, lru_conv_w, lru_conv_b,
           lru_w_a, lru_b_a, lru_w_x, lru_b_x, lru_lambda, lru_w_out, kv_g, kv_w_mod, kv_b_mod,
           kv_w, kv_b_f, attn_w_q, attn_w_o, moe_w_grp, moe_b_grp, moe_w_exp, moe_b_exp,
           moe_w_up, moe_w_down, g_final):
    d = D_MODEL
    bp, tp, _ = x_prompt.shape
    bs, ts, _ = x_sample.shape
    hd = N_HEADS * HEAD_DIM
    row = lambda a: a.reshape(1, -1)

    def moe_weights(layer):
        wr_t = jnp.concatenate(
            [moe_w_grp[layer].T, moe_w_exp[layer].T,
             jnp.zeros((ROUTER_ROWS - N_GROUPS - N_EXPERTS, d), F32)], axis=0)
        br = jnp.concatenate(
            [moe_b_grp[layer], moe_b_exp[layer],
             jnp.zeros((ROUTER_ROWS - N_GROUPS - N_EXPERTS,), F32)]).reshape(ROUTER_ROWS, 1)
        return dict(g_moe=row(g_moe[layer]), wr_t=wr_t, br=br,
                    w_up=moe_w_up[layer].astype(BF16), w_down=moe_w_down[layer].astype(BF16))

    w = dict(
        lru=dict(g_mix=row(g_mix[0]), w_in=lru_w_in[0].astype(BF16), conv_w=lru_conv_w[0],
                 conv_b=row(lru_conv_b[0]), w_a=lru_w_a[0].astype(BF16), b_a=row(lru_b_a[0]),
                 w_x=lru_w_x[0].astype(BF16), b_x=row(lru_b_x[0]), lam=row(lru_lambda[0]),
                 w_out=lru_w_out[0].astype(BF16)),
        moe0=moe_weights(0),
        moe1=moe_weights(1),
        post0=dict(kv_g=row(kv_g), w_k=kv_w[:, :hd].astype(BF16),
                   w_v=kv_w[:, hd:2 * hd].astype(BF16), w_f_t=kv_w[:, 2 * hd:].T,
                   w_v_t=kv_w[:, hd:2 * hd].T.astype(BF16), b_f=kv_b_f.reshape(N_HEADS, 1),
                   g_mix1=row(g_mix[1]), w_q=attn_w_q[0].astype(BF16),
                   w_q_t=attn_w_q[0].T.astype(BF16)),
        w_o=attn_w_o[0].astype(BF16),
        g_final=row(g_final),
    )

    n_c = bp + bs
    c_rows = -(-n_c // SUBLANES) * SUBLANES
    c_all = jnp.concatenate([c_prompt, c_sample, jnp.zeros((c_rows - n_c, d), F32)], axis=0)
    mod_l0 = _mm_bias(c_all, w_mod[0], b_mod[0])
    mod_l1 = _mm_bias(c_all, w_mod[1], b_mod[1])
    mod_kv = _mm_bias(c_all, kv_w_mod, kv_b_mod)

    def prompt_mods(m, parts):
        return [a.reshape(bp, 1, d) for a in jnp.split(m[:bp], parts, axis=-1)]

    def sample_mods(m, parts):
        return [jnp.repeat(a, ts, axis=0).reshape(1, bs * ts, d)
                for a in jnp.split(m[bp:n_c], parts, axis=-1)]

    prev_p = jnp.zeros((bp, SUBLANES, D_RNN), F32)
    h0_p = jnp.zeros((bp, 1, D_RNN), F32)
    attend_p = lambda fct, qa, ka, vt: _fox_prompt(qa, ka, vt)
    y_p, h_p, conv_p, k_p, v_p, lft_p = _trunk(
        x_prompt, prompt_mods(mod_l0, 6), prompt_mods(mod_l1, 6), prompt_mods(mod_kv, 2),
        prev_p, h0_p, w, attend_p, tm=TOKEN_TILE, tmx=EXPERT_TILE, seg=TOKEN_TILE, carry=True)

    n_s = bs * ts
    prev_s = jnp.pad(state_conv[0], ((0, 0), (ts - (CONV_WIDTH - 1), 0), (0, 0))).reshape(1, n_s, D_RNN)
    h0_s = jnp.repeat(state_h[0], ts, axis=0).reshape(1, n_s, D_RNN)
    n_pool, page_size = cache_k.shape[0], cache_k.shape[1]
    cache_lf = cache_logf.reshape(n_pool, SUBLANES, page_size * N_HEADS // SUBLANES)

    def attend_s(fct, q, kb, vb):
        c_new = jnp.transpose(fct.reshape(N_HEADS, bs, ts), (1, 0, 2)).reshape(bs, N_HEADS * ts)
        cq = c_new.reshape(bs, N_HEADS * ts, 1)
        ck = jnp.pad(c_new, ((0, 0), (0, LANES - N_HEADS * ts))).reshape(bs, 1, LANES)
        o = _fox_decode(q.reshape(bs, ts, d), kb.reshape(bs, ts, d), vb.reshape(bs, ts, d), cq, ck,
                        cache_k, cache_v, cache_lf, page_table)
        return o.reshape(1, n_s, d)

    y_s, h_s, conv_s, k_s, v_s, lft_s = _trunk(
        x_sample.reshape(1, n_s, d), sample_mods(mod_l0, 6), sample_mods(mod_l1, 6),
        sample_mods(mod_kv, 2), prev_s, h0_s, w, attend_s, tm=n_s, tmx=SAMPLE_EXPERT_TILE, seg=ts,
        carry=False)

    n_conv = CONV_WIDTH - 1
    return (
        y_p,
        y_s.reshape(bs, ts, d),
        h_p.reshape(1, bp, D_RNN),
        conv_p[:, SUBLANES - n_conv:].reshape(1, bp, n_conv, D_RNN),
        k_p.reshape(bp, tp, N_HEADS, HEAD_DIM),
        v_p.reshape(bp, tp, N_HEADS, HEAD_DIM),
        jnp.transpose(lft_p, (0, 2, 1)),
        h_s.reshape(bs, ts, D_RNN)[:, ts - 1].reshape(1, bs, D_RNN),
        conv_s.reshape(bs, ts, D_RNN)[:, ts - n_conv:].reshape(1, bs, n_conv, D_RNN),
        k_s.reshape(bs, ts, N_HEADS, HEAD_DIM),
        v_s.reshape(bs, ts, N_HEADS, HEAD_DIM),
        jnp.transpose(lft_s.reshape(N_HEADS, bs, ts), (1, 2, 0)),
    )
```

```python
import functools
import math

import jax
import jax.numpy as jnp
from jax import lax
from jax.experimental import pallas as pl
from jax.experimental.pallas import tpu as pltpu

F32 = jnp.float32
BF16 = jnp.bfloat16
I32 = jnp.int32

D_MODEL = 1024
D_RNN = D_MODEL
N_LRU_BLOCKS = 8
LRU_BLOCK = D_RNN // N_LRU_BLOCKS
CONV_WIDTH = 4
LRU_C = 8.0
N_HEADS = 8
HEAD_DIM = D_MODEL // N_HEADS
N_GROUPS = 4
EXPERTS_PER_GROUP = 4
N_EXPERTS = N_GROUPS * EXPERTS_PER_GROUP
D_EXPERT = D_MODEL // 2
EPS = 1e-6
N_PAIRS = 6
N_CLASSES = N_GROUPS * N_PAIRS
PAIR_LO = (0, 0, 0, 1, 1, 2)
PAIR_HI = (1, 2, 3, 2, 3, 3)

SUBLANES = 8
LANES = 128
VMEM_LIMIT = 48 * 1024 * 1024
DMA_THREADS = 2

TOKEN_TILE = 256
EXPERT_TILE = 256
SAMPLE_EXPERT_TILE = 32
ROUTER_ROWS = 32
ATTN_TILE = 512
PAGES_PER_STEP = 16
MOD_TILE = 512

HIGHEST = lax.Precision.HIGHEST
NT_DIMS = (((1,), (1,)), ((), ()))


def _params(*semantics):
    return pltpu.CompilerParams(dimension_semantics=semantics, vmem_limit_bytes=VMEM_LIMIT)


def _norm_mod(x, g, shift, scale):
    y = x * lax.rsqrt(jnp.mean(x * x, axis=-1, keepdims=True) + EPS)
    return (y * g) * (1.0 + scale) + shift


def _log_sigmoid(x):
    return jnp.minimum(x, 0.0) - jnp.log1p(jnp.exp(-jnp.abs(x)))


def _sigmoid(x):
    return 1.0 / (1.0 + jnp.exp(-x))


def _gelu_tanh(x):
    c = math.sqrt(2.0 / math.pi)
    return x * (0.5 * (1.0 + jnp.tanh(c * (x + 0.044715 * (x * x * x)))))


def _mod_spec(arr, tm):
    if arr.shape[1] == 1:
        return pl.BlockSpec((1, 1, arr.shape[2]), lambda b, t: (b, 0, 0))
    return pl.BlockSpec((1, tm, arr.shape[2]), lambda b, t: (b, t, 0))


ROW_TILES = D_MODEL // LANES


def _row_spec(nt, tm):
    return pl.BlockSpec((tm * ROW_TILES, LANES), lambda b, t: (b * nt + t, 0))


def _to_token_tiles(ref, lead, x):
    rows = x.shape[0]
    for c in range(ROW_TILES):
        ref[(*lead, pl.ds(c, rows, stride=ROW_TILES), slice(None))] = x[:, c * LANES:(c + 1) * LANES]


def _from_token_tiles(ref, lead, rows):
    return jnp.concatenate(
        [ref[(*lead, pl.ds(c, rows, stride=ROW_TILES), slice(None))] for c in range(ROW_TILES)],
        axis=1)


def _start_token_gather(src_hbm, idx_ref, k, buf, slot, sem, count):
    for r in range(count):
        rows = pl.ds(pl.multiple_of(idx_ref[k, 0, r], ROW_TILES), ROW_TILES)
        pltpu.make_async_copy(src_hbm.at[rows], buf.at[slot, pl.ds(r * ROW_TILES, ROW_TILES)],
                              sem.at[slot]).start(priority=r % DMA_THREADS)


def _wait_token_gather(src_hbm, buf, slot, sem, count):
    pltpu.make_async_copy(src_hbm.at[pl.ds(0, count * ROW_TILES)], buf.at[slot], sem.at[slot]).wait()


GATHER_AHEAD = 3
GATHER_SLOTS = GATHER_AHEAD + 1


def _gathered_tile(step, is_last, src_hbm, idxp_ref, idxa_ref, buf, sem, count):
    slot = lax.rem(step, GATHER_SLOTS)

    @pl.when(step == 0)
    def _():
        for k in range(GATHER_AHEAD):
            _start_token_gather(src_hbm, idxp_ref, k, buf, k, sem, count)

    _wait_token_gather(src_hbm, buf, slot, sem, count)
    _start_token_gather(src_hbm, idxa_ref, 0, buf, lax.rem(step + GATHER_AHEAD, GATHER_SLOTS), sem,
                        count)
    x = _from_token_tiles(buf, (slot,), count)

    @pl.when(is_last)
    def _():
        for ahead in range(1, GATHER_SLOTS):
            _wait_token_gather(src_hbm, buf, lax.rem(step + ahead, GATHER_SLOTS), sem, count)

    return x


def _pad_gather_steps(idx):
    return jnp.concatenate([idx, jnp.broadcast_to(idx[-1:], (GATHER_AHEAD,) + idx.shape[1:])], axis=0)


def _gather_in_specs(count, step_of):
    return [pl.BlockSpec(memory_space=pl.ANY),
            pl.BlockSpec((GATHER_AHEAD, 1, count), lambda *g: (0, 0, 0), memory_space=pltpu.SMEM),
            pl.BlockSpec((1, 1, count), lambda *g: (step_of(*g) + GATHER_AHEAD, 0, 0),
                         memory_space=pltpu.SMEM)]


def _gather_scratch(count):
    return [pltpu.VMEM((GATHER_SLOTS, count * ROW_TILES, LANES), F32),
            pltpu.SemaphoreType.DMA((GATHER_SLOTS,))]


def _full_spec(arr):
    zeros = (0,) * arr.ndim
    return pl.BlockSpec(arr.shape, lambda *_: zeros)


def _mm_bias_kernel(x_ref, w_ref, b_ref, o_ref):
    o_ref[...] = jnp.dot(x_ref[...].astype(BF16), w_ref[...].astype(BF16),
                         preferred_element_type=F32) + b_ref[...]


def _mm_bias(x, w, b):
    m, d = x.shape
    n = w.shape[1]
    return pl.pallas_call(
        _mm_bias_kernel,
        grid=(n // MOD_TILE,),
        in_specs=[pl.BlockSpec((m, d), lambda j: (0, 0)),
                  pl.BlockSpec((d, MOD_TILE), lambda j: (0, j)),
                  pl.BlockSpec((1, MOD_TILE), lambda j: (0, j))],
        out_specs=pl.BlockSpec((m, MOD_TILE), lambda j: (0, j)),
        out_shape=jax.ShapeDtypeStruct((m, n), F32),
        compiler_params=_params("arbitrary"),
        name="mod_vectors",
    )(x, w, b.reshape(1, n))


def _lru_kernel(x_ref, sh_ref, sc_ref, gt_ref, gmix_ref, win_ref, cw_ref, cb_ref, wa_ref, ba_ref,
                wx_ref, bx_ref, lam_ref, wout_ref, prev_ref, h0_ref,
                xo_ref, hs_ref, xb_ref, *scratch, tm, seg, carry):
    x = x_ref[0]
    hn = _norm_mod(x, gmix_ref[...], sh_ref[0], sc_ref[0])
    xg = jnp.dot(hn.astype(BF16), win_ref[...], preferred_element_type=F32)
    xb = xg[:, :D_RNN]
    gb = xg[:, D_RNN:]

    if carry:
        prev_scr, h_scr = scratch

        @pl.when(pl.program_id(1) == 0)
        def _():
            prev_scr[...] = prev_ref[0]
            h_scr[...] = h0_ref[0]

        prev = prev_scr[...]
        hprev = h_scr[...]
    else:
        prev = prev_ref[0]
        hprev = h0_ref[0]

    row = lax.broadcasted_iota(I32, (tm, 1), 0)
    rseg = row & (seg - 1)
    nprev = prev.shape[0]

    cw = cw_ref[...]
    xc = cb_ref[...] + cw[CONV_WIDTH - 1:CONV_WIDTH] * xb
    for d in range(1, CONV_WIDTH):
        rolled = pltpu.roll(xb, d, axis=0)
        rp = pltpu.roll(prev, nprev - SUBLANES + d, axis=0)
        if carry:
            head = jnp.where(row[:SUBLANES] < d, rp, rolled[:SUBLANES])
            shifted = head if tm == SUBLANES else jnp.concatenate([head, rolled[SUBLANES:]], axis=0)
        else:
            shifted = jnp.where(rseg < d, rp, rolled)
        xc = xc + cw[CONV_WIDTH - 1 - d:CONV_WIDTH - d] * shifted

    xcb = xc.astype(BF16)
    ra = jnp.concatenate(
        [jnp.dot(xcb[:, n * LRU_BLOCK:(n + 1) * LRU_BLOCK], wa_ref[n], preferred_element_type=F32)
         for n in range(N_LRU_BLOCKS)], axis=1)
    rx = jnp.concatenate(
        [jnp.dot(xcb[:, n * LRU_BLOCK:(n + 1) * LRU_BLOCK], wx_ref[n], preferred_element_type=F32)
         for n in range(N_LRU_BLOCKS)], axis=1)
    r = _sigmoid(ra + ba_ref[...])
    ig = _sigmoid(rx + bx_ref[...])
    log_a = (LRU_C * r) * _log_sigmoid(lam_ref[...])
    a = jnp.exp(log_a)
    bt = jnp.sqrt(-jnp.tanh(log_a) * (a * a + 1.0)) * (ig * xc)

    s = 1
    while s < seg:
        inside = rseg >= s
        a_sh = jnp.where(inside, pltpu.roll(a, s, axis=0), 1.0)
        b_sh = jnp.where(inside, pltpu.roll(bt, s, axis=0), 0.0)
        bt = a * b_sh + bt
        a = a * a_sh
        s *= 2
    hs = bt + a * hprev

    if carry:
        h_scr[...] = hs[tm - 1:tm]
        prev_scr[...] = xb[tm - SUBLANES:]
        hs_ref[0] = hs[tm - 1:tm]
        xb_ref[0] = xb[tm - SUBLANES:]
    else:
        hs_ref[0] = hs
        xb_ref[0] = xb

    y = (hs * _gelu_tanh(gb)).astype(BF16)
    out = jnp.dot(y, wout_ref[...], preferred_element_type=F32)
    xo_ref[0] = x + gt_ref[0] * out


def _lru_layer(x, sh, sc, gt, p, prev, h0, *, tm, seg, carry):
    bsz, t_len, d = x.shape
    r = D_RNN
    nt = t_len // tm
    tok = pl.BlockSpec((1, tm, d), lambda b, t: (b, t, 0))
    if carry:
        state_specs = [pl.BlockSpec((1, SUBLANES, r), lambda b, t: (b, 0, 0)),
                       pl.BlockSpec((1, 1, r), lambda b, t: (b, 0, 0))]
        out_specs = [tok,
                     pl.BlockSpec((1, 1, r), lambda b, t: (b, 0, 0)),
                     pl.BlockSpec((1, SUBLANES, r), lambda b, t: (b, 0, 0))]
        out_shape = [jax.ShapeDtypeStruct((bsz, t_len, d), F32),
                     jax.ShapeDtypeStruct((bsz, 1, r), F32),
                     jax.ShapeDtypeStruct((bsz, SUBLANES, r), F32)]
        scratch = [pltpu.VMEM((SUBLANES, r), F32), pltpu.VMEM((1, r), F32)]
    else:
        rtok = pl.BlockSpec((1, tm, r), lambda b, t: (b, t, 0))
        state_specs = [rtok, rtok]
        out_specs = [tok, rtok, rtok]
        out_shape = [jax.ShapeDtypeStruct((bsz, t_len, d), F32),
                     jax.ShapeDtypeStruct((bsz, t_len, r), F32),
                     jax.ShapeDtypeStruct((bsz, t_len, r), F32)]
        scratch = []
    weights = [p["g_mix"], p["w_in"], p["conv_w"], p["conv_b"], p["w_a"], p["b_a"], p["w_x"],
               p["b_x"], p["lam"], p["w_out"]]
    return pl.pallas_call(
        functools.partial(_lru_kernel, tm=tm, seg=seg, carry=carry),
        grid=(bsz, nt),
        in_specs=[tok, _mod_spec(sh, tm), _mod_spec(sc, tm), _mod_spec(gt, tm)]
                 + [_full_spec(w) for w in weights] + state_specs,
        out_specs=out_specs,
        out_shape=out_shape,
        scratch_shapes=scratch,
        compiler_params=_params("arbitrary", "arbitrary"),
        name="rglru_layer",
    )(x, sh, sc, gt, *weights, prev, h0)


def _router_kernel(x_ref, sh_ref, sc_ref, g_ref, wr_ref, br_ref,
                   hn_ref, cls_ref, rank_ref, glo_ref, ghi_ref, cnt_ref, carry_scr, *, tm):
    @pl.when((pl.program_id(0) == 0) & (pl.program_id(1) == 0))
    def _():
        carry_scr[...] = jnp.zeros_like(carry_scr)

    hn = _norm_mod(x_ref[0], g_ref[...], sh_ref[0], sc_ref[0])
    _to_token_tiles(hn_ref, (), hn)
    lt = lax.dot_general(wr_ref[...], hn, NT_DIMS, precision=HIGHEST,
                         preferred_element_type=F32) + br_ref[...]

    gl = [lt[k:k + 1] for k in range(N_GROUPS)]
    gmax = jnp.maximum(jnp.maximum(gl[0], gl[1]), jnp.maximum(gl[2], gl[3]))
    g_sel = jnp.where(gl[0] >= gmax, 0, jnp.where(gl[1] >= gmax, 1, jnp.where(gl[2] >= gmax, 2, 3)))
    p_grp = 1.0 / (jnp.exp(gl[0] - gmax) + jnp.exp(gl[1] - gmax)
                   + jnp.exp(gl[2] - gmax) + jnp.exp(gl[3] - gmax))

    def expert_logit(e):
        rows = [lt[N_GROUPS + g * EXPERTS_PER_GROUP + e:N_GROUPS + g * EXPERTS_PER_GROUP + e + 1]
                for g in range(N_GROUPS)]
        return jnp.where(g_sel == 0, rows[0],
                         jnp.where(g_sel == 1, rows[1], jnp.where(g_sel == 2, rows[2], rows[3])))

    es = [expert_logit(e) for e in range(EXPERTS_PER_GROUP)]

    def first_argmax(vals):
        vmax = jnp.maximum(jnp.maximum(vals[0], vals[1]), jnp.maximum(vals[2], vals[3]))
        idx = jnp.where(vals[0] >= vmax, 0,
                        jnp.where(vals[1] >= vmax, 1, jnp.where(vals[2] >= vmax, 2, 3)))
        return vmax, idx

    v1, i1 = first_argmax(es)
    rest = [jnp.where(i1 == e, -jnp.inf, es[e]) for e in range(EXPERTS_PER_GROUP)]
    v2, i2 = first_argmax(rest)
    e21 = jnp.exp(v2 - v1)
    w1 = (1.0 / (1.0 + e21)) * p_grp
    w2 = (e21 / (1.0 + e21)) * p_grp
    first_lower = i1 < i2
    lo = jnp.where(first_lower, i1, i2)
    hi = jnp.where(first_lower, i2, i1)
    glo_ref[0] = jnp.where(first_lower, w1, w2)
    ghi_ref[0] = jnp.where(first_lower, w2, w1)
    pair = jnp.where(lo == 0, hi - 1, jnp.where(lo == 1, hi + 1, 5))
    cls = g_sel * N_PAIRS + pair
    cls_ref[0] = cls

    crow = lax.broadcasted_iota(I32, (ROUTER_ROWS, tm), 0)
    onehot = jnp.where(crow == cls, 1.0, 0.0)
    ki = lax.broadcasted_iota(I32, (tm, tm), 0)
    kj = lax.broadcasted_iota(I32, (tm, tm), 1)
    upper = jnp.where(ki <= kj, 1.0, 0.0).astype(BF16)
    cum = jnp.dot(onehot.astype(BF16), upper, preferred_element_type=F32)
    carry = carry_scr[...]
    rank = jnp.sum(onehot * (cum - 1.0 + carry), axis=0, keepdims=True)
    rank_ref[0] = rank.astype(I32)
    carry = carry + cum[:, tm - 1:tm]
    carry_scr[...] = carry
    cnt_ref[...] = carry


def _router(x, sh, sc, g, wr_t, br, *, tm):
    bsz, t_len, d = x.shape
    nt = t_len // tm
    n_tiles = bsz * nt
    tok = pl.BlockSpec((1, tm, d), lambda b, t: (b, t, 0))
    lane_row = pl.BlockSpec((1, 1, tm), lambda b, t: (b * nt + t, 0, 0))
    row_shape = jax.ShapeDtypeStruct((n_tiles, 1, tm), F32)
    row_shape_i = jax.ShapeDtypeStruct((n_tiles, 1, tm), I32)
    return pl.pallas_call(
        functools.partial(_router_kernel, tm=tm),
        grid=(bsz, nt),
        in_specs=[tok, _mod_spec(sh, tm), _mod_spec(sc, tm), _full_spec(g), _full_spec(wr_t),
                  _full_spec(br)],
        out_specs=[_row_spec(nt, tm), lane_row, lane_row, lane_row, lane_row,
                   pl.BlockSpec((ROUTER_ROWS, 1), lambda b, t: (0, 0))],
        out_shape=[jax.ShapeDtypeStruct((bsz * t_len * ROW_TILES, LANES), F32), row_shape_i, row_shape_i,
                   row_shape, row_shape, jax.ShapeDtypeStruct((ROUTER_ROWS, 1), F32)],
        scratch_shapes=[pltpu.VMEM((ROUTER_ROWS, 1), F32)],
        compiler_params=_params("arbitrary", "arbitrary"),
        name="moe_router",
    )(x, sh, sc, g, wr_t, br)


def _expert_kernel(e1_ref, e2_ref, valid_ref, hn_hbm, srcp_ref, srca_ref, glo_ref, ghi_ref,
                   wu1_ref, wd1_ref, wu2_ref, wd2_ref, o_ref, xbuf, sem_g, *, tmx):
    del e1_ref, e2_ref
    j = pl.program_id(0)
    last = pl.num_programs(0) - 1

    @pl.when(valid_ref[j] == 0)
    def _():
        o_ref[...] = jnp.zeros_like(o_ref)

    @pl.when(valid_ref[j] == 1)
    def _():
        is_last = (j == last) | (valid_ref[jnp.minimum(j + 1, last)] == 0)
        x = _gathered_tile(j, is_last, hn_hbm, srcp_ref, srca_ref, xbuf, sem_g, tmx).astype(BF16)
        ri = lax.broadcasted_iota(I32, (tmx, tmx), 0)
        ci = lax.broadcasted_iota(I32, (tmx, tmx), 1)

        def column(row_ref):
            return jnp.sum(jnp.where(ri == ci, row_ref[0], 0.0), axis=1, keepdims=True)

        def ffn(wu_ref, wd_ref, gate):
            gu = jnp.dot(x, wu_ref[0], preferred_element_type=F32)
            g_ = gu[:, :D_EXPERT]
            u_ = gu[:, D_EXPERT:]
            act = ((g_ * _sigmoid(g_)) * u_).astype(BF16)
            return gate * jnp.dot(act, wd_ref[0], preferred_element_type=F32)

        _to_token_tiles(o_ref, (), ffn(wu1_ref, wd1_ref, column(glo_ref))
                        + ffn(wu2_ref, wd2_ref, column(ghi_ref)))


def _experts(hn, src, glo_s, ghi_s, w_up, w_down, e1, e2, valid, n_tiles, tmx):
    d = D_MODEL
    idx_block = (1, 1, tmx)
    at_tile = lambda j, e1, e2, v: (j, 0, 0)
    grid_spec = pltpu.PrefetchScalarGridSpec(
        num_scalar_prefetch=3,
        grid=(n_tiles,),
        in_specs=_gather_in_specs(tmx, lambda j, *_: j) + [
            pl.BlockSpec(idx_block, at_tile),
            pl.BlockSpec(idx_block, at_tile),
            pl.BlockSpec((1, d, 2 * D_EXPERT), lambda j, e1, e2, v: (e1[j], 0, 0)),
            pl.BlockSpec((1, D_EXPERT, d), lambda j, e1, e2, v: (e1[j], 0, 0)),
            pl.BlockSpec((1, d, 2 * D_EXPERT), lambda j, e1, e2, v: (e2[j], 0, 0)),
            pl.BlockSpec((1, D_EXPERT, d), lambda j, e1, e2, v: (e2[j], 0, 0)),
        ],
        out_specs=pl.BlockSpec((tmx * ROW_TILES, LANES), lambda j, e1, e2, v: (j, 0)),
        scratch_shapes=_gather_scratch(tmx),
    )
    return pl.pallas_call(
        functools.partial(_expert_kernel, tmx=tmx),
        grid_spec=grid_spec,
        out_shape=jax.ShapeDtypeStruct((n_tiles * tmx * ROW_TILES, LANES), F32),
        compiler_params=_params("arbitrary"),
        name="moe_experts",
    )(e1, e2, valid, hn, src, src, glo_s, ghi_s, w_up, w_down, w_up, w_down)


def _moe(x, sh, sc, p, *, tm, tmx):
    bsz, t_len, d = x.shape
    n = bsz * t_len
    hn, cls, rank, glo, ghi, cnt = _router(x, sh, sc, p["g_moe"], p["wr_t"], p["br"], tm=tm)
    cls = cls.reshape(n)
    rank = rank.reshape(n)
    counts = cnt[:N_CLASSES, 0].astype(I32)
    padded = ((counts + tmx - 1) // tmx) * tmx
    ends = jnp.cumsum(padded)
    offs = ends - padded
    pos = offs[cls] + rank
    n_tiles = n // tmx + N_CLASSES
    n_rows = n_tiles * tmx
    total = ends[N_CLASSES - 1]
    tile_start = jnp.arange(n_tiles, dtype=I32) * tmx
    valid = (tile_start < total).astype(I32)
    last_tile = jnp.maximum(total // tmx - 1, 0)
    tile_cls = jnp.minimum(
        jnp.sum((ends[None, :] <= jnp.minimum(tile_start, last_tile * tmx)[:, None]).astype(I32),
                axis=1), N_CLASSES - 1)
    grp = tile_cls // N_PAIRS
    pair = tile_cls % N_PAIRS
    e1 = grp * EXPERTS_PER_GROUP + jnp.asarray(PAIR_LO, I32)[pair]
    e2 = grp * EXPERTS_PER_GROUP + jnp.asarray(PAIR_HI, I32)[pair]
    token_info = jnp.stack([jnp.arange(n, dtype=I32).astype(F32), glo.reshape(n), ghi.reshape(n)],
                           axis=1)
    row_info = jnp.zeros((n_rows, 3), F32).at[pos].set(token_info, unique_indices=True)
    as_tiles = lambda a: a.reshape(n_tiles, 1, tmx)
    src_rows = _pad_gather_steps(as_tiles(row_info[:, 0].astype(I32) * ROW_TILES))
    ys = _experts(hn, src_rows, as_tiles(row_info[:, 1]), as_tiles(row_info[:, 2]),
                  p["w_up"], p["w_down"], e1, e2, valid, n_tiles, tmx)
    return ys, _pad_gather_steps((pos * ROW_TILES).reshape(n // tm, 1, tm))


AUG = 2 * HEAD_DIM
N_PIECES = 3
VT_ROWS = HEAD_DIM + 16
LOG2E = 1.0 / math.log(2.0)


def _bf16_pieces(x):
    pieces = []
    rest = x
    for _ in range(N_PIECES):
        piece = rest.astype(BF16).astype(F32)
        pieces.append(piece)
        rest = rest - piece
    return pieces


def _grid_step():
    n_steps = pl.num_programs(0) * pl.num_programs(1)
    return pl.program_id(0) * pl.num_programs(1) + pl.program_id(1), n_steps


def _post0_kernel(x_ref, y_hbm, posp_ref, posa_ref, g2_ref, shk_ref, sck_ref, kvg_ref,
                  wk_ref, wv_ref, wft_ref, bf_ref, shq_ref, scq_ref, gq_ref, wq_ref, *refs,
                  tm, seg, carry, prompt_layout):
    if prompt_layout:
        wvt_ref, x1_ref, k_ref, v_ref, lft_ref, fct_ref, qa_ref, ka_ref, vt_ref = refs[:9]
        scratch = refs[9:]
    else:
        x1_ref, k_ref, v_ref, lft_ref, fct_ref, q_ref, kb_ref, vb_ref = refs[:8]
        scratch = refs[8:]
    ybuf, sem_y = scratch[:2]
    scratch = scratch[2:]
    scale = HEAD_DIM ** -0.5
    step, n_steps = _grid_step()
    y = _gathered_tile(step, step == n_steps - 1, y_hbm, posp_ref, posa_ref, ybuf, sem_y, tm)
    x1 = x_ref[0] + g2_ref[0] * y
    x1_ref[0] = x1
    hk = _norm_mod(x1, kvg_ref[...], shk_ref[0], sck_ref[0])
    hkb = hk.astype(BF16)
    k = jnp.dot(hkb, wk_ref[...], preferred_element_type=F32)
    v = jnp.dot(hkb, wv_ref[...], preferred_element_type=F32)
    k_ref[0] = k
    v_ref[0] = v
    f = lax.dot_general(wft_ref[...], hk, NT_DIMS, precision=HIGHEST,
                        preferred_element_type=F32) + bf_ref[...]
    lf = _log_sigmoid(f)
    lft_ref[0] = lf
    col = lax.broadcasted_iota(I32, (1, tm), 1)
    cseg = col & (seg - 1)
    c = lf
    s = 1
    while s < seg:
        c = c + jnp.where(cseg >= s, pltpu.roll(c, s, axis=1), 0.0)
        s *= 2
    if carry:
        (f_scr,) = scratch

        @pl.when(pl.program_id(1) == 0)
        def _():
            f_scr[...] = jnp.zeros_like(f_scr)

        c = c + f_scr[...]
        f_scr[...] = c[:, tm - 1:tm]
    fct_ref[0] = c
    hqb = _norm_mod(x1, gq_ref[...], shq_ref[0], scq_ref[0]).astype(BF16)

    if not prompt_layout:
        q_ref[0] = (jnp.dot(hqb, wq_ref[...], preferred_element_type=F32) * scale).astype(BF16)
        kb_ref[0] = k.astype(BF16)
        vb_ref[0] = v.astype(BF16)
        return

    qt = lax.dot_general(wq_ref[...], hqb, NT_DIMS, preferred_element_type=F32) * (scale * LOG2E)
    vt = lax.dot_general(wvt_ref[...], hkb, NT_DIMS, preferred_element_type=F32)
    ones_rows = jnp.where(lax.broadcasted_iota(I32, (VT_ROWS - HEAD_DIM, tm), 0) == 0, 1.0, 0.0)
    v_rows = []
    for h in range(N_HEADS):
        v_rows += [vt[h * HEAD_DIM:(h + 1) * HEAD_DIM], ones_rows]
    vt_ref[0] = jnp.concatenate(v_rows, axis=0).astype(BF16)
    hi, mid, lo = _bf16_pieces(c * LOG2E)
    sub = lax.broadcasted_iota(I32, (SUBLANES, tm), 0)
    pad = jnp.zeros((HEAD_DIM - SUBLANES, tm), F32)
    q_rows, k_rows = [], []
    for h in range(N_HEADS):
        hs = slice(h, h + 1)
        eq = jnp.where(sub < 3, -1.0, jnp.where(sub == 3, hi[hs], jnp.where(
            sub == 4, mid[hs], jnp.where(sub == 5, lo[hs], 0.0))))
        ek = jnp.where(sub == 0, hi[hs], jnp.where(sub == 1, mid[hs], jnp.where(
            sub == 2, lo[hs], jnp.where(sub < 6, 1.0, 0.0))))
        q_rows += [qt[h * HEAD_DIM:(h + 1) * HEAD_DIM], eq, pad]
        k_rows += [ek, pad]
    qa_ref[0] = jnp.concatenate(q_rows, axis=0).astype(BF16)
    ekt = jnp.concatenate(k_rows, axis=0).astype(BF16)
    ri = lax.broadcasted_iota(I32, (tm, tm), 0)
    ci = lax.broadcasted_iota(I32, (tm, tm), 1)
    eye = jnp.where(ri == ci, 1.0, 0.0).astype(BF16)
    extra = lax.dot_general(eye, ekt, NT_DIMS, preferred_element_type=F32)
    k_cols = []
    for h in range(N_HEADS):
        hs = slice(h * HEAD_DIM, (h + 1) * HEAD_DIM)
        k_cols += [k[:, hs], extra[:, hs]]
    ka_ref[0] = jnp.concatenate(k_cols, axis=1).astype(BF16)


def _post0(x, y, y_rows, g2, shk, sck, shq, scq, p, *, tm, seg, carry, prompt_layout):
    bsz, t_len, d = x.shape
    nt = t_len // tm
    tok = pl.BlockSpec((1, tm, d), lambda b, t: (b, t, 0))
    headrow = pl.BlockSpec((1, N_HEADS, tm), lambda b, t: (b, 0, t))
    weights_kv = [p["kv_g"], p["w_k"], p["w_v"], p["w_f_t"], p["b_f"]]
    tok_f32 = jax.ShapeDtypeStruct((bsz, t_len, d), F32)
    head_f32 = jax.ShapeDtypeStruct((bsz, N_HEADS, t_len), F32)
    if prompt_layout:
        weights_q = [p["g_mix1"], p["w_q_t"], p["w_v_t"]]
        out_specs = [tok, tok, tok, headrow, headrow,
                     pl.BlockSpec((1, N_HEADS * AUG, tm), lambda b, t: (b, 0, t)),
                     pl.BlockSpec((1, tm, N_HEADS * AUG), lambda b, t: (b, t, 0)),
                     pl.BlockSpec((1, N_HEADS * VT_ROWS, tm), lambda b, t: (b, 0, t))]
        out_shape = [tok_f32] * 3 + [head_f32] * 2 + [
            jax.ShapeDtypeStruct((bsz, N_HEADS * AUG, t_len), BF16),
            jax.ShapeDtypeStruct((bsz, t_len, N_HEADS * AUG), BF16),
            jax.ShapeDtypeStruct((bsz, N_HEADS * VT_ROWS, t_len), BF16)]
    else:
        weights_q = [p["g_mix1"], p["w_q"]]
        out_specs = [tok, tok, tok, headrow, headrow, tok, tok, tok]
        out_shape = [tok_f32] * 3 + [head_f32] * 2 + [jax.ShapeDtypeStruct((bsz, t_len, d), BF16)] * 3
    return pl.pallas_call(
        functools.partial(_post0_kernel, tm=tm, seg=seg, carry=carry, prompt_layout=prompt_layout),
        grid=(bsz, t_len // tm),
        in_specs=[tok] + _gather_in_specs(tm, lambda b, t: b * nt + t)
                 + [_mod_spec(g2, tm), _mod_spec(shk, tm), _mod_spec(sck, tm)]
                 + [_full_spec(w) for w in weights_kv]
                 + [_mod_spec(shq, tm), _mod_spec(scq, tm)] + [_full_spec(w) for w in weights_q],
        out_specs=out_specs,
        out_shape=out_shape,
        scratch_shapes=_gather_scratch(tm) + ([pltpu.VMEM((N_HEADS, 1), F32)] if carry else []),
        compiler_params=_params("arbitrary", "arbitrary"),
        name="shared_kv_and_q",
    )(x, y, y_rows, y_rows, g2, shk, sck, *weights_kv, shq, scq, *weights_q)


def _fox_prompt_kernel(qi_ref, kj_ref, qa_ref, ka_ref, vt_ref, o_ref, m_scr, acc_scr, *, tq):
    p = pl.program_id(1)
    i = qi_ref[p]
    j = kj_ref[p]

    @pl.when(j == 0)
    def _():
        m_scr[...] = jnp.full_like(m_scr, -jnp.inf)
        acc_scr[...] = jnp.zeros_like(acc_scr)

    def step(masked):
        if masked:
            key = lax.broadcasted_iota(I32, (tq, tq), 0)
            qry = lax.broadcasted_iota(I32, (tq, tq), 1)
        for h in range(N_HEADS):
            hv = slice(h * VT_ROWS, (h + 1) * VT_ROWS)
            s = jnp.dot(ka_ref[0, :, h * AUG:(h + 1) * AUG], qa_ref[0, h * AUG:(h + 1) * AUG, :],
                        preferred_element_type=F32)
            if masked:
                s = jnp.where(key <= qry, s, -jnp.inf)
            m_old = m_scr[h]
            m_new = jnp.maximum(m_old, jnp.max(s, axis=0, keepdims=True))
            alpha = jnp.exp2(m_old - m_new)
            pr = jnp.exp2(s - m_new).astype(BF16)
            acc_scr[hv, :] = alpha * acc_scr[hv, :] + jnp.dot(vt_ref[0, hv, :], pr,
                                                              preferred_element_type=F32)
            m_scr[h] = m_new

    @pl.when(j < i)
    def _():
        step(False)

    @pl.when(j == i)
    def _():
        step(True)
        for h in range(N_HEADS):
            base = h * VT_ROWS
            out = acc_scr[base:base + HEAD_DIM, :] / acc_scr[base + HEAD_DIM:base + HEAD_DIM + 1, :]
            o_ref[0, :, h * HEAD_DIM:(h + 1) * HEAD_DIM] = jnp.transpose(out).astype(BF16)


def _fox_prompt(qa, ka, vt):
    bsz, _, t_len = vt.shape
    d = N_HEADS * HEAD_DIM
    tq = ATTN_TILE
    nb = t_len // tq
    pairs = [(i, j) for i in range(nb) for j in range(i + 1)]
    qi = jnp.asarray([a for a, _ in pairs], I32)
    kj = jnp.asarray([b for _, b in pairs], I32)
    grid_spec = pltpu.PrefetchScalarGridSpec(
        num_scalar_prefetch=2,
        grid=(bsz, len(pairs)),
        in_specs=[
            pl.BlockSpec((1, N_HEADS * AUG, tq), lambda b, p, qi, kj: (b, 0, qi[p])),
            pl.BlockSpec((1, tq, N_HEADS * AUG), lambda b, p, qi, kj: (b, kj[p], 0)),
            pl.BlockSpec((1, N_HEADS * VT_ROWS, tq), lambda b, p, qi, kj: (b, 0, kj[p])),
        ],
        out_specs=pl.BlockSpec((1, tq, d), lambda b, p, qi, kj: (b, qi[p], 0)),
        scratch_shapes=[pltpu.VMEM((N_HEADS, 1, tq), F32),
                        pltpu.VMEM((N_HEADS * VT_ROWS, tq), F32)],
    )
    return pl.pallas_call(
        functools.partial(_fox_prompt_kernel, tq=tq),
        grid_spec=grid_spec,
        out_shape=jax.ShapeDtypeStruct((bsz, t_len, d), BF16),
        compiler_params=_params("arbitrary", "arbitrary"),
        name="fox_prompt_attention",
    )(qi, kj, qa, ka, vt)


def _page_suffix(lf):
    lane = lax.broadcasted_iota(I32, lf.shape, 1)
    sub = lax.broadcasted_iota(I32, lf.shape, 0)
    x = lf
    sh = N_HEADS
    while sh < LANES:
        x = x + jnp.where(lane + sh < LANES, pltpu.roll(x, LANES - sh, axis=1), 0.0)
        sh *= 2
    y = jnp.where(lane < N_HEADS, x, 0.0)
    sh = N_HEADS
    while sh < LANES:
        y = y + pltpu.roll(y, sh, axis=1)
        sh *= 2
    z = y
    sh = 1
    while sh < SUBLANES:
        z = z + jnp.where(sub + sh < SUBLANES, pltpu.roll(z, SUBLANES - sh, axis=0), 0.0)
        sh *= 2
    return x + (z - y) - lf, z[0:1]


def _fox_decode_kernel(pt_ref, q_ref, kn_ref, vn_ref, cq_ref, ck_ref, *refs, pages):
    del pt_ref
    k_refs = refs[:pages]
    v_refs = refs[pages:2 * pages]
    lf_refs = refs[2 * pages:3 * pages]
    o_ref = refs[3 * pages]
    qm_scr, m_scr, l_scr, acc_scr, r_scr = refs[3 * pages + 1:]
    step = pl.program_id(1)
    n_tok = q_ref.shape[1]
    rows = n_tok * N_HEADS
    page_size = k_refs[0].shape[1]
    lane = lax.broadcasted_iota(I32, (rows, LANES), 1)
    row = lax.broadcasted_iota(I32, (rows, LANES), 0)
    row_head = row // n_tok
    row_tok = row - row_head * n_tok
    cq = cq_ref[0]

    def by_head(x):
        return jnp.concatenate(
            [x[:, h * HEAD_DIM:(h + 1) * HEAD_DIM] for h in range(N_HEADS)], axis=0)

    def attend(u, v_bf):
        m_old = m_scr[...]
        m_new = jnp.maximum(m_old, jnp.max(u, axis=1, keepdims=True) + cq)
        alpha = jnp.exp(m_old - m_new)
        pr = jnp.exp(u - (m_new - cq))
        l_scr[...] = alpha * l_scr[...] + jnp.sum(pr, axis=1, keepdims=True)
        acc_scr[...] = alpha * acc_scr[...] + jnp.dot(pr.astype(BF16), v_bf,
                                                      preferred_element_type=F32)
        m_scr[...] = m_new

    @pl.when(step == 0)
    def _():
        qm = by_head(q_ref[0].astype(F32)).astype(BF16)
        qm_scr[...] = qm
        m_scr[...] = jnp.full_like(m_scr, -jnp.inf)
        l_scr[...] = jnp.zeros_like(l_scr)
        acc_scr[...] = jnp.zeros_like(acc_scr)
        r_scr[...] = jnp.zeros_like(r_scr)
        pad = jnp.zeros((LANES - rows, HEAD_DIM), F32)
        kn = jnp.concatenate([by_head(kn_ref[0].astype(F32)), pad], axis=0).astype(BF16)
        vn = jnp.concatenate([by_head(vn_ref[0].astype(F32)), pad], axis=0).astype(BF16)
        s = lax.dot_general(qm, kn, NT_DIMS, preferred_element_type=F32)
        key_head = lane // n_tok
        key_tok = lane - key_head * n_tok
        u = jnp.where(key_head == row_head, jnp.where(key_tok <= row_tok, s - ck_ref[0], -jnp.inf),
                      -jnp.inf)
        attend(u, vn)

    own_head = (lane & (N_HEADS - 1)) == row_head
    qm = qm_scr[...]
    after_sum = r_scr[...]
    u_parts, v_parts = [], []
    for pg in range(pages):
        later, total = _page_suffix(lf_refs[pg][0])
        after = after_sum + later
        after_sum = after_sum + total
        k2 = k_refs[pg][0].reshape(page_size * N_HEADS, HEAD_DIM).astype(BF16)
        v_parts.append(v_refs[pg][0].reshape(page_size * N_HEADS, HEAD_DIM).astype(BF16))
        s = lax.dot_general(qm, k2, NT_DIMS, preferred_element_type=F32)
        u_parts += [jnp.where(own_head, s[:, c * LANES:(c + 1) * LANES] + after[c:c + 1], -jnp.inf)
                    for c in range(page_size * N_HEADS // LANES)]
    r_scr[...] = after_sum
    attend(jnp.concatenate(u_parts, axis=1), jnp.concatenate(v_parts, axis=0))

    @pl.when(step == pl.num_programs(1) - 1)
    def _():
        o = acc_scr[...] / l_scr[...]
        o_ref[0] = jnp.concatenate(
            [o[h * n_tok:(h + 1) * n_tok] for h in range(N_HEADS)], axis=1).astype(BF16)


def _fox_decode(q, k_new, v_new, cq, ck, cache_k, cache_v, cache_lf, page_table):
    bsz, n_tok, d = q.shape
    n_pages = page_table.shape[1]
    page_size = cache_k.shape[1]
    pages = PAGES_PER_STEP
    rows = n_tok * N_HEADS

    def page_map(pg, ndim):
        def index_map(b, s, pt):
            return (pt[b, n_pages - 1 - (s * pages + pg)],) + (0,) * (ndim - 1)
        return index_map

    seq = lambda b, s, pt: (b, 0, 0)
    in_specs = [pl.BlockSpec((1, n_tok, d), seq), pl.BlockSpec((1, n_tok, d), seq),
                pl.BlockSpec((1, n_tok, d), seq), pl.BlockSpec((1, rows, 1), seq),
                pl.BlockSpec((1, 1, LANES), seq)]
    kv_block = (1, page_size, N_HEADS, HEAD_DIM)
    in_specs += [pl.BlockSpec(kv_block, page_map(pg, 4)) for pg in range(pages)]
    in_specs += [pl.BlockSpec(kv_block, page_map(pg, 4)) for pg in range(pages)]
    in_specs += [pl.BlockSpec((1, SUBLANES, LANES), page_map(pg, 3)) for pg in range(pages)]
    grid_spec = pltpu.PrefetchScalarGridSpec(
        num_scalar_prefetch=1,
        grid=(bsz, n_pages // pages),
        in_specs=in_specs,
        out_specs=pl.BlockSpec((1, n_tok, d), seq),
        scratch_shapes=[pltpu.VMEM((rows, HEAD_DIM), BF16), pltpu.VMEM((rows, 1), F32),
                        pltpu.VMEM((rows, 1), F32), pltpu.VMEM((rows, HEAD_DIM), F32),
                        pltpu.VMEM((1, LANES), F32)],
    )
    return pl.pallas_call(
        functools.partial(_fox_decode_kernel, pages=pages),
        grid_spec=grid_spec,
        out_shape=jax.ShapeDtypeStruct((bsz, n_tok, d), BF16),
        compiler_params=_params("arbitrary", "arbitrary"),
        name="fox_decode_attention",
    )(page_table, q, k_new, v_new, cq, ck, *([cache_k] * pages), *([cache_v] * pages),
      *([cache_lf] * pages))


def _oproj_kernel(x_ref, o_ref, g1_ref, wo_ref, xo_ref):
    xo_ref[0] = x_ref[0] + g1_ref[0] * jnp.dot(o_ref[0], wo_ref[...], preferred_element_type=F32)


def _oproj(x, o, g1, w_o, *, tm):
    bsz, t_len, d = x.shape
    tok = pl.BlockSpec((1, tm, d), lambda b, t: (b, t, 0))
    return pl.pallas_call(
        _oproj_kernel,
        grid=(bsz, t_len // tm),
        in_specs=[tok, tok, _mod_spec(g1, tm), _full_spec(w_o)],
        out_specs=tok,
        out_shape=jax.ShapeDtypeStruct((bsz, t_len, d), F32),
        compiler_params=_params("arbitrary", "arbitrary"),
        name="attn_out_proj",
    )(x, o, g1, w_o)


def _final_kernel(x_ref, y_hbm, posp_ref, posa_ref, g2_ref, g_ref, o_ref, ybuf, sem_y):
    step, n_steps = _grid_step()
    y = _gathered_tile(step, step == n_steps - 1, y_hbm, posp_ref, posa_ref, ybuf, sem_y,
                       x_ref.shape[1])
    x = x_ref[0] + g2_ref[0] * y
    o_ref[0] = (x * lax.rsqrt(jnp.mean(x * x, axis=-1, keepdims=True) + EPS)) * g_ref[...]


def _final(x, y, y_rows, g2, g_final, *, tm):
    bsz, t_len, d = x.shape
    nt = t_len // tm
    tok = pl.BlockSpec((1, tm, d), lambda b, t: (b, t, 0))
    return pl.pallas_call(
        _final_kernel,
        grid=(bsz, nt),
        in_specs=[tok] + _gather_in_specs(tm, lambda b, t: b * nt + t)
                 + [_mod_spec(g2, tm), _full_spec(g_final)],
        out_specs=tok,
        out_shape=jax.ShapeDtypeStruct((bsz, t_len, d), F32),
        scratch_shapes=_gather_scratch(tm),
        compiler_params=_params("arbitrary", "arbitrary"),
        name="final_norm",
    )(x, y, y_rows, y_rows, g2, g_final)


def _trunk(x, mods0, mods1, kvmods, prev, h0, w, attend, *, tm, tmx, seg, carry):
    sh1, sc1, g1, sh2, sc2, g2 = mods0
    x, h_out, conv_out = _lru_layer(x, sh1, sc1, g1, w["lru"], prev, h0, tm=tm, seg=seg, carry=carry)
    y, y_rows = _moe(x, sh2, sc2, w["moe0"], tm=tm, tmx=tmx)
    sh1b, sc1b, g1b, sh2b, sc2b, g2b = mods1
    shk, sck = kvmods
    x, k, v, lft, fct, *operands = _post0(x, y, y_rows, g2, shk, sck, sh1b, sc1b, w["post0"],
                                          tm=tm, seg=seg, carry=carry, prompt_layout=carry)
    o = attend(fct, *operands)
    x = _oproj(x, o, g1b, w["w_o"], tm=tm)
    y, y_rows = _moe(x, sh2b, sc2b, w["moe1"], tm=tm, tmx=tmx)
    y_out = _final(x, y, y_rows, g2b, w["g_final"], tm=tm)
    return y_out, h_out, conv_out, k, v, lft


def kernel(x_prompt, x_sample, state_h, state_conv, cache_k, cache_v, cache_logf, page_table,
           c_prompt, c_sample, w_mod, b_mod, g_mix, g_moe, lru_w_in, lru_conv_w, lru_conv_b,
           lru_w_a, lru_b_a, lru_w_x, lru_b_x, lru_lambda, lru_w_out, kv_g, kv_w_mod, kv_b_mod,
           kv_w, kv_b_f, attn_w_q, attn_w_o, moe_w_grp, moe_b_grp, moe_w_exp, moe_b_exp,
           moe_w_up, moe_w_down, g_final):
    d = D_MODEL
    bp, tp, _ = x_prompt.shape
    bs, ts, _ = x_sample.shape
    hd = N_HEADS * HEAD_DIM
    row = lambda a: a.reshape(1, -1)

    def moe_weights(layer):
        wr_t = jnp.concatenate(
            [moe_w_grp[layer].T, moe_w_exp[layer].T,
             jnp.zeros((ROUTER_ROWS - N_GROUPS - N_EXPERTS, d), F32)], axis=0)
        br = jnp.concatenate(
            [moe_b_grp[layer], moe_b_exp[layer],
             jnp.zeros((ROUTER_ROWS - N_GROUPS - N_EXPERTS,), F32)]).reshape(ROUTER_ROWS, 1)
        return dict(g_moe=row(g_moe[layer]), wr_t=wr_t, br=br,
                    w_up=moe_w_up[layer].astype(BF16), w_down=moe_w_down[layer].astype(BF16))

    w = dict(
        lru=dict(g_mix=row(g_mix[0]), w_in=lru_w_in[0].astype(BF16), conv_w=lru_conv_w[0],
                 conv_b=row(lru_conv_b[0]), w_a=lru_w_a[0].astype(BF16), b_a=row(lru_b_a[0]),
                 w_x=lru_w_x[0].astype(BF16), b_x=row(lru_b_x[0]), lam=row(lru_lambda[0]),
                 w_out=lru_w_out[0].astype(BF16)),
        moe0=moe_weights(0),
        moe1=moe_weights(1),
        post0=dict(kv_g=row(kv_g), w_k=kv_w[:, :hd].astype(BF16),
                   w_v=kv_w[:, hd:2 * hd].astype(BF16), w_f_t=kv_w[:, 2 * hd:].T,
                   w_v_t=kv_w[:, hd:2 * hd].T.astype(BF16), b_f=kv_b_f.reshape(N_HEADS, 1),
                   g_mix1=row(g_mix[1]), w_q=attn_w_q[0].astype(BF16),
                   w_q_t=attn_w_q[0].T.astype(BF16)),
        w_o=attn_w_o[0].astype(BF16),
        g_final=row(g_final),
    )

    n_c = bp + bs
    c_rows = -(-n_c // SUBLANES) * SUBLANES
    c_all = jnp.concatenate([c_prompt, c_sample, jnp.zeros((c_rows - n_c, d), F32)], axis=0)
    mod_l0 = _mm_bias(c_all, w_mod[0], b_mod[0])
    mod_l1 = _mm_bias(c_all, w_mod[1], b_mod[1])
    mod_kv = _mm_bias(c_all, kv_w_mod, kv_b_mod)

    def prompt_mods(m, parts):
        return [a.reshape(bp, 1, d) for a in jnp.split(m[:bp], parts, axis=-1)]

    def sample_mods(m, parts):
        return [jnp.repeat(a, ts, axis=0).reshape(1, bs * ts, d)
                for a in jnp.split(m[bp:n_c], parts, axis=-1)]

    prev_p = jnp.zeros((bp, SUBLANES, D_RNN), F32)
    h0_p = jnp.zeros((bp, 1, D_RNN), F32)
    attend_p = lambda fct, qa, ka, vt: _fox_prompt(qa, ka, vt)
    y_p, h_p, conv_p, k_p, v_p, lft_p = _trunk(
        x_prompt, prompt_mods(mod_l0, 6), prompt_mods(mod_l1, 6), prompt_mods(mod_kv, 2),
        prev_p, h0_p, w, attend_p, tm=TOKEN_TILE, tmx=EXPERT_TILE, seg=TOKEN_TILE, carry=True)

    n_s = bs * ts
    prev_s = jnp.pad(state_conv[0], ((0, 0), (ts - (CONV_WIDTH - 1), 0), (0, 0))).reshape(1, n_s, D_RNN)
    h0_s = jnp.repeat(state_h[0], ts, axis=0).reshape(1, n_s, D_RNN)
    n_pool, page_size = cache_k.shape[0], cache_k.shape[1]
    cache_lf = cache_logf.reshape(n_pool, SUBLANES, page_size * N_HEADS // SUBLANES)

    def attend_s(fct, q, kb, vb):
        c_new = jnp.transpose(fct.reshape(N_HEADS, bs, ts), (1, 0, 2)).reshape(bs, N_HEADS * ts)
        cq = c_new.reshape(bs, N_HEADS * ts, 1)
        ck = jnp.pad(c_new, ((0, 0), (0, LANES - N_HEADS * ts))).reshape(bs, 1, LANES)
        o = _fox_decode(q.reshape(bs, ts, d), kb.reshape(bs, ts, d), vb.reshape(bs, ts, d), cq, ck,
                        cache_k, cache_v, cache_lf, page_table)
        return o.reshape(1, n_s, d)

    y_s, h_s, conv_s, k_s, v_s, lft_s = _trunk(
        x_sample.reshape(1, n_s, d), sample_mods(mod_l0, 6), sample_mods(mod_l1, 6),
        sample_mods(mod_kv, 2), prev_s, h0_s, w, attend_s, tm=n_s, tmx=SAMPLE_EXPERT_TILE, seg=ts,
        carry=False)

    n_conv = CONV_WIDTH - 1
    return (
        y_p,
        y_s.reshape(bs, ts, d),
        h_p.reshape(1, bp, D_RNN),
        conv_p[:, SUBLANES - n_conv:].reshape(1, bp, n_conv, D_RNN),
        k_p.reshape(bp, tp, N_HEADS, HEAD_DIM),
        v_p.reshape(bp, tp, N_HEADS, HEAD_DIM),
        jnp.transpose(lft_p, (0, 2, 1)),
        h_s.reshape(bs, ts, D_RNN)[:, ts - 1].reshape(1, bs, D_RNN),
        conv_s.reshape(bs, ts, D_RNN)[:, ts - n_conv:].reshape(1, bs, n_conv, D_RNN),
        k_s.reshape(bs, ts, N_HEADS, HEAD_DIM),
        v_s.reshape(bs, ts, N_HEADS, HEAD_DIM),
        jnp.transpose(lft_s.reshape(N_HEADS, bs, ts), (1, 2, 0)),
    )
```
